```python
import functools
import math
import jax
import jax.numpy as jnp
from jax import lax
import numpy as np

D_MODEL = 2048
BATCH = 1
SEQ = 8192
DEPTH = 1
DEC_BATCH = 128
DEC_SEQ = 4
PAST_LEN = 2048
PAGE_SIZE = 128

HEAD_DIM = 128
GDN_HEADS = D_MODEL // 256
GDN_DK = HEAD_DIM
GDN_DV = HEAD_DIM
GDN_CONV = 4
GDN_CHUNK = 64
MOBA_HEADS = D_MODEL // 512
MOBA_BLOCK = 256
MOBA_TOPK = 3
MOBA_QBLOCK = 128
MEM_HEADS = 4
N_MEM = 256
N_BUCKETS = 32
MAX_DISTANCE = 128
N_EXPERTS = 32
TOP_K = 4
D_FF = D_MODEL
SWIGLU_LIMIT = 7.0
SWIGLU_ALPHA = 1.702
MOE_BLOCK = 128
EPS = 1e-6
NEG_INF = -1e30

GDN_QK = GDN_HEADS * GDN_DK
GDN_V = GDN_HEADS * GDN_DV
GDN_CONV_DIM = 2 * GDN_QK + GDN_V
MOBA_W = MOBA_HEADS * HEAD_DIM
MEM_W = MEM_HEADS * HEAD_DIM
MIX_W = GDN_V + MOBA_W + MEM_W
OFF_Z = GDN_CONV_DIM
OFF_BETA = OFF_Z + GDN_V
OFF_A = OFF_BETA + GDN_HEADS
OFF_MOBA = OFF_A + GDN_HEADS
OFF_MEM = OFF_MOBA + 3 * MOBA_W
IN_W = OFF_MEM + MEM_W
IN_SPLITS = [OFF_Z, OFF_BETA, OFF_A, OFF_MOBA, OFF_MEM]

kernel_name = 'hymba_gdn_moba_memxattn_moe_step'


def rmsnorm(x, g):
    xf = x.astype(jnp.float32)
    y = xf * lax.rsqrt(jnp.mean(xf * xf, axis=-1, keepdims=True) + EPS)
    return (y * g.astype(jnp.float32)).astype(x.dtype)


def l2norm(x):
    xf = x.astype(jnp.float32)
    return xf * lax.rsqrt(jnp.sum(xf * xf, axis=-1, keepdims=True) + EPS)


def t5_bucket(dist):
    exact = N_BUCKETS // 2
    d = jnp.maximum(dist, 0)
    large = exact + (jnp.log(jnp.maximum(d, 1).astype(jnp.float32) / exact)
                     / math.log(MAX_DISTANCE / exact) * (N_BUCKETS - exact)).astype(jnp.int32)
    return jnp.where(d < exact, d, jnp.minimum(large, N_BUCKETS - 1))


def causal_conv(x, buf, w):
    t = x.shape[1]
    xp = jnp.concatenate([buf.astype(x.dtype), x], axis=1)
    y = sum(xp[:, j:j + t] * w[j] for j in range(GDN_CONV))
    return jax.nn.silu(y), xp[:, t:]


def gdn_chunked(q, k, v, g, beta, s0):
    bsz, t = q.shape[0], q.shape[1]
    c = min(GDN_CHUNK, t)
    pad = (-t) % c
    n = (t + pad) // c

    def to_chunks(a):
        a = jnp.pad(a, ((0, 0), (0, pad)) + ((0, 0),) * (a.ndim - 2))
        a = a.reshape((bsz, n, c) + a.shape[2:])
        return jnp.swapaxes(jnp.moveaxis(a, 1, 0), 2, 3)

    idx = jnp.arange(c)
    lower_incl = idx[:, None] >= idx[None, :]
    lower_strict = idx[:, None] > idx[None, :]
    eye = jnp.eye(c, dtype=jnp.float32)

    def step(s, xs):
        qc, kc, vc, gc, bc = xs
        gcum = jnp.cumsum(gc, axis=-1)
        decay = jnp.exp(jnp.where(lower_incl, gcum[..., :, None] - gcum[..., None, :], -jnp.inf))
        kbeta = kc * bc[..., None]
        lmat = jnp.einsum('bhik,bhjk->bhij', kbeta, kc) * jnp.where(lower_strict, decay, 0.0)
        rhs = jnp.concatenate([vc * bc[..., None], kbeta * jnp.exp(gcum)[..., None]], axis=-1)
        sol = lax.linalg.triangular_solve(eye + lmat, rhs, left_side=True, lower=True, unit_diagonal=True)
        u, w = sol[..., :GDN_DV], sol[..., GDN_DV:]
        v_new = u - jnp.einsum('bhck,bhkv->bhcv', w, s)
        intra = jnp.einsum('bhik,bhjk->bhij', qc, kc) * decay
        o = (jnp.einsum('bhck,bhkv->bhcv', qc * jnp.exp(gcum)[..., None], s)
             + jnp.einsum('bhij,bhjv->bhiv', intra, v_new))
        g_last = gcum[..., -1]
        s_new = (s * jnp.exp(g_last)[..., None, None]
                 + jnp.einsum('bhck,bhcv->bhkv', kc * jnp.exp(g_last[..., None] - gcum)[..., None], v_new))
        return s_new, o

    s_fin, o = lax.scan(step, s0, tuple(to_chunks(a) for a in (q, k, v, g, beta)))
    o = jnp.moveaxis(jnp.swapaxes(o, 2, 3), 0, 1).reshape(bsz, n * c, GDN_HEADS, GDN_DV)[:, :t]
    return o, s_fin


def gdn_mixer(qkv_in, z, beta_in, a_in, conv_buf, state, conv_w, a_log, dt_bias, o_norm):
    bsz, t, _ = qkv_in.shape
    qkv, conv_new = causal_conv(qkv_in, conv_buf, conv_w)
    q, k, v = jnp.split(qkv, [GDN_QK, 2 * GDN_QK], axis=-1)
    q = l2norm(q.reshape(bsz, t, GDN_HEADS, GDN_DK)) * GDN_DK ** -0.5
    k = l2norm(k.reshape(bsz, t, GDN_HEADS, GDN_DK))
    v = v.reshape(bsz, t, GDN_HEADS, GDN_DV).astype(jnp.float32)
    beta = jax.nn.sigmoid(beta_in.astype(jnp.float32))
    g = -jnp.exp(a_log.astype(jnp.float32)) * jax.nn.softplus(a_in.astype(jnp.float32) + dt_bias.astype(jnp.float32))
    o, s_new = gdn_chunked(q, k, v, g, beta, state.astype(jnp.float32))
    o = rmsnorm(o, o_norm) * jax.nn.silu(z.reshape(bsz, t, GDN_HEADS, GDN_DV).astype(jnp.float32))
    return o.reshape(bsz, t, GDN_V).astype(qkv_in.dtype), s_new.astype(state.dtype), conv_new


def moba_sequence(q, k, v, q_pos, rel_bias):
    length = k.shape[0]
    nb = -(-length // MOBA_BLOCK)
    kpad = nb * MOBA_BLOCK - length
    kb = jnp.pad(k, ((0, kpad), (0, 0), (0, 0))).reshape(nb, MOBA_BLOCK, MOBA_HEADS, HEAD_DIM)
    vb = jnp.pad(v, ((0, kpad), (0, 0), (0, 0))).reshape(nb, MOBA_BLOCK, MOBA_HEADS, HEAD_DIM)
    kmean = jnp.mean(kb.astype(jnp.float32), axis=1)
    heads = jnp.arange(MOBA_HEADS)
    offs = jnp.arange(MOBA_BLOCK, dtype=jnp.int32)
    blk_ids = jnp.arange(nb)
    tq = q.shape[0]
    c = min(MOBA_QBLOCK, tq)
    qpad = (-tq) % c

    def attend(args):
        qc, pc = args
        own = pc // MOBA_BLOCK
        s = jnp.einsum('chd,nhd->chn', qc.astype(jnp.float32), kmean)
        s = jnp.where((blk_ids[None, :] < own[:, None])[:, None, :], s, -jnp.inf)
        if nb < MOBA_TOPK:
            s = jnp.pad(s, ((0, 0), (0, 0), (0, MOBA_TOPK - nb)), constant_values=-jnp.inf)
        _, sel = lax.top_k(s, MOBA_TOPK)
        own_b = jnp.broadcast_to(own[:, None, None], (c, MOBA_HEADS, 1)).astype(sel.dtype)
        valid = jnp.concatenate([sel < own_b, jnp.ones_like(own_b, dtype=bool)], axis=-1)
        blocks = jnp.concatenate([jnp.minimum(sel, nb - 1), own_b], axis=-1)
        kg = kb[blocks, :, heads[None, :, None], :]
        vg = vb[blocks, :, heads[None, :, None], :]
        dist = pc[:, None, None, None] - (blocks[..., None] * MOBA_BLOCK + offs)
        bias = rel_bias[t5_bucket(dist), heads[None, :, None, None]].astype(jnp.float32)
        logits = jnp.einsum('chd,chjsd->chjs', qc, kg).astype(jnp.float32) * HEAD_DIM ** -0.5 + bias
        logits = jnp.where(valid[..., None] & (dist >= 0), logits, NEG_INF)
        p = jax.nn.softmax(logits.reshape(c, MOBA_HEADS, -1), axis=-1).reshape(logits.shape)
        return jnp.einsum('chjs,chjsd->chd', p.astype(vg.dtype), vg)

    if qpad:
        q = jnp.pad(q, ((0, qpad), (0, 0), (0, 0)))
        q_pos = jnp.concatenate([q_pos, jnp.broadcast_to(q_pos[-1:], (qpad,))])
    nq = (tq + qpad) // c
    if nq == 1:
        out = attend((q, q_pos))
    else:
        out = lax.map(attend, (q.reshape(nq, c, MOBA_HEADS, HEAD_DIM), q_pos.reshape(nq, c)))
    return out.reshape(nq * c, MOBA_HEADS, HEAD_DIM)[:tq]


def moba_contiguous(q, k, v, pos, rel_bias):
    return lax.map(lambda a: moba_sequence(a[0], a[1], a[2], pos, rel_bias), (q, k, v))


def moba_paged(q, k, v, pool_k, pool_v, page_table, pos, rel_bias):
    def one(a):
        qb, kb, vb, pt = a
        kf = jnp.concatenate([pool_k[pt].reshape(-1, MOBA_HEADS, HEAD_DIM).astype(kb.dtype), kb], axis=0)
        vf = jnp.concatenate([pool_v[pt].reshape(-1, MOBA_HEADS, HEAD_DIM).astype(vb.dtype), vb], axis=0)
        return moba_sequence(qb, kf, vf, pos, rel_bias)
    return lax.map(one, (q, k, v, page_table))


def memory_kv(mem, g, w_kv, k_norm):
    bsz, m, _ = mem.shape
    kv = rmsnorm(mem, g) @ w_kv
    k, v = jnp.split(kv, 2, axis=-1)
    k = rmsnorm(k.reshape(bsz, m, MEM_HEADS, HEAD_DIM), k_norm)
    return k, v.reshape(bsz, m, MEM_HEADS, HEAD_DIM)


def mem_attend(q, k, v):
    logits = jnp.einsum('bthd,bmhd->bhtm', q, k.astype(q.dtype)).astype(jnp.float32) * HEAD_DIM ** -0.5
    p = jax.nn.softmax(logits, axis=-1)
    return jnp.einsum('bhtm,bmhd->bthd', p.astype(v.dtype), v)


def moe_ffn(h, router_w, router_b, w_gu, b_gu, w_dn, b_dn):
    n_tok, d = h.shape
    logits = (h @ router_w).astype(jnp.float32) + router_b.astype(jnp.float32)
    top_v, top_i = lax.top_k(logits, TOP_K)
    top_w = jax.nn.softmax(top_v, axis=-1)
    a = n_tok * TOP_K
    e = top_i.reshape(a).astype(jnp.int32)
    order = jnp.argsort(e)
    e_s = e[order]
    tok_s = (order // TOP_K).astype(jnp.int32)
    counts = jnp.zeros((N_EXPERTS,), jnp.int32).at[e].add(1)
    cap = (counts + MOE_BLOCK - 1) // MOE_BLOCK * MOE_BLOCK
    start = jnp.cumsum(counts) - counts
    cap_end = jnp.cumsum(cap)
    pstart = cap_end - cap
    dest = pstart[e_s] + jnp.arange(a, dtype=jnp.int32) - start[e_s]
    n_blocks = -(-(a + N_EXPERTS * (MOE_BLOCK - 1)) // MOE_BLOCK)
    rows = n_blocks * MOE_BLOCK
    row_tok = jnp.full((rows,), n_tok, jnp.int32).at[dest].set(tok_s)
    row_w = jnp.zeros((rows,), jnp.float32).at[dest].set(top_w.reshape(a)[order])
    block_e = jnp.minimum(jnp.searchsorted(cap_end, jnp.arange(n_blocks, dtype=jnp.int32) * MOE_BLOCK, side='right'),
                          N_EXPERTS - 1).astype(jnp.int32)
    h_pad = jnp.concatenate([h, jnp.zeros((1, d), h.dtype)], axis=0)
    xb = h_pad[row_tok].reshape(n_blocks, MOE_BLOCK, d)

    def expert_block(args):
        xe, ei = args
        gu = xe @ w_gu[ei] + b_gu[ei]
        gate, up = jnp.split(gu, 2, axis=-1)
        gate = jnp.minimum(gate, SWIGLU_LIMIT)
        up = jnp.clip(up, -SWIGLU_LIMIT, SWIGLU_LIMIT)
        act = gate * jax.nn.sigmoid(SWIGLU_ALPHA * gate) * (up + 1.0)
        return act @ w_dn[ei] + b_dn[ei]

    yb = lax.map(expert_block, (xb, block_e)).reshape(rows, d)
    out = jnp.zeros((n_tok + 1, d), jnp.float32).at[row_tok].add(yb.astype(jnp.float32) * row_w[:, None])
    return out[:n_tok].astype(h.dtype)


def hybrid_layer(x, conv_buf, gdn_state, mem_k, mem_v, moba_fn, lw):
    (norm_mix, w_in, conv_w, a_log, dt_bias, gdn_o_norm, moba_q_norm, moba_k_norm,
     mem_q_norm, w_out, norm_ffn, router_w, router_b, w_gu, b_gu, w_dn, b_dn) = lw
    bsz, t, _ = x.shape
    proj = rmsnorm(x, norm_mix) @ w_in
    qkv_in, z, beta_in, a_in, moba_qkv, mem_q = jnp.split(proj, IN_SPLITS, axis=-1)
    o_gdn, gdn_new, conv_new = gdn_mixer(qkv_in, z, beta_in, a_in, conv_buf, gdn_state,
                                         conv_w, a_log, dt_bias, gdn_o_norm)
    mq, mk, mv = [p.reshape(bsz, t, MOBA_HEADS, HEAD_DIM) for p in jnp.split(moba_qkv, 3, axis=-1)]
    mq = rmsnorm(mq, moba_q_norm)
    mk = rmsnorm(mk, moba_k_norm)
    o_moba = moba_fn(mq, mk, mv).reshape(bsz, t, MOBA_W)
    cq = rmsnorm(mem_q.reshape(bsz, t, MEM_HEADS, HEAD_DIM), mem_q_norm)
    o_mem = mem_attend(cq, mem_k, mem_v).reshape(bsz, t, MEM_W)
    x = x + jnp.concatenate([o_gdn, o_moba.astype(x.dtype), o_mem.astype(x.dtype)], axis=-1) @ w_out
    y = x + moe_ffn(rmsnorm(x, norm_ffn).reshape(bsz * t, -1), router_w, router_b,
                    w_gu, b_gu, w_dn, b_dn).reshape(x.shape)
    return y, mk, mv, gdn_new, conv_new


def setup_inputs(seed: int = 0) -> dict:
    key = jax.random.key(seed)
    ks = jax.random.split(key, 32)
    f32 = jnp.float32
    n_pages = PAST_LEN // PAGE_SIZE
    n_phys = (DEC_BATCH * n_pages * 5 + 3) // 4

    def nrm(k, shape, scale):
        return jax.random.normal(k, shape, f32) * scale

    def gain(k, shape):
        return 1.0 + 0.02 * jax.random.normal(k, shape, f32)

    page_table = jax.random.permutation(ks[8], n_phys)[:DEC_BATCH * n_pages].reshape(DEC_BATCH, n_pages).astype(jnp.int32)
    dt = jnp.exp(jax.random.uniform(ks[15], (DEPTH, GDN_HEADS), f32, math.log(1e-3), math.log(1e-1)))
    return {
        'x_prompt': nrm(ks[0], (BATCH, SEQ, D_MODEL), 1.0),
        'x_sample': nrm(ks[1], (DEC_BATCH, DEC_SEQ, D_MODEL), 1.0),
        'cache_moba_k': nrm(ks[2], (DEPTH, n_phys, PAGE_SIZE, MOBA_HEADS, HEAD_DIM), 1.0),
        'cache_moba_v': nrm(ks[3], (DEPTH, n_phys, PAGE_SIZE, MOBA_HEADS, HEAD_DIM), 1.0),
        'state_gdn': nrm(ks[4], (DEPTH, DEC_BATCH, GDN_HEADS, GDN_DK, GDN_DV), 0.5),
        'state_gdn_conv': nrm(ks[5], (DEPTH, DEC_BATCH, GDN_CONV - 1, GDN_CONV_DIM), 1.0),
        'cache_mem_k': nrm(ks[6], (DEPTH, DEC_BATCH, N_MEM, MEM_HEADS, HEAD_DIM), 1.0),
        'cache_mem_v': nrm(ks[7], (DEPTH, DEC_BATCH, N_MEM, MEM_HEADS, HEAD_DIM), 1.0),
        'page_table': page_table,
        'mem_prompt': nrm(ks[9], (BATCH, N_MEM, D_MODEL), 1.0),
        'rel_bias': nrm(ks[10], (N_BUCKETS, MOBA_HEADS), 0.5),
        'norm_mix': gain(ks[11], (DEPTH, D_MODEL)),
        'w_in': nrm(ks[12], (DEPTH, D_MODEL, IN_W), D_MODEL ** -0.5),
        'conv_w': nrm(ks[13], (DEPTH, GDN_CONV, GDN_CONV_DIM), GDN_CONV ** -0.5),
        'a_log': jnp.log(jax.random.uniform(ks[14], (DEPTH, GDN_HEADS), f32, 1.0, 16.0)),
        'dt_bias': jnp.log(jnp.expm1(dt)),
        'gdn_o_norm': gain(ks[16], (DEPTH, GDN_DV)),
        'moba_q_norm': gain(ks[17], (DEPTH, HEAD_DIM)),
        'moba_k_norm': gain(ks[18], (DEPTH, HEAD_DIM)),
        'mem_q_norm': gain(ks[19], (DEPTH, HEAD_DIM)),
        'mem_norm': gain(ks[20], (DEPTH, D_MODEL)),
        'w_mem_kv': nrm(ks[21], (DEPTH, D_MODEL, 2 * MEM_W), D_MODEL ** -0.5),
        'mem_k_norm': gain(ks[22], (DEPTH, HEAD_DIM)),
        'w_out': nrm(ks[23], (DEPTH, MIX_W, D_MODEL), MIX_W ** -0.5),
        'norm_ffn': gain(ks[24], (DEPTH, D_MODEL)),
        'router_w': nrm(ks[25], (DEPTH, D_MODEL, N_EXPERTS), D_MODEL ** -0.5),
        'router_b': nrm(ks[26], (DEPTH, N_EXPERTS), 0.01),
        'w_gu': nrm(ks[27], (DEPTH, N_EXPERTS, D_MODEL, 2 * D_FF), D_MODEL ** -0.5),
        'b_gu': nrm(ks[28], (DEPTH, N_EXPERTS, 2 * D_FF), 0.01),
        'w_dn': nrm(ks[29], (DEPTH, N_EXPERTS, D_FF, D_MODEL), D_FF ** -0.5),
        'b_dn': nrm(ks[30], (DEPTH, N_EXPERTS, D_MODEL), 0.01),
    }


def reference(x_prompt, x_sample, cache_moba_k, cache_moba_v, state_gdn, state_gdn_conv, cache_mem_k,
              cache_mem_v, page_table, mem_prompt, rel_bias, norm_mix, w_in, conv_w, a_log, dt_bias,
              gdn_o_norm, moba_q_norm, moba_k_norm, mem_q_norm, mem_norm, w_mem_kv, mem_k_norm, w_out,
              norm_ffn, router_w, router_b, w_gu, b_gu, w_dn, b_dn):
    bsz, t, _ = x_prompt.shape
    ds = x_sample.shape[1]
    past = page_table.shape[1] * cache_moba_k.shape[2]
    pos_p = jnp.arange(t, dtype=jnp.int32)
    pos_s = past + jnp.arange(ds, dtype=jnp.int32)
    hp, hs = x_prompt, x_sample
    p_k, p_v, p_gdn, p_conv, p_mk, p_mv = [], [], [], [], [], []
    s_k, s_v, s_gdn, s_conv = [], [], [], []
    for l in range(DEPTH):
        lw = (norm_mix[l], w_in[l], conv_w[l], a_log[l], dt_bias[l], gdn_o_norm[l], moba_q_norm[l],
              moba_k_norm[l], mem_q_norm[l], w_out[l], norm_ffn[l], router_w[l], router_b[l],
              w_gu[l], b_gu[l], w_dn[l], b_dn[l])
        mem_k, mem_v = memory_kv(mem_prompt, mem_norm[l], w_mem_kv[l], mem_k_norm[l])
        moba_p = functools.partial(moba_contiguous, pos=pos_p, rel_bias=rel_bias)
        hp, k_new, v_new, g_new, c_new = hybrid_layer(
            hp, jnp.zeros((bsz, GDN_CONV - 1, GDN_CONV_DIM), hp.dtype),
            jnp.zeros((bsz, GDN_HEADS, GDN_DK, GDN_DV), jnp.float32), mem_k, mem_v, moba_p, lw)
        p_k.append(k_new)
        p_v.append(v_new)
        p_gdn.append(g_new)
        p_conv.append(c_new)
        p_mk.append(mem_k)
        p_mv.append(mem_v)
        moba_s = functools.partial(moba_paged, pool_k=cache_moba_k[l], pool_v=cache_moba_v[l],
                                   page_table=page_table, pos=pos_s, rel_bias=rel_bias)
        hs, k_new, v_new, g_new, c_new = hybrid_layer(
            hs, state_gdn_conv[l], state_gdn[l], cache_mem_k[l], cache_mem_v[l], moba_s, lw)
        s_k.append(k_new)
        s_v.append(v_new)
        s_gdn.append(g_new)
        s_conv.append(c_new)
    return (hp, hs, jnp.stack(p_k), jnp.stack(p_v), jnp.stack(p_gdn), jnp.stack(p_conv), jnp.stack(p_mk),
            jnp.stack(p_mv), jnp.stack(s_k), jnp.stack(s_v), jnp.stack(s_gdn), jnp.stack(s_conv))
```

```python
import functools
import math

import numpy as np
import jax
import jax.numpy as jnp
from jax import lax
from jax.experimental import pallas as pl
from jax.experimental.pallas import tpu as pltpu

f32 = jnp.float32
bf16 = jnp.bfloat16
i32 = jnp.int32
HIGHEST = lax.Precision.HIGHEST

LANES = 128
SUBLANES = 8
VMEM_LIMIT = 56 * 1024 * 1024

D_MODEL = 2048
HEAD_DIM = 128
GDN_HEADS = 8
GDN_CONV = 4
GDN_CHUNK = 64
GDN_QK = GDN_HEADS * HEAD_DIM
GDN_V = GDN_HEADS * HEAD_DIM
GDN_CONV_DIM = 2 * GDN_QK + GDN_V
MOBA_HEADS = 4
MOBA_BLOCK = 256
MOBA_TOPK = 3
MOBA_W = MOBA_HEADS * HEAD_DIM
MEM_HEADS = 4
MEM_W = MEM_HEADS * HEAD_DIM
N_BUCKETS = 32
MAX_DISTANCE = 128
N_EXPERTS = 32
TOP_K = 4
D_FF = D_MODEL
SWIGLU_LIMIT = 7.0
SWIGLU_ALPHA = 1.702
EPS = 1e-6
NEG_INF = -1e30

A_W = GDN_CONV_DIM + GDN_V
B_MQ, B_MK, B_CQ, B_MV, B_BA = 0, MOBA_W, 2 * MOBA_W, 3 * MOBA_W, 4 * MOBA_W
B_W = B_BA + LANES
B_NORM_GROUPS = 3 * MOBA_HEADS


def _params(*sem):
    return pltpu.CompilerParams(dimension_semantics=sem, vmem_limit_bytes=VMEM_LIMIT)


def _dot(a, b, **kw):
    return jnp.dot(a, b, preferred_element_type=f32, **kw)


def _dot_nt(a, b, **kw):
    return lax.dot_general(a, b, (((1,), (1,)), ((), ())), preferred_element_type=f32, **kw)


def _dot_tn(a, b, **kw):
    return lax.dot_general(a, b, (((0,), (0,)), ((), ())), preferred_element_type=f32, **kw)


def _rms(x, gain):
    return x * lax.rsqrt(jnp.mean(x * x, axis=-1, keepdims=True) + EPS) * gain


def _silu(x):
    return x * jax.nn.sigmoid(x)


def _norm_matmul_kernel(x_ref, g_ref, w_ref, gain_ref, o_ref, h_ref, *, n_norm):
    @pl.when(pl.program_id(1) == 0)
    def _():
        h_ref[...] = _rms(x_ref[...], g_ref[...]).astype(bf16)

    acc = _dot(h_ref[...], w_ref[...])
    if n_norm == 0:
        o_ref[...] = acc
    else:
        for gi in range(acc.shape[1] // LANES):
            sl = slice(gi * LANES, (gi + 1) * LANES)
            blk = acc[:, sl]
            if gi < n_norm:
                blk = _rms(blk, gain_ref[:, sl])
            o_ref[:, sl] = blk


def _norm_matmul(x, g, w, gain, *, tm, tn, n_norm):
    n, d = x.shape
    wn = w.shape[1]
    return pl.pallas_call(
        functools.partial(_norm_matmul_kernel, n_norm=n_norm),
        grid=(n // tm, wn // tn),
        in_specs=[
            pl.BlockSpec((tm, d), lambda i, j: (i, 0)),
            pl.BlockSpec((1, d), lambda i, j: (0, 0)),
            pl.BlockSpec((d, tn), lambda i, j: (0, j)),
            pl.BlockSpec((1, gain.shape[1]), lambda i, j: (0, 0)),
        ],
        out_specs=pl.BlockSpec((tm, tn), lambda i, j: (i, j)),
        out_shape=jax.ShapeDtypeStruct((n, wn), f32),
        scratch_shapes=[pltpu.VMEM((tm, d), bf16)],
        compiler_params=_params("parallel", "arbitrary"),
        name="norm_matmul",
    )(x, g, w, gain)


def _unit_lower_inverse(lmat):
    c = lmat.shape[0]
    row = lax.broadcasted_iota(i32, (c, c), 0)
    col = lax.broadcasted_iota(i32, (c, c), 1)
    p = jnp.where(row == col, 1.0, 0.0).astype(f32) - lmat
    m = _dot(lmat, lmat, precision=HIGHEST)
    power = 2
    while True:
        p = p + _dot(p, m, precision=HIGHEST)
        power *= 2
        if power >= c:
            return p
        m = _dot(m, m, precision=HIGHEST)


def _gdn_prompt_kernel(qkv_ref, z_ref, ba_ref, convw_ref, alog_ref, dtb_ref, onorm_ref,
                       o_ref, s_out_ref, xp_ref, s_ref):
    c = GDN_CHUNK
    step = pl.program_id(0)

    @pl.when(step == 0)
    def _():
        xp_ref[0:SUBLANES, :] = jnp.zeros((SUBLANES, GDN_CONV_DIM), f32)
        s_ref[...] = jnp.zeros_like(s_ref)

    xp_ref[SUBLANES:SUBLANES + c, :] = qkv_ref[...]
    w = convw_ref[...]
    y = None
    for j in range(GDN_CONV):
        lo = SUBLANES - (GDN_CONV - 1) + j
        term = xp_ref[lo:lo + c, :] * w[j:j + 1, :]
        y = term if y is None else y + term
    y = _silu(y)
    xp_ref[0:SUBLANES, :] = xp_ref[c:c + SUBLANES, :]

    ba = ba_ref[...]
    beta_all = jax.nn.sigmoid(ba)
    g_all = -jnp.exp(alog_ref[...]) * jax.nn.softplus(ba + dtb_ref[...])
    row = lax.broadcasted_iota(i32, (c, c), 0)
    col = lax.broadcasted_iota(i32, (c, c), 1)
    incl = row >= col
    strict = row > col
    gcum_all = _dot(jnp.where(incl, 1.0, 0.0).astype(f32), g_all, precision=HIGHEST)
    gcum_t = gcum_all.T
    onorm = onorm_ref[...]

    for h in range(GDN_HEADS):
        hs = slice(h * HEAD_DIM, (h + 1) * HEAD_DIM)
        q = y[:, h * HEAD_DIM:(h + 1) * HEAD_DIM]
        k = y[:, GDN_QK + h * HEAD_DIM:GDN_QK + (h + 1) * HEAD_DIM]
        v = y[:, 2 * GDN_QK + h * HEAD_DIM:2 * GDN_QK + (h + 1) * HEAD_DIM]
        q = q * lax.rsqrt(jnp.sum(q * q, axis=-1, keepdims=True) + EPS) * (HEAD_DIM ** -0.5)
        k = k * lax.rsqrt(jnp.sum(k * k, axis=-1, keepdims=True) + EPS)
        beta = beta_all[:, h:h + 1]
        gc = gcum_all[:, GDN_HEADS + h:GDN_HEADS + h + 1]
        gr = gcum_t[GDN_HEADS + h:GDN_HEADS + h + 1, :]
        decay = jnp.exp(jnp.where(incl, gc - gr, -jnp.inf))
        kbeta = k * beta
        lmat = _dot_nt(kbeta, k) * jnp.where(strict, decay, 0.0)
        tinv = _unit_lower_inverse(lmat)
        egc = jnp.exp(gc)
        u = _dot(tinv, v * beta)
        wmat = _dot(tinv, kbeta * egc)
        s = s_ref[h]
        v_new = u - _dot(wmat, s)
        intra = _dot_nt(q, k) * decay
        o = _dot(q * egc, s) + _dot(intra, v_new)
        g_last = gcum_all[c - 1:c, GDN_HEADS + h:GDN_HEADS + h + 1]
        s_ref[h] = s * jnp.exp(g_last) + _dot_tn(k * jnp.exp(g_last - gc), v_new)
        o_ref[:, hs] = (_rms(o, onorm) * _silu(z_ref[:, hs])).astype(o_ref.dtype)

    @pl.when(step == pl.num_programs(0) - 1)
    def _():
        s_out_ref[...] = s_ref[...]


def _gdn_prompt(proj_a, proj_b, conv_w, alog_row, dtb_row, onorm, *, n_tok):
    c = GDN_CHUNK
    zblk = GDN_CONV_DIM // GDN_V
    return pl.pallas_call(
        _gdn_prompt_kernel,
        grid=(n_tok // c,),
        in_specs=[
            pl.BlockSpec((c, GDN_CONV_DIM), lambda i: (i, 0)),
            pl.BlockSpec((c, GDN_V), lambda i: (i, zblk)),
            pl.BlockSpec((c, LANES), lambda i: (i, B_BA // LANES)),
            pl.BlockSpec((GDN_CONV, GDN_CONV_DIM), lambda i: (0, 0)),
            pl.BlockSpec((1, LANES), lambda i: (0, 0)),
            pl.BlockSpec((1, LANES), lambda i: (0, 0)),
            pl.BlockSpec((1, HEAD_DIM), lambda i: (0, 0)),
        ],
        out_specs=[
            pl.BlockSpec((c, GDN_V), lambda i: (i, 0)),
            pl.BlockSpec((GDN_HEADS, HEAD_DIM, HEAD_DIM), lambda i: (0, 0, 0)),
        ],
        out_shape=[
            jax.ShapeDtypeStruct((n_tok, GDN_V), bf16),
            jax.ShapeDtypeStruct((GDN_HEADS, HEAD_DIM, HEAD_DIM), f32),
        ],
        scratch_shapes=[
            pltpu.VMEM((c + SUBLANES, GDN_CONV_DIM), f32),
            pltpu.VMEM((GDN_HEADS, HEAD_DIM, HEAD_DIM), f32),
        ],
        compiler_params=_params("arbitrary"),
        name="gdn_prompt",
    )(proj_a, proj_a, proj_b, conv_w, alog_row, dtb_row, onorm)


GDN_S_SEQ = SUBLANES


def _gdn_sample_kernel(q_ref, k_ref, v_ref, z_ref, ba_ref, bq_ref, bk_ref, bv_ref, wq_ref, wk_ref, wv_ref,
                       alog_ref, dtb_ref, onorm_ref, s_ref, o_ref, so_ref, *, n_tok):
    h = pl.program_id(1)
    nb = GDN_S_SEQ
    lane = lax.broadcasted_iota(i32, (nb, LANES), 1)

    def conv(x_ref, buf_ref, w_ref, t):
        y = None
        for j in range(GDN_CONV):
            i = t + j
            row = buf_ref[i] if i < GDN_CONV - 1 else x_ref[i - (GDN_CONV - 1)]
            term = row * w_ref[j:j + 1, :]
            y = term if y is None else y + term
        return _silu(y)

    def lane_col(x, idx):
        return jnp.sum(jnp.where(lane == idx, x, 0.0), axis=1, keepdims=True)

    qs, ks, vs, betas, egs = [], [], [], [], []
    for t in range(n_tok):
        q = conv(q_ref, bq_ref, wq_ref, t)
        k = conv(k_ref, bk_ref, wk_ref, t)
        q = q * lax.rsqrt(jnp.sum(q * q, axis=-1, keepdims=True) + EPS) * (HEAD_DIM ** -0.5)
        k = k * lax.rsqrt(jnp.sum(k * k, axis=-1, keepdims=True) + EPS)
        qs.append(q.T)
        ks.append(k.T)
        vs.append(conv(v_ref, bv_ref, wv_ref, t))
        ba = ba_ref[t]
        betas.append(lane_col(jax.nn.sigmoid(ba), h))
        g = -jnp.exp(alog_ref[...]) * jax.nn.softplus(ba + dtb_ref[...])
        egs.append(jnp.exp(lane_col(g, GDN_HEADS + h)))

    o_rows = [[None] * nb for _ in range(n_tok)]
    for b in range(nb):
        s = s_ref[b]
        for t in range(n_tok):
            kc = ks[t][:, b:b + 1]
            qc = qs[t][:, b:b + 1]
            s = s * egs[t][b:b + 1, :]
            ks_row = jnp.sum(kc * s, axis=0, keepdims=True)
            delta = (vs[t][b:b + 1, :] - ks_row) * betas[t][b:b + 1, :]
            s = s + kc * delta
            o_rows[t][b] = jnp.sum(qc * s, axis=0, keepdims=True)
        so_ref[b] = s
    for t in range(n_tok):
        o = jnp.concatenate(o_rows[t], axis=0)
        o_ref[t] = _rms(o, onorm_ref[...]) * _silu(z_ref[t])


def _gdn_sample(xa, ba, conv_buf, conv_w, alog_row, dtb_row, onorm, state):
    n_tok, n_seq, _ = xa.shape
    nb = GDN_S_SEQ
    hq, hk, hv, hz = 0, GDN_HEADS, 2 * GDN_HEADS, 3 * GDN_HEADS

    def xspec(off, rows):
        return pl.BlockSpec((rows, nb, HEAD_DIM), lambda i, h: (0, i, off + h))

    def wspec(off):
        return pl.BlockSpec((GDN_CONV, HEAD_DIM), lambda i, h: (0, off + h))

    row = pl.BlockSpec((1, LANES), lambda i, h: (0, 0))
    sspec = pl.BlockSpec((nb, None, HEAD_DIM, HEAD_DIM), lambda i, h: (i, h, 0, 0))
    return pl.pallas_call(
        functools.partial(_gdn_sample_kernel, n_tok=n_tok),
        grid=(n_seq // nb, GDN_HEADS),
        in_specs=[xspec(hq, n_tok), xspec(hk, n_tok), xspec(hv, n_tok), xspec(hz, n_tok),
                  pl.BlockSpec((n_tok, nb, LANES), lambda i, h: (0, i, 0)),
                  xspec(hq, GDN_CONV - 1), xspec(hk, GDN_CONV - 1), xspec(hv, GDN_CONV - 1),
                  wspec(hq), wspec(hk), wspec(hv), row, row, row, sspec],
        out_specs=[pl.BlockSpec((n_tok, nb, HEAD_DIM), lambda i, h: (0, i, h)), sspec],
        out_shape=[jax.ShapeDtypeStruct((n_tok, n_seq, GDN_V), f32),
                   jax.ShapeDtypeStruct(state.shape, f32)],
        compiler_params=_params("parallel", "arbitrary"),
        name="gdn_sample",
    )(xa, xa, xa, xa, ba, conv_buf, conv_buf, conv_buf, conv_w, conv_w, conv_w, alog_row, dtb_row, onorm, state)


def _bucket_thresholds():
    exact = N_BUCKETS // 2
    d = np.arange(0, 2 * MAX_DISTANCE)
    val = np.log(np.maximum(d, 1).astype(np.float64) / exact) / math.log(MAX_DISTANCE / exact) * (N_BUCKETS - exact)
    frac = np.abs(val - np.round(val))[exact + 1:MAX_DISTANCE]
    assert frac.min() > 1e-3, "a bucket boundary sits on an integer distance"
    bucket = np.where(d < exact, d, np.minimum(exact + val.astype(np.int64), N_BUCKETS - 1))
    assert np.all(np.diff(bucket) >= 0)
    return [int(np.argmax(bucket >= b)) for b in range(N_BUCKETS)]


_BUCKET_THR = _bucket_thresholds()


def _bias_from_dist(dist, rb_ref, h):
    v = jnp.full(dist.shape, rb_ref[0, h], f32)
    for b in range(1, N_BUCKETS):
        v = jnp.where(dist >= _BUCKET_THR[b], rb_ref[b, h], v)
    return v


def _topk_mask(s, blk, k):
    nblk = s.shape[1]
    sel = jnp.zeros(s.shape, f32)
    for _ in range(k):
        m = jnp.max(s, axis=1, keepdims=True)
        cand = jnp.where((s == m) & (m > -jnp.inf), blk, nblk)
        pick = blk == jnp.min(cand, axis=1, keepdims=True)
        sel = jnp.where(pick, 1.0, sel)
        s = jnp.where(pick, -jnp.inf, s)
    return sel


def _block_mean_kernel(k_ref, o_ref):
    o_ref[0] = jnp.mean(k_ref[...], axis=0, keepdims=True)


def _block_mean(proj_b, n_blk):
    return pl.pallas_call(
        _block_mean_kernel,
        grid=(n_blk,),
        in_specs=[pl.BlockSpec((MOBA_BLOCK, MOBA_W), lambda i: (i, B_MK // MOBA_W))],
        out_specs=pl.BlockSpec((1, 1, MOBA_W), lambda i: (i, 0, 0)),
        out_shape=jax.ShapeDtypeStruct((n_blk, 1, MOBA_W), f32),
        compiler_params=_params("parallel"),
        name="moba_block_mean",
    )(proj_b)


def _moba_select_kernel(q_ref, km_ref, o_ref):
    own = pl.program_id(0)
    nblk = km_ref.shape[0]
    blk = lax.broadcasted_iota(i32, (MOBA_BLOCK, nblk), 1)
    outs = []
    for h in range(MOBA_HEADS):
        hs = slice(h * HEAD_DIM, (h + 1) * HEAD_DIM)
        s = _dot_nt(q_ref[:, hs], km_ref[:, hs], precision=HIGHEST)
        s = jnp.where(blk < own, s, -jnp.inf)
        outs.append(_topk_mask(s, blk, MOBA_TOPK))
    pad = LANES - MOBA_HEADS * nblk
    if pad:
        outs.append(jnp.zeros((MOBA_BLOCK, pad), f32))
    o_ref[...] = jnp.concatenate(outs, axis=1)


def _moba_select(proj_b, kmean, n_blk):
    assert n_blk * MOBA_HEADS <= LANES
    return pl.pallas_call(
        _moba_select_kernel,
        grid=(n_blk,),
        in_specs=[pl.BlockSpec((MOBA_BLOCK, MOBA_W), lambda i: (i, B_MQ // MOBA_W)),
                  pl.BlockSpec((n_blk, MOBA_W), lambda i: (0, 0))],
        out_specs=pl.BlockSpec((MOBA_BLOCK, LANES), lambda i: (i, 0)),
        out_shape=jax.ShapeDtypeStruct((n_blk * MOBA_BLOCK, LANES), f32),
        compiler_params=_params("parallel"),
        name="moba_select",
    )(proj_b, kmean)


def _moba_prompt_kernel(qi_ref, kj_ref, rb_ref, q_ref, k_ref, v_ref, sel_ref, o_ref,
                        bias_ref, m_ref, l_ref, acc_ref, *, n_blk):
    step = pl.program_id(0)
    qi = qi_ref[step]
    kj = kj_ref[step]
    nq = MOBA_BLOCK

    @pl.when(step == 0)
    def _():
        r = lax.broadcasted_iota(i32, (nq, nq), 0)
        c = lax.broadcasted_iota(i32, (nq, nq), 1)
        for h in range(MOBA_HEADS):
            bias_ref[h, 0] = jnp.where(c <= r, _bias_from_dist(r - c, rb_ref, h), NEG_INF)
            bias_ref[h, 1] = _bias_from_dist(r - c + nq, rb_ref, h)
            bias_ref[h, 2] = jnp.full((nq, nq), rb_ref[N_BUCKETS - 1, h], f32)

    first = kj == qi

    @pl.when(first)
    def _():
        m_ref[...] = jnp.full(m_ref.shape, NEG_INF, f32)
        l_ref[...] = jnp.zeros_like(l_ref)
        acc_ref[...] = jnp.zeros_like(acc_ref)

    slot = jnp.where(first, 0, jnp.where(kj == qi - 1, 1, 2))
    sel = sel_ref[...]
    lane = lax.broadcasted_iota(i32, sel.shape, 1)
    for h in range(MOBA_HEADS):
        hs = slice(h * HEAD_DIM, (h + 1) * HEAD_DIM)
        s = _dot_nt(q_ref[:, hs].astype(bf16), k_ref[:, hs].astype(bf16)) * (HEAD_DIM ** -0.5) + bias_ref[h, slot]
        picked = jnp.max(jnp.where(lane == h * n_blk + kj, sel, 0.0), axis=1, keepdims=True)
        s = jnp.where((picked > 0.0) | first, s, NEG_INF)
        m_prev = m_ref[h]
        m_new = jnp.maximum(m_prev, jnp.max(s, axis=1, keepdims=True))
        alpha = jnp.exp(m_prev - m_new)
        p = jnp.exp(s - m_new)
        l_ref[h] = alpha * l_ref[h] + jnp.sum(p, axis=1, keepdims=True)
        acc_ref[h] = alpha * acc_ref[h] + _dot(p.astype(bf16), v_ref[:, hs].astype(bf16))
        m_ref[h] = m_new

    @pl.when(kj == 0)
    def _():
        for h in range(MOBA_HEADS):
            hs = slice(h * HEAD_DIM, (h + 1) * HEAD_DIM)
            o_ref[:, hs] = (acc_ref[h] / l_ref[h]).astype(o_ref.dtype)


def _moba_prompt(proj_b, sel, rel_bias, n_blk):
    qi = np.concatenate([np.full(i + 1, i) for i in range(n_blk)]).astype(np.int32)
    kj = np.concatenate([np.arange(i, -1, -1) for i in range(n_blk)]).astype(np.int32)
    nq = MOBA_BLOCK
    grid_spec = pltpu.PrefetchScalarGridSpec(
        num_scalar_prefetch=2,
        grid=(len(qi),),
        in_specs=[
            pl.BlockSpec(memory_space=pltpu.SMEM),
            pl.BlockSpec((nq, MOBA_W), lambda s, qi, kj: (qi[s], B_MQ // MOBA_W)),
            pl.BlockSpec((nq, MOBA_W), lambda s, qi, kj: (kj[s], B_MK // MOBA_W)),
            pl.BlockSpec((nq, MOBA_W), lambda s, qi, kj: (kj[s], B_MV // MOBA_W)),
            pl.BlockSpec((nq, LANES), lambda s, qi, kj: (qi[s], 0)),
        ],
        out_specs=pl.BlockSpec((nq, MOBA_W), lambda s, qi, kj: (qi[s], 0)),
        scratch_shapes=[
            pltpu.VMEM((MOBA_HEADS, 3, nq, nq), f32),
            pltpu.VMEM((MOBA_HEADS, nq, 1), f32),
            pltpu.VMEM((MOBA_HEADS, nq, 1), f32),
            pltpu.VMEM((MOBA_HEADS, nq, HEAD_DIM), f32),
        ],
    )
    return pl.pallas_call(
        functools.partial(_moba_prompt_kernel, n_blk=n_blk),
        grid_spec=grid_spec,
        out_shape=jax.ShapeDtypeStruct((n_blk * nq, MOBA_W), bf16),
        compiler_params=_params("arbitrary"),
        name="moba_prompt",
    )(jnp.asarray(qi), jnp.asarray(kj), rel_bias, proj_b, proj_b, proj_b, sel)


def _moba_sample_kernel(pt_ref, rb_ref, q_ref, kn_ref, vn_ref, *rest, n_pages, page):
    kp = rest[:n_pages]
    vp = rest[n_pages:2 * n_pages]
    o_ref, kbuf, vbuf, bias_ref = rest[2 * n_pages:]
    n_tok = q_ref.shape[0]
    past = n_pages * page
    n_blk = past // MOBA_BLOCK
    ppb = MOBA_BLOCK // page

    @pl.when(pl.program_id(0) == 0)
    def _():
        t = lax.broadcasted_iota(i32, (n_tok, past), 0)
        p = lax.broadcasted_iota(i32, (n_tok, past), 1)
        for h in range(MOBA_HEADS):
            bias_ref[h] = _bias_from_dist(past + t - p, rb_ref, h)

    sums = []
    for p in range(n_pages):
        kpage = kp[p][...]
        kbuf[p * page:(p + 1) * page, :] = kpage.astype(bf16)
        vbuf[p * page:(p + 1) * page, :] = vp[p][...].astype(bf16)
        sums.append(jnp.sum(kpage, axis=0, keepdims=True))
    kmean = jnp.concatenate(
        [sum(sums[n * ppb + 1:(n + 1) * ppb], sums[n * ppb]) / MOBA_BLOCK for n in range(n_blk)], axis=0)

    q = q_ref[...]
    blk = lax.broadcasted_iota(i32, (n_tok, n_blk), 1)
    tq = lax.broadcasted_iota(i32, (n_tok, n_tok), 0)
    tk = lax.broadcasted_iota(i32, (n_tok, n_tok), 1)
    scale = HEAD_DIM ** -0.5
    for h in range(MOBA_HEADS):
        hs = slice(h * HEAD_DIM, (h + 1) * HEAD_DIM)
        qh = q[:, hs]
        sel = _topk_mask(_dot_nt(qh, kmean[:, hs], precision=HIGHEST), blk, MOBA_TOPK)
        sel_full = jnp.concatenate(
            [jnp.broadcast_to(sel[:, n:n + 1], (n_tok, MOBA_BLOCK)) for n in range(n_blk)], axis=1)
        qb = qh.astype(bf16)
        lp = _dot_nt(qb, kbuf[:, hs]) * scale + bias_ref[h]
        lp = jnp.where(sel_full > 0.0, lp, NEG_INF)
        ln = _dot_nt(qb, kn_ref[:, hs].astype(bf16)) * scale + _bias_from_dist(tq - tk, rb_ref, h)
        ln = jnp.where(tk <= tq, ln, NEG_INF)
        m = jnp.maximum(jnp.max(lp, axis=1, keepdims=True), jnp.max(ln, axis=1, keepdims=True))
        pp = jnp.exp(lp - m)
        pn = jnp.exp(ln - m)
        den = jnp.sum(pp, axis=1, keepdims=True) + jnp.sum(pn, axis=1, keepdims=True)
        num = _dot(pp.astype(bf16), vbuf[:, hs]) + _dot(pn.astype(bf16), vn_ref[:, hs].astype(bf16))
        o_ref[:, hs] = num / den


def _moba_sample(proj_b3, pool_k, pool_v, page_table, rel_bias):
    n_seq, n_tok, _ = proj_b3.shape
    n_pages = page_table.shape[1]
    page = pool_k.shape[1]

    def new_spec(col):
        return pl.BlockSpec((None, n_tok, MOBA_W), lambda b, pt: (b, 0, col // MOBA_W))

    def page_spec(p):
        return pl.BlockSpec((None, page, MOBA_W), lambda b, pt: (pt[b, p], 0, 0))

    grid_spec = pltpu.PrefetchScalarGridSpec(
        num_scalar_prefetch=1,
        grid=(n_seq,),
        in_specs=[pl.BlockSpec(memory_space=pltpu.SMEM), new_spec(B_MQ), new_spec(B_MK), new_spec(B_MV)]
        + [page_spec(p) for p in range(n_pages)] * 2,
        out_specs=pl.BlockSpec((None, n_tok, MOBA_W), lambda b, pt: (b, 0, 0)),
        scratch_shapes=[
            pltpu.VMEM((n_pages * page, MOBA_W), bf16),
            pltpu.VMEM((n_pages * page, MOBA_W), bf16),
            pltpu.VMEM((MOBA_HEADS, n_tok, n_pages * page), f32),
        ],
    )
    return pl.pallas_call(
        functools.partial(_moba_sample_kernel, n_pages=n_pages, page=page),
        grid_spec=grid_spec,
        out_shape=jax.ShapeDtypeStruct((n_seq, n_tok, MOBA_W), f32),
        compiler_params=_params("arbitrary"),
        name="moba_sample",
    )(page_table, rel_bias, proj_b3, proj_b3, proj_b3, *([pool_k] * n_pages), *([pool_v] * n_pages))


def _mem_attn_kernel(q_ref, k_ref, v_ref, o_ref):
    for h in range(MEM_HEADS):
        hs = slice(h * HEAD_DIM, (h + 1) * HEAD_DIM)
        s = _dot_nt(q_ref[:, hs].astype(bf16), k_ref[:, hs].astype(bf16)) * (HEAD_DIM ** -0.5)
        p = jnp.exp(s - jnp.max(s, axis=1, keepdims=True))
        num = _dot(p.astype(bf16), v_ref[:, hs].astype(bf16))
        o_ref[:, hs] = (num / jnp.sum(p, axis=1, keepdims=True)).astype(o_ref.dtype)


def _mem_attn_prompt(proj_b, mem_kv, *, n_tok, tq):
    n_mem = mem_kv.shape[0]
    return pl.pallas_call(
        _mem_attn_kernel,
        grid=(n_tok // tq,),
        in_specs=[pl.BlockSpec((tq, MEM_W), lambda i: (i, B_CQ // MEM_W)),
                  pl.BlockSpec((n_mem, MEM_W), lambda i: (0, 0)),
                  pl.BlockSpec((n_mem, MEM_W), lambda i: (0, 1))],
        out_specs=pl.BlockSpec((tq, MEM_W), lambda i: (i, 0)),
        out_shape=jax.ShapeDtypeStruct((n_tok, MEM_W), bf16),
        compiler_params=_params("parallel"),
        name="mem_attn_prompt",
    )(proj_b, mem_kv, mem_kv)


def _mem_attn_sample(proj_b3, mem_k, mem_v):
    n_seq, n_tok, _ = proj_b3.shape
    n_mem = mem_k.shape[1]
    return pl.pallas_call(
        _mem_attn_kernel,
        grid=(n_seq,),
        in_specs=[pl.BlockSpec((None, n_tok, MEM_W), lambda b: (b, 0, B_CQ // MEM_W)),
                  pl.BlockSpec((None, n_mem, MEM_W), lambda b: (b, 0, 0)),
                  pl.BlockSpec((None, n_mem, MEM_W), lambda b: (b, 0, 0))],
        out_specs=pl.BlockSpec((None, n_tok, MEM_W), lambda b: (b, 0, 0)),
        out_shape=jax.ShapeDtypeStruct((n_seq, n_tok, MEM_W), f32),
        compiler_params=_params("parallel"),
        name="mem_attn_sample",
    )(proj_b3, mem_k, mem_v)


def _split_bf16(x):
    hi = x.astype(bf16)
    return hi, (x - hi.astype(f32)).astype(bf16)


def _outproj_router_kernel(x_ref, og_ref, om_ref, oc_ref, w_ref, g_ref, rw_ref, rb_ref,
                           y_ref, h_ref, ti_ref, tw_ref):
    mix = (_dot(og_ref[...], w_ref[0:GDN_V, :])
           + _dot(om_ref[...], w_ref[GDN_V:GDN_V + MOBA_W, :])
           + _dot(oc_ref[...], w_ref[GDN_V + MOBA_W:, :]))
    y = x_ref[...] + mix
    y_ref[...] = y
    h = _rms(y, g_ref[...])
    h_ref[...] = h
    h_hi, h_lo = _split_bf16(h)
    w_hi, w_lo = _split_bf16(rw_ref[...])
    logits = _dot(h_hi, w_hi) + (_dot(h_hi, w_lo) + _dot(h_lo, w_hi)) + rb_ref[...]
    lane = lax.broadcasted_iota(i32, logits.shape, 1)
    s = jnp.where(lane < N_EXPERTS, logits, -jnp.inf)
    vals, idxs = [], []
    for _ in range(TOP_K):
        m = jnp.max(s, axis=1, keepdims=True)
        idx = jnp.min(jnp.where(s == m, lane, LANES), axis=1, keepdims=True)
        vals.append(m)
        idxs.append(idx)
        s = jnp.where(lane == idx, -jnp.inf, s)
    exps = [jnp.exp(v - vals[0]) for v in vals]
    den = exps[0]
    for e in exps[1:]:
        den = den + e
    ti = jnp.zeros(logits.shape, i32)
    tw = jnp.zeros(logits.shape, f32)
    for r in range(TOP_K):
        ti = jnp.where(lane == r, idxs[r], ti)
        tw = jnp.where(lane == r, exps[r] / den, tw)
    ti_ref[...] = ti
    tw_ref[...] = tw


def _outproj_router(x, og, om, oc, w_out, g, rw, rb, *, tm):
    n, d = x.shape
    row = lambda w: pl.BlockSpec((tm, w), lambda i: (i, 0))
    full = lambda a: pl.BlockSpec(a.shape, lambda i: (0,) * a.ndim)
    return pl.pallas_call(
        _outproj_router_kernel,
        grid=(n // tm,),
        in_specs=[row(d), row(GDN_V), row(MOBA_W), row(MEM_W), full(w_out), full(g), full(rw), full(rb)],
        out_specs=[row(d), row(d), row(LANES), row(LANES)],
        out_shape=[jax.ShapeDtypeStruct((n, d), f32), jax.ShapeDtypeStruct((n, d), f32),
                   jax.ShapeDtypeStruct((n, LANES), i32), jax.ShapeDtypeStruct((n, LANES), f32)],
        compiler_params=_params("parallel"),
        name="outproj_router",
    )(x, og, om, oc, w_out, g, rw, rb)


def _row_gather_kernel(idx_ref, src_ref, dst_ref, sem, *, rows):
    base = pl.program_id(0) * rows

    def row_copy(r, src_row):
        return pltpu.make_async_copy(src_ref.at[pl.ds(src_row, 1)], dst_ref.at[pl.ds(base + r, 1)], sem)

    def issue(r, carry):
        row_copy(r, idx_ref[base + r]).start()
        return carry

    def drain(r, carry):
        row_copy(r, 0).wait()
        return carry

    lax.fori_loop(0, rows, issue, 0)
    lax.fori_loop(0, rows, drain, 0)


def _row_gather(src, idx, *, rows):
    n_out = idx.shape[0]
    grid_spec = pltpu.PrefetchScalarGridSpec(
        num_scalar_prefetch=1,
        grid=(n_out // rows,),
        in_specs=[pl.BlockSpec(memory_space=pl.ANY)],
        out_specs=pl.BlockSpec(memory_space=pl.ANY),
        scratch_shapes=[pltpu.SemaphoreType.DMA(())],
    )
    return pl.pallas_call(
        functools.partial(_row_gather_kernel, rows=rows),
        grid_spec=grid_spec,
        out_shape=jax.ShapeDtypeStruct((n_out, src.shape[1]), src.dtype),
        compiler_params=_params("arbitrary"),
        name="row_gather",
    )(idx, src)


def _visit_store(o_ref, new, lo, hi):
    @pl.when(lo == 0)
    def _():
        o_ref[...] = new

    @pl.when(lo > 0)
    def _():
        r = lax.broadcasted_iota(i32, new.shape, 0)
        o_ref[...] = jnp.where((r >= lo) & (r < hi), new, o_ref[...])


def _moe_gate_up_kernel(tv_ref, ev_ref, lo_ref, hi_ref, x_ref, wg_ref, wu_ref, bg_ref, bu_ref, o_ref):
    v = pl.program_id(1)
    lo = lo_ref[v]
    hi = hi_ref[v]

    @pl.when(hi > lo)
    def _():
        x = x_ref[...].astype(bf16)
        gate = _dot(x, wg_ref[...].astype(bf16)) + bg_ref[...]
        up = _dot(x, wu_ref[...].astype(bf16)) + bu_ref[...]
        gate = jnp.minimum(gate, SWIGLU_LIMIT)
        up = jnp.clip(up, -SWIGLU_LIMIT, SWIGLU_LIMIT)
        act = gate * jax.nn.sigmoid(SWIGLU_ALPHA * gate) * (up + 1.0)
        _visit_store(o_ref, act.astype(o_ref.dtype), lo, hi)


def _moe_gate_up(xs, w_gu, b_gu3, tv, ev, lo, hi, *, tm, tf):
    rows, d = xs.shape
    nf = D_FF // tf
    grid_spec = pltpu.PrefetchScalarGridSpec(
        num_scalar_prefetch=4,
        grid=(nf, tv.shape[0]),
        in_specs=[
            pl.BlockSpec((tm, d), lambda f, v, tv, ev, lo, hi: (tv[v], 0)),
            pl.BlockSpec((None, d, tf), lambda f, v, tv, ev, lo, hi: (ev[v], 0, f)),
            pl.BlockSpec((None, d, tf), lambda f, v, tv, ev, lo, hi: (ev[v], 0, nf + f)),
            pl.BlockSpec((None, 1, tf), lambda f, v, tv, ev, lo, hi: (ev[v], 0, f)),
            pl.BlockSpec((None, 1, tf), lambda f, v, tv, ev, lo, hi: (ev[v], 0, nf + f)),
        ],
        out_specs=pl.BlockSpec((tm, tf), lambda f, v, tv, ev, lo, hi: (tv[v], f)),
    )
    return pl.pallas_call(
        _moe_gate_up_kernel,
        grid_spec=grid_spec,
        out_shape=jax.ShapeDtypeStruct((rows, D_FF), bf16),
        compiler_params=_params("arbitrary", "arbitrary"),
        name="moe_gate_up",
    )(tv, ev, lo, hi, xs, w_gu, w_gu, b_gu3, b_gu3)


def _moe_down_kernel(tv_ref, ev_ref, lo_ref, hi_ref, a_ref, w_ref, b_ref, o_ref):
    v = pl.program_id(1)
    lo = lo_ref[v]
    hi = hi_ref[v]

    @pl.when(hi > lo)
    def _():
        y = _dot(a_ref[...], w_ref[...].astype(bf16)) + b_ref[...]
        _visit_store(o_ref, y, lo, hi)


def _moe_down(act, w_dn, b_dn3, tv, ev, lo, hi, *, tm, tn):
    rows, dff = act.shape
    d = w_dn.shape[2]
    grid_spec = pltpu.PrefetchScalarGridSpec(
        num_scalar_prefetch=4,
        grid=(d // tn, tv.shape[0]),
        in_specs=[
            pl.BlockSpec((tm, dff), lambda n, v, tv, ev, lo, hi: (tv[v], 0)),
            pl.BlockSpec((None, dff, tn), lambda n, v, tv, ev, lo, hi: (ev[v], 0, n)),
            pl.BlockSpec((None, 1, tn), lambda n, v, tv, ev, lo, hi: (ev[v], 0, n)),
        ],
        out_specs=pl.BlockSpec((tm, tn), lambda n, v, tv, ev, lo, hi: (tv[v], n)),
    )
    return pl.pallas_call(
        _moe_down_kernel,
        grid_spec=grid_spec,
        out_shape=jax.ShapeDtypeStruct((rows, d), f32),
        compiler_params=_params("arbitrary", "arbitrary"),
        name="moe_down",
    )(tv, ev, lo, hi, act, w_dn, b_dn3)


def _moe_combine_kernel(y_ref, yg_ref, tw_ref, o_ref):
    tw = tw_ref[...]
    moe = None
    for k in range(TOP_K):
        term = yg_ref[k] * tw[:, k:k + 1]
        moe = term if moe is None else moe + term
    o_ref[...] = y_ref[...] + moe


def _moe_combine(y1, yg, tw, *, tm):
    n, d = y1.shape
    return pl.pallas_call(
        _moe_combine_kernel,
        grid=(n // tm,),
        in_specs=[pl.BlockSpec((tm, d), lambda i: (i, 0)),
                  pl.BlockSpec((TOP_K, tm, d), lambda i: (0, i, 0)),
                  pl.BlockSpec((tm, LANES), lambda i: (i, 0))],
        out_specs=pl.BlockSpec((tm, d), lambda i: (i, 0)),
        out_shape=jax.ShapeDtypeStruct((n, d), f32),
        compiler_params=_params("parallel"),
        name="moe_combine",
    )(y1, yg, tw)


def _route(top_i, *, tm):
    e = top_i.reshape(-1)
    a = e.shape[0]
    order = jnp.argsort(e, stable=True).astype(i32)
    tok_sorted = order // TOP_K
    pos = jnp.zeros((a,), i32).at[order].set(jnp.arange(a, dtype=i32))
    counts = jnp.zeros((N_EXPERTS,), i32).at[e].add(1)
    ends = jnp.cumsum(counts)
    starts = ends - counts
    first_tile = starts // tm
    last_tile = jnp.maximum(ends - 1, 0) // tm
    n_vis = jnp.where(counts > 0, last_tile - first_tile + 1, 0)
    vis_end = jnp.cumsum(n_vis)
    vis_start = vis_end - n_vis
    total = vis_end[-1]
    v = jnp.arange(a // tm + N_EXPERTS, dtype=i32)
    vc = jnp.minimum(v, total - 1)
    ev = jnp.searchsorted(vis_end, vc, side="right").astype(i32)
    tv = first_tile[ev] + (vc - vis_start[ev])
    lo = jnp.maximum(starts[ev], tv * tm) - tv * tm
    hi = jnp.minimum(ends[ev], (tv + 1) * tm) - tv * tm
    valid = v < total
    return tok_sorted, pos, tv, ev, jnp.where(valid, lo, 0), jnp.where(valid, hi, 0)


MOE_TM = 256
MOE_TF = 512
MOE_TN = 512
GATHER_ROWS = 512


def _moe(y1, h, top_i, top_w, w_gu, b_gu, w_dn, b_dn):
    n_tok = y1.shape[0]
    tok_sorted, pos, tv, ev, lo, hi = _route(top_i[:, :TOP_K], tm=MOE_TM)
    xs = _row_gather(h, tok_sorted, rows=GATHER_ROWS)
    act = _moe_gate_up(xs, w_gu, b_gu[:, None, :], tv, ev, lo, hi, tm=MOE_TM, tf=MOE_TF)
    ys = _moe_down(act, w_dn, b_dn[:, None, :], tv, ev, lo, hi, tm=MOE_TM, tn=MOE_TN)
    pos_kmajor = pos.reshape(n_tok, TOP_K).T.reshape(-1)
    yg = _row_gather(ys, pos_kmajor, rows=GATHER_ROWS).reshape(TOP_K, n_tok, -1)
    return _moe_combine(y1, yg, top_w, tm=GATHER_ROWS)


def kernel(x_prompt, x_sample, cache_moba_k, cache_moba_v, state_gdn, state_gdn_conv, cache_mem_k, cache_mem_v, page_table, mem_prompt, rel_bias, norm_mix, w_in, conv_w, a_log, dt_bias, gdn_o_norm, moba_q_norm, moba_k_norm, mem_q_norm, mem_norm, w_mem_kv, mem_k_norm, w_out, norm_ffn, router_w, router_b, w_gu, b_gu, w_dn, b_dn):
    assert x_prompt.shape[0] == 1 and all(a.shape[0] == 1 for a in (w_in, w_out, w_gu, w_dn, state_gdn))
    n_p = x_prompt.shape[1]
    n_seq, n_st = x_sample.shape[:2]
    n_s = n_seq * n_st
    d = x_prompt.shape[2]
    n_blk = n_p // MOBA_BLOCK
    x_all = jnp.concatenate([x_prompt.reshape(n_p, d), x_sample.reshape(n_s, d)], axis=0)

    w = w_in[0]
    off_beta = A_W
    off_moba = off_beta + 2 * GDN_HEADS
    off_mem = off_moba + 3 * MOBA_W
    w_a = w[:, :A_W].astype(bf16)
    w_b = jnp.concatenate(
        [w[:, off_moba:off_moba + MOBA_W], w[:, off_moba + MOBA_W:off_moba + 2 * MOBA_W], w[:, off_mem:],
         w[:, off_moba + 2 * MOBA_W:off_mem], w[:, off_beta:off_moba],
         jnp.zeros((d, LANES - 2 * GDN_HEADS), w.dtype)], axis=1).astype(bf16)
    gain_b = jnp.concatenate([jnp.tile(moba_q_norm[0], MOBA_HEADS), jnp.tile(moba_k_norm[0], MOBA_HEADS),
                              jnp.tile(mem_q_norm[0], MEM_HEADS)])[None]
    proj_a = _norm_matmul(x_all, norm_mix, w_a, gain_b, tm=512, tn=1024, n_norm=0)
    proj_b = _norm_matmul(x_all, norm_mix, w_b, gain_b, tm=256, tn=B_W, n_norm=B_NORM_GROUPS)
    proj_b3 = proj_b[n_p:].reshape(n_seq, n_st, B_W)

    pad_row = lambda a: jnp.zeros((1, LANES), f32).at[0, GDN_HEADS:2 * GDN_HEADS].set(a[0])
    alog_row, dtb_row = pad_row(a_log), pad_row(dt_bias)
    o_gdn_p, p_gdn = _gdn_prompt(proj_a, proj_b, conv_w[0], alog_row, dtb_row, gdn_o_norm, n_tok=n_p)
    p_conv = proj_a[n_p - (GDN_CONV - 1):n_p, :GDN_CONV_DIM]
    xa_s = proj_a[n_p:].reshape(n_seq, n_st, A_W)
    o_gdn_s, s_gdn = _gdn_sample(jnp.swapaxes(xa_s, 0, 1), jnp.swapaxes(proj_b3[..., B_BA:], 0, 1),
                                 jnp.swapaxes(state_gdn_conv[0], 0, 1), conv_w[0], alog_row, dtb_row,
                                 gdn_o_norm, state_gdn[0])
    o_gdn_s = jnp.swapaxes(o_gdn_s, 0, 1).reshape(n_s, GDN_V).astype(bf16)
    s_conv = jnp.concatenate([state_gdn_conv[0], xa_s[..., :GDN_CONV_DIM]], axis=1)[:, n_st:]

    kmean = _block_mean(proj_b, n_blk).reshape(n_blk, MOBA_W)
    sel = _moba_select(proj_b, kmean, n_blk)
    o_moba_p = _moba_prompt(proj_b, sel, rel_bias, n_blk)
    n_phys, page = cache_moba_k.shape[1:3]
    o_moba_s = _moba_sample(proj_b3, cache_moba_k[0].reshape(n_phys, page, MOBA_W),
                            cache_moba_v[0].reshape(n_phys, page, MOBA_W), page_table, rel_bias)
    o_moba_s = o_moba_s.reshape(n_s, MOBA_W).astype(bf16)

    n_mem = mem_prompt.shape[1]
    mem_kv = _norm_matmul(mem_prompt[0], mem_norm, w_mem_kv[0].astype(bf16),
                          jnp.tile(mem_k_norm[0], MEM_HEADS)[None], tm=n_mem, tn=2 * MEM_W, n_norm=MEM_HEADS)
    o_mem_p = _mem_attn_prompt(proj_b, mem_kv, n_tok=n_p, tq=512)
    o_mem_s = _mem_attn_sample(proj_b3, cache_mem_k[0].reshape(n_seq, n_mem, MEM_W),
                               cache_mem_v[0].reshape(n_seq, n_mem, MEM_W))
    o_mem_s = o_mem_s.reshape(n_s, MEM_W).astype(bf16)

    rw = jnp.pad(router_w[0], ((0, 0), (0, LANES - N_EXPERTS)))
    rb = jnp.pad(router_b, ((0, 0), (0, LANES - N_EXPERTS)))
    y1, h2, top_i, top_w = _outproj_router(
        x_all, jnp.concatenate([o_gdn_p, o_gdn_s]), jnp.concatenate([o_moba_p, o_moba_s]),
        jnp.concatenate([o_mem_p, o_mem_s]), w_out[0].astype(bf16), norm_ffn, rw, rb, tm=256)
    y = _moe(y1, h2, top_i, top_w, w_gu[0], b_gu[0], w_dn[0], b_dn[0])

    heads = lambda a, lead: a.reshape(lead + (MOBA_HEADS, HEAD_DIM))
    return (y[:n_p].reshape(1, n_p, d), y[n_p:].reshape(n_seq, n_st, d),
            heads(proj_b[:n_p, B_MK:B_MK + MOBA_W], (1, 1, n_p)), heads(proj_b[:n_p, B_MV:B_MV + MOBA_W], (1, 1, n_p)),
            p_gdn[None, None], p_conv[None, None],
            heads(mem_kv[:, :MEM_W], (1, 1, n_mem)), heads(mem_kv[:, MEM_W:], (1, 1, n_mem)),
            heads(proj_b3[..., B_MK:B_MK + MOBA_W], (1, n_seq, n_st)),
            heads(proj_b3[..., B_MV:B_MV + MOBA_W], (1, n_seq, n_st)),
            s_gdn[None], s_conv[None])
```

```python
import functools
import math

import numpy as np
import jax
import jax.numpy as jnp
from jax import lax
from jax.experimental import pallas as pl
from jax.experimental.pallas import tpu as pltpu

f32 = jnp.float32
bf16 = jnp.bfloat16
i32 = jnp.int32
HIGHEST = lax.Precision.HIGHEST

LANES = 128
SUBLANES = 8
VMEM_LIMIT = 56 * 1024 * 1024

D_MODEL = 2048
HEAD_DIM = 128
GDN_HEADS = 8
GDN_CONV = 4
GDN_CHUNK = 64
GDN_QK = GDN_HEADS * HEAD_DIM
GDN_V = GDN_HEADS * HEAD_DIM
GDN_CONV_DIM = 2 * GDN_QK + GDN_V
MOBA_HEADS = 4
MOBA_BLOCK = 256
MOBA_TOPK = 3
MOBA_W = MOBA_HEADS * HEAD_DIM
MEM_HEADS = 4
MEM_W = MEM_HEADS * HEAD_DIM
N_BUCKETS = 32
MAX_DISTANCE = 128
N_EXPERTS = 32
TOP_K = 4
D_FF = D_MODEL
SWIGLU_LIMIT = 7.0
SWIGLU_ALPHA = 1.702
EPS = 1e-6
NEG_INF = -1e30

A_W = GDN_CONV_DIM + GDN_V
B_MQ, B_MK, B_CQ, B_MV, B_BA = 0, MOBA_W, 2 * MOBA_W, 3 * MOBA_W, 4 * MOBA_W
B_W = B_BA + LANES
B_NORM_GROUPS = 3 * MOBA_HEADS


def _params(*sem):
    return pltpu.CompilerParams(dimension_semantics=sem, vmem_limit_bytes=VMEM_LIMIT)


def _dot(a, b, **kw):
    return jnp.dot(a, b, preferred_element_type=f32, **kw)


def _dot_nt(a, b, **kw):
    return lax.dot_general(a, b, (((1,), (1,)), ((), ())), preferred_element_type=f32, **kw)


def _dot_tn(a, b, **kw):
    return lax.dot_general(a, b, (((0,), (0,)), ((), ())), preferred_element_type=f32, **kw)


def _rms(x, gain):
    return x * lax.rsqrt(jnp.mean(x * x, axis=-1, keepdims=True) + EPS) * gain


def _silu(x):
    return x * jax.nn.sigmoid(x)


def _norm_matmul_kernel(x_ref, g_ref, w_ref, gain_ref, o_ref, h_ref, *, n_norm):
    @pl.when(pl.program_id(1) == 0)
    def _():
        h_ref[...] = _rms(x_ref[...], g_ref[...]).astype(bf16)

    acc = _dot(h_ref[...], w_ref[...])
    if n_norm == 0:
        o_ref[...] = acc
    else:
        for gi in range(acc.shape[1] // LANES):
            sl = slice(gi * LANES, (gi + 1) * LANES)
            blk = acc[:, sl]
            if gi < n_norm:
                blk = _rms(blk, gain_ref[:, sl])
            o_ref[:, sl] = blk


def _norm_matmul(x, g, w, gain, *, tm, tn, n_norm):
    n, d = x.shape
    wn = w.shape[1]
    return pl.pallas_call(
        functools.partial(_norm_matmul_kernel, n_norm=n_norm),
        grid=(n // tm, wn // tn),
        in_specs=[
            pl.BlockSpec((tm, d), lambda i, j: (i, 0)),
            pl.BlockSpec((1, d), lambda i, j: (0, 0)),
            pl.BlockSpec((d, tn), lambda i, j: (0, j)),
            pl.BlockSpec((1, gain.shape[1]), lambda i, j: (0, 0)),
        ],
        out_specs=pl.BlockSpec((tm, tn), lambda i, j: (i, j)),
        out_shape=jax.ShapeDtypeStruct((n, wn), f32),
        scratch_shapes=[pltpu.VMEM((tm, d), bf16)],
        compiler_params=_params("parallel", "arbitrary"),
        name="norm_matmul",
    )(x, g, w, gain)


def _unit_lower_inverse(lmat):
    c = lmat.shape[0]
    row = lax.broadcasted_iota(i32, (c, c), 0)
    col = lax.broadcasted_iota(i32, (c, c), 1)
    p = jnp.where(row == col, 1.0, 0.0).astype(f32) - lmat
    m = _dot(lmat, lmat, precision=HIGHEST)
    power = 2
    while True:
        p = p + _dot(p, m, precision=HIGHEST)
        power *= 2
        if power >= c:
            return p
        m = _dot(m, m, precision=HIGHEST)


def _gdn_prompt_kernel(qkv_ref, z_ref, ba_ref, convw_ref, alog_ref, dtb_ref, onorm_ref,
                       o_ref, s_out_ref, xp_ref, s_ref):
    c = GDN_CHUNK
    step = pl.program_id(0)

    @pl.when(step == 0)
    def _():
        xp_ref[0:SUBLANES, :] = jnp.zeros((SUBLANES, GDN_CONV_DIM), f32)
        s_ref[...] = jnp.zeros_like(s_ref)

    xp_ref[SUBLANES:SUBLANES + c, :] = qkv_ref[...]
    w = convw_ref[...]
    y = None
    for j in range(GDN_CONV):
        lo = SUBLANES - (GDN_CONV - 1) + j
        term = xp_ref[lo:lo + c, :] * w[j:j + 1, :]
        y = term if y is None else y + term
    y = _silu(y)
    xp_ref[0:SUBLANES, :] = xp_ref[c:c + SUBLANES, :]

    ba = ba_ref[...]
    beta_all = jax.nn.sigmoid(ba)
    g_all = -jnp.exp(alog_ref[...]) * jax.nn.softplus(ba + dtb_ref[...])
    row = lax.broadcasted_iota(i32, (c, c), 0)
    col = lax.broadcasted_iota(i32, (c, c), 1)
    incl = row >= col
    strict = row > col
    gcum_all = _dot(jnp.where(incl, 1.0, 0.0).astype(f32), g_all, precision=HIGHEST)
    gcum_t = gcum_all.T
    onorm = onorm_ref[...]

    for h in range(GDN_HEADS):
        hs = slice(h * HEAD_DIM, (h + 1) * HEAD_DIM)
        q = y[:, h * HEAD_DIM:(h + 1) * HEAD_DIM]
        k = y[:, GDN_QK + h * HEAD_DIM:GDN_QK + (h + 1) * HEAD_DIM]
        v = y[:, 2 * GDN_QK + h * HEAD_DIM:2 * GDN_QK + (h + 1) * HEAD_DIM]
        q = q * lax.rsqrt(jnp.sum(q * q, axis=-1, keepdims=True) + EPS) * (HEAD_DIM ** -0.5)
        k = k * lax.rsqrt(jnp.sum(k * k, axis=-1, keepdims=True) + EPS)
        beta = beta_all[:, h:h + 1]
        gc = gcum_all[:, GDN_HEADS + h:GDN_HEADS + h + 1]
        gr = gcum_t[GDN_HEADS + h:GDN_HEADS + h + 1, :]
        decay = jnp.exp(jnp.where(incl, gc - gr, -jnp.inf))
        kbeta = k * beta
        lmat = _dot_nt(kbeta, k) * jnp.where(strict, decay, 0.0)
        tinv = _unit_lower_inverse(lmat)
        egc = jnp.exp(gc)
        u = _dot(tinv, v * beta)
        wmat = _dot(tinv, kbeta * egc)
        s = s_ref[h]
        v_new = u - _dot(wmat, s)
        intra = _dot_nt(q, k) * decay
        o = _dot(q * egc, s) + _dot(intra, v_new)
        g_last = gcum_all[c - 1:c, GDN_HEADS + h:GDN_HEADS + h + 1]
        s_ref[h] = s * jnp.exp(g_last) + _dot_tn(k * jnp.exp(g_last - gc), v_new)
        o_ref[:, hs] = (_rms(o, onorm) * _silu(z_ref[:, hs])).astype(o_ref.dtype)

    @pl.when(step == pl.num_programs(0) - 1)
    def _():
        s_out_ref[...] = s_ref[...]


def _gdn_prompt(proj_a, proj_b, conv_w, alog_row, dtb_row, onorm, *, n_tok):
    c = GDN_CHUNK
    zblk = GDN_CONV_DIM // GDN_V
    return pl.pallas_call(
        _gdn_prompt_kernel,
        grid=(n_tok // c,),
        in_specs=[
            pl.BlockSpec((c, GDN_CONV_DIM), lambda i: (i, 0)),
            pl.BlockSpec((c, GDN_V), lambda i: (i, zblk)),
            pl.BlockSpec((c, LANES), lambda i: (i, B_BA // LANES)),
            pl.BlockSpec((GDN_CONV, GDN_CONV_DIM), lambda i: (0, 0)),
            pl.BlockSpec((1, LANES), lambda i: (0, 0)),
            pl.BlockSpec((1, LANES), lambda i: (0, 0)),
            pl.BlockSpec((1, HEAD_DIM), lambda i: (0, 0)),
        ],
        out_specs=[
            pl.BlockSpec((c, GDN_V), lambda i: (i, 0)),
            pl.BlockSpec((GDN_HEADS, HEAD_DIM, HEAD_DIM), lambda i: (0, 0, 0)),
        ],
        out_shape=[
            jax.ShapeDtypeStruct((n_tok, GDN_V), bf16),
            jax.ShapeDtypeStruct((GDN_HEADS, HEAD_DIM, HEAD_DIM), f32),
        ],
        scratch_shapes=[
            pltpu.VMEM((c + SUBLANES, GDN_CONV_DIM), f32),
            pltpu.VMEM((GDN_HEADS, HEAD_DIM, HEAD_DIM), f32),
        ],
        compiler_params=_params("arbitrary"),
        name="gdn_prompt",
    )(proj_a, proj_a, proj_b, conv_w, alog_row, dtb_row, onorm)


GDN_S_SEQ = SUBLANES


def _gdn_sample_kernel(q_ref, k_ref, v_ref, z_ref, ba_ref, bq_ref, bk_ref, bv_ref, wq_ref, wk_ref, wv_ref,
                       alog_ref, dtb_ref, onorm_ref, s_ref, o_ref, so_ref, *, n_tok):
    h = pl.program_id(1)
    nb = GDN_S_SEQ
    lane = lax.broadcasted_iota(i32, (nb, LANES), 1)

    def conv(x_ref, buf_ref, w_ref, t):
        y = None
        for j in range(GDN_CONV):
            i = t + j
            row = buf_ref[i] if i < GDN_CONV - 1 else x_ref[i - (GDN_CONV - 1)]
            term = row * w_ref[j:j + 1, :]
            y = term if y is None else y + term
        return _silu(y)

    def lane_col(x, idx):
        return jnp.sum(jnp.where(lane == idx, x, 0.0), axis=1, keepdims=True)

    qs, ks, vs, betas, egs = [], [], [], [], []
    for t in range(n_tok):
        q = conv(q_ref, bq_ref, wq_ref, t)
        k = conv(k_ref, bk_ref, wk_ref, t)
        q = q * lax.rsqrt(jnp.sum(q * q, axis=-1, keepdims=True) + EPS) * (HEAD_DIM ** -0.5)
        k = k * lax.rsqrt(jnp.sum(k * k, axis=-1, keepdims=True) + EPS)
        qs.append(q.T)
        ks.append(k.T)
        vs.append(conv(v_ref, bv_ref, wv_ref, t))
        ba = ba_ref[t]
        betas.append(lane_col(jax.nn.sigmoid(ba), h))
        g = -jnp.exp(alog_ref[...]) * jax.nn.softplus(ba + dtb_ref[...])
        egs.append(jnp.exp(lane_col(g, GDN_HEADS + h)))

    o_rows = [[None] * nb for _ in range(n_tok)]
    for b in range(nb):
        s = s_ref[b]
        for t in range(n_tok):
            kc = ks[t][:, b:b + 1]
            qc = qs[t][:, b:b + 1]
            s = s * egs[t][b:b + 1, :]
            ks_row = jnp.sum(kc * s, axis=0, keepdims=True)
            delta = (vs[t][b:b + 1, :] - ks_row) * betas[t][b:b + 1, :]
            s = s + kc * delta
            o_rows[t][b] = jnp.sum(qc * s, axis=0, keepdims=True)
        so_ref[b] = s
    for t in range(n_tok):
        o = jnp.concatenate(o_rows[t], axis=0)
        o_ref[t] = _rms(o, onorm_ref[...]) * _silu(z_ref[t])


def _gdn_sample(xa, ba, conv_buf, conv_w, alog_row, dtb_row, onorm, state):
    n_tok, n_seq, _ = xa.shape
    nb = GDN_S_SEQ
    hq, hk, hv, hz = 0, GDN_HEADS, 2 * GDN_HEADS, 3 * GDN_HEADS

    def xspec(off, rows):
        return pl.BlockSpec((rows, nb, HEAD_DIM), lambda i, h: (0, i, off + h))

    def wspec(off):
        return pl.BlockSpec((GDN_CONV, HEAD_DIM), lambda i, h: (0, off + h))

    row = pl.BlockSpec((1, LANES), lambda i, h: (0, 0))
    sspec = pl.BlockSpec((nb, None, HEAD_DIM, HEAD_DIM), lambda i, h: (i, h, 0, 0))
    return pl.pallas_call(
        functools.partial(_gdn_sample_kernel, n_tok=n_tok),
        grid=(n_seq // nb, GDN_HEADS),
        in_specs=[xspec(hq, n_tok), xspec(hk, n_tok), xspec(hv, n_tok), xspec(hz, n_tok),
                  pl.BlockSpec((n_tok, nb, LANES), lambda i, h: (0, i, 0)),
                  xspec(hq, GDN_CONV - 1), xspec(hk, GDN_CONV - 1), xspec(hv, GDN_CONV - 1),
                  wspec(hq), wspec(hk), wspec(hv), row, row, row, sspec],
        out_specs=[pl.BlockSpec((n_tok, nb, HEAD_DIM), lambda i, h: (0, i, h)), sspec],
        out_shape=[jax.ShapeDtypeStruct((n_tok, n_seq, GDN_V), f32),
                   jax.ShapeDtypeStruct(state.shape, f32)],
        compiler_params=_params("parallel", "arbitrary"),
        name="gdn_sample",
    )(xa, xa, xa, xa, ba, conv_buf, conv_buf, conv_buf, conv_w, conv_w, conv_w, alog_row, dtb_row, onorm, state)


def _bucket_thresholds():
    exact = N_BUCKETS // 2
    d = np.arange(0, 2 * MAX_DISTANCE)
    val = np.log(np.maximum(d, 1).astype(np.float64) / exact) / math.log(MAX_DISTANCE / exact) * (N_BUCKETS - exact)
    frac = np.abs(val - np.round(val))[exact + 1:MAX_DISTANCE]
    assert frac.min() > 1e-3, "a bucket boundary sits on an integer distance"
    bucket = np.where(d < exact, d, np.minimum(exact + val.astype(np.int64), N_BUCKETS - 1))
    assert np.all(np.diff(bucket) >= 0)
    return [int(np.argmax(bucket >= b)) for b in range(N_BUCKETS)]


_BUCKET_THR = _bucket_thresholds()


def _bias_from_dist(dist, rb_ref, h):
    v = jnp.full(dist.shape, rb_ref[0, h], f32)
    for b in range(1, N_BUCKETS):
        v = jnp.where(dist >= _BUCKET_THR[b], rb_ref[b, h], v)
    return v


def _topk_mask(s, blk, k):
    nblk = s.shape[1]
    sel = jnp.zeros(s.shape, f32)
    for _ in range(k):
        m = jnp.max(s, axis=1, keepdims=True)
        cand = jnp.where((s == m) & (m > -jnp.inf), blk, nblk)
        pick = blk == jnp.min(cand, axis=1, keepdims=True)
        sel = jnp.where(pick, 1.0, sel)
        s = jnp.where(pick, -jnp.inf, s)
    return sel


def _block_mean_kernel(k_ref, o_ref):
    o_ref[0] = jnp.mean(k_ref[...], axis=0, keepdims=True)


def _block_mean(proj_b, n_blk):
    return pl.pallas_call(
        _block_mean_kernel,
        grid=(n_blk,),
        in_specs=[pl.BlockSpec((MOBA_BLOCK, MOBA_W), lambda i: (i, B_MK // MOBA_W))],
        out_specs=pl.BlockSpec((1, 1, MOBA_W), lambda i: (i, 0, 0)),
        out_shape=jax.ShapeDtypeStruct((n_blk, 1, MOBA_W), f32),
        compiler_params=_params("parallel"),
        name="moba_block_mean",
    )(proj_b)


def _moba_select_kernel(q_ref, km_ref, o_ref):
    own = pl.program_id(0)
    nblk = km_ref.shape[0]
    blk = lax.broadcasted_iota(i32, (MOBA_BLOCK, nblk), 1)
    outs = []
    for h in range(MOBA_HEADS):
        hs = slice(h * HEAD_DIM, (h + 1) * HEAD_DIM)
        s = _dot_nt(q_ref[:, hs], km_ref[:, hs], precision=HIGHEST)
        s = jnp.where(blk < own, s, -jnp.inf)
        outs.append(_topk_mask(s, blk, MOBA_TOPK))
    pad = LANES - MOBA_HEADS * nblk
    if pad:
        outs.append(jnp.zeros((MOBA_BLOCK, pad), f32))
    o_ref[...] = jnp.concatenate(outs, axis=1)


def _moba_select(proj_b, kmean, n_blk):
    assert n_blk * MOBA_HEADS <= LANES
    return pl.pallas_call(
        _moba_select_kernel,
        grid=(n_blk,),
        in_specs=[pl.BlockSpec((MOBA_BLOCK, MOBA_W), lambda i: (i, B_MQ // MOBA_W)),
                  pl.BlockSpec((n_blk, MOBA_W), lambda i: (0, 0))],
        out_specs=pl.BlockSpec((MOBA_BLOCK, LANES), lambda i: (i, 0)),
        out_shape=jax.ShapeDtypeStruct((n_blk * MOBA_BLOCK, LANES), f32),
        compiler_params=_params("parallel"),
        name="moba_select",
    )(proj_b, kmean)


def _moba_prompt_kernel(qi_ref, kj_ref, rb_ref, q_ref, k_ref, v_ref, sel_ref, o_ref,
                        bias_ref, m_ref, l_ref, acc_ref, *, n_blk):
    step = pl.program_id(0)
    qi = qi_ref[step]
    kj = kj_ref[step]
    nq = MOBA_BLOCK

    @pl.when(step == 0)
    def _():
        r = lax.broadcasted_iota(i32, (nq, nq), 0)
        c = lax.broadcasted_iota(i32, (nq, nq), 1)
        for h in range(MOBA_HEADS):
            bias_ref[h, 0] = jnp.where(c <= r, _bias_from_dist(r - c, rb_ref, h), NEG_INF)
            bias_ref[h, 1] = _bias_from_dist(r - c + nq, rb_ref, h)
            bias_ref[h, 2] = jnp.full((nq, nq), rb_ref[N_BUCKETS - 1, h], f32)

    first = kj == qi

    @pl.when(first)
    def _():
        m_ref[...] = jnp.full(m_ref.shape, NEG_INF, f32)
        l_ref[...] = jnp.zeros_like(l_ref)
        acc_ref[...] = jnp.zeros_like(acc_ref)

    slot = jnp.where(first, 0, jnp.where(kj == qi - 1, 1, 2))
    sel = sel_ref[...]
    lane = lax.broadcasted_iota(i32, sel.shape, 1)
    for h in range(MOBA_HEADS):
        hs = slice(h * HEAD_DIM, (h + 1) * HEAD_DIM)
        s = _dot_nt(q_ref[:, hs].astype(bf16), k_ref[:, hs].astype(bf16)) * (HEAD_DIM ** -0.5) + bias_ref[h, slot]
        picked = jnp.max(jnp.where(lane == h * n_blk + kj, sel, 0.0), axis=1, keepdims=True)
        s = jnp.where((picked > 0.0) | first, s, NEG_INF)
        m_prev = m_ref[h]
        m_new = jnp.maximum(m_prev, jnp.max(s, axis=1, keepdims=True))
        alpha = jnp.exp(m_prev - m_new)
        p = jnp.exp(s - m_new)
        l_ref[h] = alpha * l_ref[h] + jnp.sum(p, axis=1, keepdims=True)
        acc_ref[h] = alpha * acc_ref[h] + _dot(p.astype(bf16), v_ref[:, hs].astype(bf16))
        m_ref[h] = m_new

    @pl.when(kj == 0)
    def _():
        for h in range(MOBA_HEADS):
            hs = slice(h * HEAD_DIM, (h + 1) * HEAD_DIM)
            o_ref[:, hs] = (acc_ref[h] / l_ref[h]).astype(o_ref.dtype)


def _moba_prompt(proj_b, sel, rel_bias, n_blk):
    qi = np.concatenate([np.full(i + 1, i) for i in range(n_blk)]).astype(np.int32)
    kj = np.concatenate([np.arange(i, -1, -1) for i in range(n_blk)]).astype(np.int32)
    nq = MOBA_BLOCK
    grid_spec = pltpu.PrefetchScalarGridSpec(
        num_scalar_prefetch=2,
        grid=(len(qi),),
        in_specs=[
            pl.BlockSpec(memory_space=pltpu.SMEM),
            pl.BlockSpec((nq, MOBA_W), lambda s, qi, kj: (qi[s], B_MQ // MOBA_W)),
            pl.BlockSpec((nq, MOBA_W), lambda s, qi, kj: (kj[s], B_MK // MOBA_W)),
            pl.BlockSpec((nq, MOBA_W), lambda s, qi, kj: (kj[s], B_MV // MOBA_W)),
            pl.BlockSpec((nq, LANES), lambda s, qi, kj: (qi[s], 0)),
        ],
        out_specs=pl.BlockSpec((nq, MOBA_W), lambda s, qi, kj: (qi[s], 0)),
        scratch_shapes=[
            pltpu.VMEM((MOBA_HEADS, 3, nq, nq), f32),
            pltpu.VMEM((MOBA_HEADS, nq, 1), f32),
            pltpu.VMEM((MOBA_HEADS, nq, 1), f32),
            pltpu.VMEM((MOBA_HEADS, nq, HEAD_DIM), f32),
        ],
    )
    return pl.pallas_call(
        functools.partial(_moba_prompt_kernel, n_blk=n_blk),
        grid_spec=grid_spec,
        out_shape=jax.ShapeDtypeStruct((n_blk * nq, MOBA_W), bf16),
        compiler_params=_params("arbitrary"),
        name="moba_prompt",
    )(jnp.asarray(qi), jnp.asarray(kj), rel_bias, proj_b, proj_b, proj_b, sel)


def _moba_sample_kernel(pt_ref, rb_ref, q_ref, kn_ref, vn_ref, *rest, n_pages, page):
    kp = rest[:n_pages]
    vp = rest[n_pages:2 * n_pages]
    o_ref, kbuf, vbuf, bias_ref = rest[2 * n_pages:]
    n_tok = q_ref.shape[0]
    past = n_pages * page
    n_blk = past // MOBA_BLOCK
    ppb = MOBA_BLOCK // page

    @pl.when(pl.program_id(0) == 0)
    def _():
        t = lax.broadcasted_iota(i32, (n_tok, past), 0)
        p = lax.broadcasted_iota(i32, (n_tok, past), 1)
        for h in range(MOBA_HEADS):
            bias_ref[h] = _bias_from_dist(past + t - p, rb_ref, h)

    q = q_ref[...]
    blk = lax.broadcasted_iota(i32, (n_tok, n_blk), 1)
    tq = lax.broadcasted_iota(i32, (n_tok, n_tok), 0)
    tk = lax.broadcasted_iota(i32, (n_tok, n_tok), 1)
    scale = HEAD_DIM ** -0.5
    for h in range(MOBA_HEADS):
        hs = slice(h * HEAD_DIM, (h + 1) * HEAD_DIM)
        sums = []
        for p in range(n_pages):
            kpage = kp[p][:, h, :]
            kbuf[p * page:(p + 1) * page, hs] = kpage.astype(bf16)
            vbuf[p * page:(p + 1) * page, hs] = vp[p][:, h, :].astype(bf16)
            sums.append(jnp.sum(kpage, axis=0, keepdims=True))
        kmean = jnp.concatenate(
            [sum(sums[n * ppb + 1:(n + 1) * ppb], sums[n * ppb]) / MOBA_BLOCK for n in range(n_blk)], axis=0)
        qh = q[:, hs]
        sel = _topk_mask(_dot_nt(qh, kmean, precision=HIGHEST), blk, MOBA_TOPK)
        sel_full = jnp.concatenate(
            [jnp.broadcast_to(sel[:, n:n + 1], (n_tok, MOBA_BLOCK)) for n in range(n_blk)], axis=1)
        qb = qh.astype(bf16)
        lp = _dot_nt(qb, kbuf[:, hs]) * scale + bias_ref[h]
        lp = jnp.where(sel_full > 0.0, lp, NEG_INF)
        ln = _dot_nt(qb, kn_ref[:, hs].astype(bf16)) * scale + _bias_from_dist(tq - tk, rb_ref, h)
        ln = jnp.where(tk <= tq, ln, NEG_INF)
        m = jnp.maximum(jnp.max(lp, axis=1, keepdims=True), jnp.max(ln, axis=1, keepdims=True))
        pp = jnp.exp(lp - m)
        pn = jnp.exp(ln - m)
        den = jnp.sum(pp, axis=1, keepdims=True) + jnp.sum(pn, axis=1, keepdims=True)
        num = _dot(pp.astype(bf16), vbuf[:, hs]) + _dot(pn.astype(bf16), vn_ref[:, hs].astype(bf16))
        o_ref[:, hs] = num / den


def _moba_sample(proj_b3, pool_k, pool_v, page_table, rel_bias):
    n_seq, n_tok, _ = proj_b3.shape
    n_pages = page_table.shape[1]
    page = pool_k.shape[1]

    def new_spec(col):
        return pl.BlockSpec((None, n_tok, MOBA_W), lambda b, pt: (b, 0, col // MOBA_W))

    def page_spec(p):
        return pl.BlockSpec((None, page, MOBA_HEADS, HEAD_DIM), lambda b, pt: (pt[b, p], 0, 0, 0))

    grid_spec = pltpu.PrefetchScalarGridSpec(
        num_scalar_prefetch=1,
        grid=(n_seq,),
        in_specs=[pl.BlockSpec(memory_space=pltpu.SMEM), new_spec(B_MQ), new_spec(B_MK), new_spec(B_MV)]
        + [page_spec(p) for p in range(n_pages)] * 2,
        out_specs=pl.BlockSpec((None, n_tok, MOBA_W), lambda b, pt: (b, 0, 0)),
        scratch_shapes=[
            pltpu.VMEM((n_pages * page, MOBA_W), bf16),
            pltpu.VMEM((n_pages * page, MOBA_W), bf16),
            pltpu.VMEM((MOBA_HEADS, n_tok, n_pages * page), f32),
        ],
    )
    return pl.pallas_call(
        functools.partial(_moba_sample_kernel, n_pages=n_pages, page=page),
        grid_spec=grid_spec,
        out_shape=jax.ShapeDtypeStruct((n_seq, n_tok, MOBA_W), f32),
        compiler_params=_params("arbitrary"),
        name="moba_sample",
    )(page_table, rel_bias, proj_b3, proj_b3, proj_b3, *([pool_k] * n_pages), *([pool_v] * n_pages))


def _head(ref, h):
    return ref[:, h, :] if len(ref.shape) == 3 else ref[:, h * HEAD_DIM:(h + 1) * HEAD_DIM]


def _mem_attn_kernel(q_ref, k_ref, v_ref, o_ref):
    for h in range(MEM_HEADS):
        hs = slice(h * HEAD_DIM, (h + 1) * HEAD_DIM)
        s = _dot_nt(q_ref[:, hs].astype(bf16), _head(k_ref, h).astype(bf16)) * (HEAD_DIM ** -0.5)
        p = jnp.exp(s - jnp.max(s, axis=1, keepdims=True))
        num = _dot(p.astype(bf16), _head(v_ref, h).astype(bf16))
        o_ref[:, hs] = (num / jnp.sum(p, axis=1, keepdims=True)).astype(o_ref.dtype)


def _mem_attn_prompt(proj_b, mem_kv, *, n_tok, tq):
    n_mem = mem_kv.shape[0]
    return pl.pallas_call(
        _mem_attn_kernel,
        grid=(n_tok // tq,),
        in_specs=[pl.BlockSpec((tq, MEM_W), lambda i: (i, B_CQ // MEM_W)),
                  pl.BlockSpec((n_mem, MEM_W), lambda i: (0, 0)),
                  pl.BlockSpec((n_mem, MEM_W), lambda i: (0, 1))],
        out_specs=pl.BlockSpec((tq, MEM_W), lambda i: (i, 0)),
        out_shape=jax.ShapeDtypeStruct((n_tok, MEM_W), bf16),
        compiler_params=_params("parallel"),
        name="mem_attn_prompt",
    )(proj_b, mem_kv, mem_kv)


def _mem_attn_sample(proj_b3, mem_k, mem_v):
    n_seq, n_tok, _ = proj_b3.shape
    n_mem = mem_k.shape[1]
    return pl.pallas_call(
        _mem_attn_kernel,
        grid=(n_seq,),
        in_specs=[pl.BlockSpec((None, n_tok, MEM_W), lambda b: (b, 0, B_CQ // MEM_W)),
                  pl.BlockSpec((None, n_mem, MEM_HEADS, HEAD_DIM), lambda b: (b, 0, 0, 0)),
                  pl.BlockSpec((None, n_mem, MEM_HEADS, HEAD_DIM), lambda b: (b, 0, 0, 0))],
        out_specs=pl.BlockSpec((None, n_tok, MEM_W), lambda b: (b, 0, 0)),
        out_shape=jax.ShapeDtypeStruct((n_seq, n_tok, MEM_W), f32),
        compiler_params=_params("parallel"),
        name="mem_attn_sample",
    )(proj_b3, mem_k, mem_v)


def _split_bf16(x):
    hi = x.astype(bf16)
    return hi, (x - hi.astype(f32)).astype(bf16)


def _outproj_router_kernel(x_ref, og_ref, om_ref, oc_ref, w_ref, g_ref, rw_ref, rb_ref,
                           y_ref, h_ref, ti_ref, tw_ref):
    mix = (_dot(og_ref[...], w_ref[0:GDN_V, :])
           + _dot(om_ref[...], w_ref[GDN_V:GDN_V + MOBA_W, :])
           + _dot(oc_ref[...], w_ref[GDN_V + MOBA_W:, :]))
    y = x_ref[...] + mix
    y_ref[...] = y
    h = _rms(y, g_ref[...])
    for c in range(h.shape[1] // LANES):
        h_ref[:, c, :] = h[:, c * LANES:(c + 1) * LANES]
    h_hi, h_lo = _split_bf16(h)
    w_hi, w_lo = _split_bf16(rw_ref[...])
    logits = _dot(h_hi, w_hi) + (_dot(h_hi, w_lo) + _dot(h_lo, w_hi)) + rb_ref[...]
    lane = lax.broadcasted_iota(i32, logits.shape, 1)
    s = jnp.where(lane < N_EXPERTS, logits, -jnp.inf)
    vals, idxs = [], []
    for _ in range(TOP_K):
        m = jnp.max(s, axis=1, keepdims=True)
        idx = jnp.min(jnp.where(s == m, lane, LANES), axis=1, keepdims=True)
        vals.append(m)
        idxs.append(idx)
        s = jnp.where(lane == idx, -jnp.inf, s)
    exps = [jnp.exp(v - vals[0]) for v in vals]
    den = exps[0]
    for e in exps[1:]:
        den = den + e
    ti = jnp.zeros(logits.shape, i32)
    tw = jnp.zeros(logits.shape, f32)
    for r in range(TOP_K):
        ti = jnp.where(lane == r, idxs[r], ti)
        tw = jnp.where(lane == r, exps[r] / den, tw)
    ti_ref[...] = ti
    tw_ref[...] = tw


def _outproj_router(x, og, om, oc, w_out, g, rw, rb, *, tm):
    n, d = x.shape
    row = lambda w: pl.BlockSpec((tm, w), lambda i: (i, 0))
    full = lambda a: pl.BlockSpec(a.shape, lambda i: (0,) * a.ndim)
    return pl.pallas_call(
        _outproj_router_kernel,
        grid=(n // tm,),
        in_specs=[row(d), row(GDN_V), row(MOBA_W), row(MEM_W), full(w_out), full(g), full(rw), full(rb)],
        out_specs=[row(d), pl.BlockSpec((tm, d // LANES, LANES), lambda i: (i, 0, 0)), row(LANES), row(LANES)],
        out_shape=[jax.ShapeDtypeStruct((n, d), f32), jax.ShapeDtypeStruct((n, d // LANES, LANES), f32),
                   jax.ShapeDtypeStruct((n, LANES), i32), jax.ShapeDtypeStruct((n, LANES), f32)],
        compiler_params=_params("parallel"),
        name="outproj_router",
    )(x, og, om, oc, w_out, g, rw, rb)


GATHER_UNROLL = 8


def _issue_slab_gather(idx_ref, idx_base, src_ref, dst_ref, sem, count):
    def body(j, carry):
        pltpu.make_async_copy(src_ref.at[idx_ref[idx_base + j]], dst_ref.at[j], sem).start()
        return carry

    lax.fori_loop(0, count, body, 0, unroll=GATHER_UNROLL)


def _wait_slab_gather(src_ref, dst_ref, sem, count):
    pltpu.make_async_copy(src_ref.at[pl.ds(0, count)], dst_ref, sem).wait()


def _gather_x_kernel(idx_ref, src_ref, o_ref, buf_ref, sem, *, rows):
    i = pl.program_id(0)
    slot = i % 2

    @pl.when(i == 0)
    def _():
        _issue_slab_gather(idx_ref, 0, src_ref, buf_ref.at[0], sem.at[0], rows)

    @pl.when(i + 1 < pl.num_programs(0))
    def _():
        _issue_slab_gather(idx_ref, (i + 1) * rows, src_ref, buf_ref.at[1 - slot], sem.at[1 - slot], rows)

    _wait_slab_gather(src_ref, buf_ref.at[slot], sem.at[slot], rows)
    for c in range(buf_ref.shape[2]):
        o_ref[:, c * LANES:(c + 1) * LANES] = buf_ref[slot, :, c, :].astype(o_ref.dtype)


def _gather_x(src, idx, *, rows):
    n_out = idx.shape[0]
    g = src.shape[1]
    grid_spec = pltpu.PrefetchScalarGridSpec(
        num_scalar_prefetch=1,
        grid=(n_out // rows,),
        in_specs=[pl.BlockSpec(memory_space=pl.ANY)],
        out_specs=pl.BlockSpec((rows, g * LANES), lambda i, idx: (i, 0)),
        scratch_shapes=[pltpu.VMEM((2, rows, g, LANES), f32), pltpu.SemaphoreType.DMA((2,))],
    )
    return pl.pallas_call(
        functools.partial(_gather_x_kernel, rows=rows),
        grid_spec=grid_spec,
        out_shape=jax.ShapeDtypeStruct((n_out, g * LANES), bf16),
        compiler_params=_params("arbitrary"),
        name="moe_gather_x",
    )(idx, src)


def _visit_store(o_ref, new, lo, hi):
    @pl.when(lo == 0)
    def _():
        o_ref[...] = new

    @pl.when(lo > 0)
    def _():
        r = lax.broadcasted_iota(i32, new.shape, 0)
        o_ref[...] = jnp.where((r >= lo) & (r < hi), new, o_ref[...])


def _moe_gate_up_kernel(tv_ref, ev_ref, lo_ref, hi_ref, x_ref, wg_ref, wu_ref, bg_ref, bu_ref, o_ref):
    v = pl.program_id(1)
    lo = lo_ref[v]
    hi = hi_ref[v]

    @pl.when(hi > lo)
    def _():
        x = x_ref[...].astype(bf16)
        gate = _dot(x, wg_ref[...].astype(bf16)) + bg_ref[...]
        up = _dot(x, wu_ref[...].astype(bf16)) + bu_ref[...]
        gate = jnp.minimum(gate, SWIGLU_LIMIT)
        up = jnp.clip(up, -SWIGLU_LIMIT, SWIGLU_LIMIT)
        act = gate * jax.nn.sigmoid(SWIGLU_ALPHA * gate) * (up + 1.0)
        _visit_store(o_ref, act.astype(o_ref.dtype), lo, hi)


def _moe_gate_up(xs, w_gu, b_gu3, tv, ev, lo, hi, *, tm, tf):
    rows, d = xs.shape
    nf = D_FF // tf
    grid_spec = pltpu.PrefetchScalarGridSpec(
        num_scalar_prefetch=4,
        grid=(nf, tv.shape[0]),
        in_specs=[
            pl.BlockSpec((tm, d), lambda f, v, tv, ev, lo, hi: (tv[v], 0)),
            pl.BlockSpec((None, d, tf), lambda f, v, tv, ev, lo, hi: (ev[v], 0, f)),
            pl.BlockSpec((None, d, tf), lambda f, v, tv, ev, lo, hi: (ev[v], 0, nf + f)),
            pl.BlockSpec((None, 1, tf), lambda f, v, tv, ev, lo, hi: (ev[v], 0, f)),
            pl.BlockSpec((None, 1, tf), lambda f, v, tv, ev, lo, hi: (ev[v], 0, nf + f)),
        ],
        out_specs=pl.BlockSpec((tm, tf), lambda f, v, tv, ev, lo, hi: (tv[v], f)),
    )
    return pl.pallas_call(
        _moe_gate_up_kernel,
        grid_spec=grid_spec,
        out_shape=jax.ShapeDtypeStruct((rows, D_FF), bf16),
        compiler_params=_params("arbitrary", "arbitrary"),
        name="moe_gate_up",
    )(tv, ev, lo, hi, xs, w_gu, w_gu, b_gu3, b_gu3)


def _moe_down_kernel(tv_ref, ev_ref, lo_ref, hi_ref, a_ref, w_ref, b_ref, o_ref):
    v = pl.program_id(1)
    lo = lo_ref[v]
    hi = hi_ref[v]

    @pl.when(hi > lo)
    def _():
        y = _dot(a_ref[...], w_ref[...].astype(bf16)) + b_ref[...]
        groups = [y[:, c * LANES:(c + 1) * LANES] for c in range(o_ref.shape[1])]

        @pl.when(lo == 0)
        def _():
            for c, blk in enumerate(groups):
                o_ref[:, c, :] = blk

        @pl.when(lo > 0)
        def _():
            r = lax.broadcasted_iota(i32, groups[0].shape, 0)
            mine = (r >= lo) & (r < hi)
            for c, blk in enumerate(groups):
                o_ref[:, c, :] = jnp.where(mine, blk, o_ref[:, c, :])


def _moe_down(act, w_dn, b_dn3, tv, ev, lo, hi, *, tm, tn):
    rows, dff = act.shape
    d = w_dn.shape[2]
    grid_spec = pltpu.PrefetchScalarGridSpec(
        num_scalar_prefetch=4,
        grid=(d // tn, tv.shape[0]),
        in_specs=[
            pl.BlockSpec((tm, dff), lambda n, v, tv, ev, lo, hi: (tv[v], 0)),
            pl.BlockSpec((None, dff, tn), lambda n, v, tv, ev, lo, hi: (ev[v], 0, n)),
            pl.BlockSpec((None, 1, tn), lambda n, v, tv, ev, lo, hi: (ev[v], 0, n)),
        ],
        out_specs=pl.BlockSpec((tm, tn // LANES, LANES), lambda n, v, tv, ev, lo, hi: (tv[v], n, 0)),
    )
    return pl.pallas_call(
        _moe_down_kernel,
        grid_spec=grid_spec,
        out_shape=jax.ShapeDtypeStruct((rows, d // LANES, LANES), f32),
        compiler_params=_params("arbitrary", "arbitrary"),
        name="moe_down",
    )(tv, ev, lo, hi, act, w_dn, b_dn3)


def _moe_combine_kernel(pos_ref, y_ref, tw_ref, ys_ref, o_ref, buf_ref, sem, *, rows):
    i = pl.program_id(0)
    slot = i % 2
    n_tok = pl.num_programs(0) * rows

    def issue(tile, s):
        for k in range(TOP_K):
            _issue_slab_gather(pos_ref, k * n_tok + tile * rows, ys_ref, buf_ref.at[s, k], sem.at[s], rows)

    @pl.when(i == 0)
    def _():
        issue(0, 0)

    @pl.when(i + 1 < pl.num_programs(0))
    def _():
        issue(i + 1, 1 - slot)

    for k in range(TOP_K):
        _wait_slab_gather(ys_ref, buf_ref.at[slot, k], sem.at[slot], rows)
    tw = tw_ref[...]
    for c in range(buf_ref.shape[3]):
        cs = slice(c * LANES, (c + 1) * LANES)
        acc = y_ref[:, cs]
        for k in range(TOP_K):
            acc = acc + buf_ref[slot, k, :, c, :] * tw[:, k:k + 1]
        o_ref[:, cs] = acc


def _moe_combine(y1, ys, pos_kmajor, tw, *, rows):
    n, d = y1.shape
    g = ys.shape[1]
    grid_spec = pltpu.PrefetchScalarGridSpec(
        num_scalar_prefetch=1,
        grid=(n // rows,),
        in_specs=[pl.BlockSpec((rows, d), lambda i, pos: (i, 0)),
                  pl.BlockSpec((rows, LANES), lambda i, pos: (i, 0)),
                  pl.BlockSpec(memory_space=pl.ANY)],
        out_specs=pl.BlockSpec((rows, d), lambda i, pos: (i, 0)),
        scratch_shapes=[pltpu.VMEM((2, TOP_K, rows, g, LANES), f32), pltpu.SemaphoreType.DMA((2,))],
    )
    return pl.pallas_call(
        functools.partial(_moe_combine_kernel, rows=rows),
        grid_spec=grid_spec,
        out_shape=jax.ShapeDtypeStruct((n, d), f32),
        compiler_params=_params("arbitrary"),
        name="moe_combine",
    )(pos_kmajor, y1, tw, ys)


def _route(top_i, *, tm):
    e = top_i.reshape(-1)
    a = e.shape[0]
    order = jnp.argsort(e, stable=True).astype(i32)
    tok_sorted = order // TOP_K
    pos = jnp.zeros((a,), i32).at[order].set(jnp.arange(a, dtype=i32))
    counts = jnp.zeros((N_EXPERTS,), i32).at[e].add(1)
    ends = jnp.cumsum(counts)
    starts = ends - counts
    first_tile = starts // tm
    last_tile = jnp.maximum(ends - 1, 0) // tm
    n_vis = jnp.where(counts > 0, last_tile - first_tile + 1, 0)
    vis_end = jnp.cumsum(n_vis)
    vis_start = vis_end - n_vis
    total = vis_end[-1]
    v = jnp.arange(a // tm + N_EXPERTS, dtype=i32)
    vc = jnp.minimum(v, total - 1)
    ev = jnp.searchsorted(vis_end, vc, side="right").astype(i32)
    tv = first_tile[ev] + (vc - vis_start[ev])
    lo = jnp.maximum(starts[ev], tv * tm) - tv * tm
    hi = jnp.minimum(ends[ev], (tv + 1) * tm) - tv * tm
    valid = v < total
    return tok_sorted, pos, tv, ev, jnp.where(valid, lo, 0), jnp.where(valid, hi, 0)


MOE_TM = 256
MOE_TF = 512
MOE_TN = 1024
GATHER_ROWS = 256
COMBINE_ROWS = 128


def _moe(y1, h_slabs, top_i, top_w, w_gu, b_gu, w_dn, b_dn):
    n_tok = y1.shape[0]
    tok_sorted, pos, tv, ev, lo, hi = _route(top_i[:, :TOP_K], tm=MOE_TM)
    xs = _gather_x(h_slabs, tok_sorted, rows=GATHER_ROWS)
    act = _moe_gate_up(xs, w_gu, b_gu[:, None, :], tv, ev, lo, hi, tm=MOE_TM, tf=MOE_TF)
    ys = _moe_down(act, w_dn, b_dn[:, None, :], tv, ev, lo, hi, tm=MOE_TM, tn=MOE_TN)
    pos_kmajor = pos.reshape(n_tok, TOP_K).T.reshape(-1)
    return _moe_combine(y1, ys, pos_kmajor, top_w, rows=COMBINE_ROWS)


def kernel(x_prompt, x_sample, cache_moba_k, cache_moba_v, state_gdn, state_gdn_conv, cache_mem_k, cache_mem_v, page_table, mem_prompt, rel_bias, norm_mix, w_in, conv_w, a_log, dt_bias, gdn_o_norm, moba_q_norm, moba_k_norm, mem_q_norm, mem_norm, w_mem_kv, mem_k_norm, w_out, norm_ffn, router_w, router_b, w_gu, b_gu, w_dn, b_dn):
    assert x_prompt.shape[0] == 1 and all(a.shape[0] == 1 for a in (w_in, w_out, w_gu, w_dn, state_gdn))
    n_p = x_prompt.shape[1]
    n_seq, n_st = x_sample.shape[:2]
    n_s = n_seq * n_st
    d = x_prompt.shape[2]
    n_blk = n_p // MOBA_BLOCK
    x_all = jnp.concatenate([x_prompt.reshape(n_p, d), x_sample.reshape(n_s, d)], axis=0)

    w = w_in[0]
    off_beta = A_W
    off_moba = off_beta + 2 * GDN_HEADS
    off_mem = off_moba + 3 * MOBA_W
    w_a = w[:, :A_W].astype(bf16)
    w_b = jnp.concatenate(
        [w[:, off_moba:off_moba + MOBA_W], w[:, off_moba + MOBA_W:off_moba + 2 * MOBA_W], w[:, off_mem:],
         w[:, off_moba + 2 * MOBA_W:off_mem], w[:, off_beta:off_moba],
         jnp.zeros((d, LANES - 2 * GDN_HEADS), w.dtype)], axis=1).astype(bf16)
    gain_b = jnp.concatenate([jnp.tile(moba_q_norm[0], MOBA_HEADS), jnp.tile(moba_k_norm[0], MOBA_HEADS),
                              jnp.tile(mem_q_norm[0], MEM_HEADS)])[None]
    proj_a = _norm_matmul(x_all, norm_mix, w_a, gain_b, tm=512, tn=1024, n_norm=0)
    proj_b = _norm_matmul(x_all, norm_mix, w_b, gain_b, tm=256, tn=B_W, n_norm=B_NORM_GROUPS)
    proj_b3 = proj_b[n_p:].reshape(n_seq, n_st, B_W)

    pad_row = lambda a: jnp.zeros((1, LANES), f32).at[0, GDN_HEADS:2 * GDN_HEADS].set(a[0])
    alog_row, dtb_row = pad_row(a_log), pad_row(dt_bias)
    o_gdn_p, p_gdn = _gdn_prompt(proj_a, proj_b, conv_w[0], alog_row, dtb_row, gdn_o_norm, n_tok=n_p)
    p_conv = proj_a[n_p - (GDN_CONV - 1):n_p, :GDN_CONV_DIM]
    xa_s = proj_a[n_p:].reshape(n_seq, n_st, A_W)
    o_gdn_s, s_gdn = _gdn_sample(jnp.swapaxes(xa_s, 0, 1), jnp.swapaxes(proj_b3[..., B_BA:], 0, 1),
                                 jnp.swapaxes(state_gdn_conv[0], 0, 1), conv_w[0], alog_row, dtb_row,
                                 gdn_o_norm, state_gdn[0])
    o_gdn_s = jnp.swapaxes(o_gdn_s, 0, 1).reshape(n_s, GDN_V).astype(bf16)
    s_conv = jnp.concatenate([state_gdn_conv[0], xa_s[..., :GDN_CONV_DIM]], axis=1)[:, n_st:]

    kmean = _block_mean(proj_b, n_blk).reshape(n_blk, MOBA_W)
    sel = _moba_select(proj_b, kmean, n_blk)
    o_moba_p = _moba_prompt(proj_b, sel, rel_bias, n_blk)
    o_moba_s = _moba_sample(proj_b3, cache_moba_k[0], cache_moba_v[0], page_table, rel_bias)
    o_moba_s = o_moba_s.reshape(n_s, MOBA_W).astype(bf16)

    n_mem = mem_prompt.shape[1]
    mem_kv = _norm_matmul(mem_prompt[0], mem_norm, w_mem_kv[0].astype(bf16),
                          jnp.tile(mem_k_norm[0], MEM_HEADS)[None], tm=n_mem, tn=2 * MEM_W, n_norm=MEM_HEADS)
    o_mem_p = _mem_attn_prompt(proj_b, mem_kv, n_tok=n_p, tq=512)
    o_mem_s = _mem_attn_sample(proj_b3, cache_mem_k[0], cache_mem_v[0])
    o_mem_s = o_mem_s.reshape(n_s, MEM_W).astype(bf16)

    rw = jnp.pad(router_w[0], ((0, 0), (0, LANES - N_EXPERTS)))
    rb = jnp.pad(router_b, ((0, 0), (0, LANES - N_EXPERTS)))
    y1, h2, top_i, top_w = _outproj_router(
        x_all, jnp.concatenate([o_gdn_p, o_gdn_s]), jnp.concatenate([o_moba_p, o_moba_s]),
        jnp.concatenate([o_mem_p, o_mem_s]), w_out[0].astype(bf16), norm_ffn, rw, rb, tm=256)
    y = _moe(y1, h2, top_i, top_w, w_gu[0], b_gu[0], w_dn[0], b_dn[0])

    heads = lambda a, lead: a.reshape(lead + (MOBA_HEADS, HEAD_DIM))
    return (y[:n_p].reshape(1, n_p, d), y[n_p:].reshape(n_seq, n_st, d),
            heads(proj_b[:n_p, B_MK:B_MK + MOBA_W], (1, 1, n_p)), heads(proj_b[:n_p, B_MV:B_MV + MOBA_W], (1, 1, n_p)),
            p_gdn[None, None], p_conv[None, None],
            heads(mem_kv[:, :MEM_W], (1, 1, n_mem)), heads(mem_kv[:, MEM_W:], (1, 1, n_mem)),
            heads(proj_b3[..., B_MK:B_MK + MOBA_W], (1, n_seq, n_st)),
            heads(proj_b3[..., B_MV:B_MV + MOBA_W], (1, n_seq, n_st)),
            s_gdn[None], s_conv[None])
```

```python
import functools
import math

import numpy as np
import jax
import jax.numpy as jnp
from jax import lax
from jax.experimental import pallas as pl
from jax.experimental.pallas import tpu as pltpu

f32 = jnp.float32
bf16 = jnp.bfloat16
i32 = jnp.int32
HIGHEST = lax.Precision.HIGHEST

LANES = 128
SUBLANES = 8
VMEM_LIMIT = 56 * 1024 * 1024

D_MODEL = 2048
HEAD_DIM = 128
GDN_HEADS = 8
GDN_CONV = 4
GDN_CHUNK = 64
GDN_QK = GDN_HEADS * HEAD_DIM
GDN_V = GDN_HEADS * HEAD_DIM
GDN_CONV_DIM = 2 * GDN_QK + GDN_V
MOBA_HEADS = 4
MOBA_BLOCK = 256
MOBA_TOPK = 3
MOBA_W = MOBA_HEADS * HEAD_DIM
MEM_HEADS = 4
MEM_W = MEM_HEADS * HEAD_DIM
N_BUCKETS = 32
MAX_DISTANCE = 128
N_EXPERTS = 32
TOP_K = 4
D_FF = D_MODEL
SWIGLU_LIMIT = 7.0
SWIGLU_ALPHA = 1.702
EPS = 1e-6
NEG_INF = -1e30

A_W = GDN_CONV_DIM + GDN_V
B_MQ, B_MK, B_CQ, B_MV, B_BA = 0, MOBA_W, 2 * MOBA_W, 3 * MOBA_W, 4 * MOBA_W
B_W = B_BA + LANES
B_NORM_GROUPS = 3 * MOBA_HEADS


def _params(*sem):
    return pltpu.CompilerParams(dimension_semantics=sem, vmem_limit_bytes=VMEM_LIMIT)


def _dot(a, b, **kw):
    return jnp.dot(a, b, preferred_element_type=f32, **kw)


def _dot_nt(a, b, **kw):
    return lax.dot_general(a, b, (((1,), (1,)), ((), ())), preferred_element_type=f32, **kw)


def _dot_tn(a, b, **kw):
    return lax.dot_general(a, b, (((0,), (0,)), ((), ())), preferred_element_type=f32, **kw)


def _rms(x, gain):
    return x * lax.rsqrt(jnp.mean(x * x, axis=-1, keepdims=True) + EPS) * gain


def _silu(x):
    return x * jax.nn.sigmoid(x)


def _norm_matmul_kernel(x_ref, g_ref, w_ref, gain_ref, o_ref, h_ref, *, n_norm):
    @pl.when(pl.program_id(1) == 0)
    def _():
        h_ref[...] = _rms(x_ref[...], g_ref[...]).astype(bf16)

    acc = _dot(h_ref[...], w_ref[...])
    if n_norm == 0:
        o_ref[...] = acc
    else:
        for gi in range(acc.shape[1] // LANES):
            sl = slice(gi * LANES, (gi + 1) * LANES)
            blk = acc[:, sl]
            if gi < n_norm:
                blk = _rms(blk, gain_ref[:, sl])
            o_ref[:, sl] = blk


def _norm_matmul(x, g, w, gain, *, tm, tn, n_norm):
    n, d = x.shape
    wn = w.shape[1]
    return pl.pallas_call(
        functools.partial(_norm_matmul_kernel, n_norm=n_norm),
        grid=(n // tm, wn // tn),
        in_specs=[
            pl.BlockSpec((tm, d), lambda i, j: (i, 0)),
            pl.BlockSpec((1, d), lambda i, j: (0, 0)),
            pl.BlockSpec((d, tn), lambda i, j: (0, j)),
            pl.BlockSpec((1, gain.shape[1]), lambda i, j: (0, 0)),
        ],
        out_specs=pl.BlockSpec((tm, tn), lambda i, j: (i, j)),
        out_shape=jax.ShapeDtypeStruct((n, wn), f32),
        scratch_shapes=[pltpu.VMEM((tm, d), bf16)],
        compiler_params=_params("parallel", "arbitrary"),
        name="norm_matmul",
    )(x, g, w, gain)


INV_BLOCK = 16


def _split_bf16(x):
    hi = x.astype(bf16)
    return hi, (x - hi.astype(f32)).astype(bf16)


def _dot3(a, b):
    a_hi, a_lo = a
    b_hi, b_lo = b
    return _dot(a_hi, b_hi) + (_dot(a_hi, b_lo) + _dot(a_lo, b_hi))


def _unit_lower_inverses(lmats):
    c = lmats[0].shape[0]
    row = lax.broadcasted_iota(i32, (c, c), 0)
    col = lax.broadcasted_iota(i32, (c, c), 1)
    same = lambda n: (row // n) == (col // n)
    eye = jnp.where(row == col, 1.0, 0.0).astype(f32)
    ms = [jnp.where(same(INV_BLOCK), l, 0.0) for l in lmats]
    ps = [eye - m for m in ms]
    power = 1
    while 2 * power < INV_BLOCK:
        sm = [_split_bf16(m) for m in ms]
        ms = [_dot3(s, s) for s in sm]
        sm = [_split_bf16(m) for m in ms]
        ps = [p + _dot3(_split_bf16(p), s) for p, s in zip(ps, sm)]
        power *= 2
    n = 2 * INV_BLOCK
    while n <= c:
        offs = [_split_bf16(jnp.where(same(n) & ~same(n // 2), l, 0.0)) for l in lmats]
        sp = [_split_bf16(p) for p in ps]
        mids = [_dot3(o, s) for o, s in zip(offs, sp)]
        ps = [p - _dot3(s, _split_bf16(m)) for p, s, m in zip(ps, sp, mids)]
        n *= 2
    return ps


def _gdn_prompt_kernel(qkv_ref, z_ref, ba_ref, convw_ref, alog_ref, dtb_ref, onorm_ref,
                       o_ref, s_out_ref, xp_ref, s_ref):
    c = GDN_CHUNK
    step = pl.program_id(0)

    @pl.when(step == 0)
    def _():
        xp_ref[0:SUBLANES, :] = jnp.zeros((SUBLANES, GDN_CONV_DIM), f32)
        s_ref[...] = jnp.zeros_like(s_ref)

    xp_ref[SUBLANES:SUBLANES + c, :] = qkv_ref[...]
    w = convw_ref[...]
    y = None
    for j in range(GDN_CONV):
        lo = SUBLANES - (GDN_CONV - 1) + j
        term = xp_ref[lo:lo + c, :] * w[j:j + 1, :]
        y = term if y is None else y + term
    y = _silu(y)
    xp_ref[0:SUBLANES, :] = xp_ref[c:c + SUBLANES, :]

    ba = ba_ref[...]
    beta_all = jax.nn.sigmoid(ba)
    g_all = -jnp.exp(alog_ref[...]) * jax.nn.softplus(ba + dtb_ref[...])
    row = lax.broadcasted_iota(i32, (c, c), 0)
    col = lax.broadcasted_iota(i32, (c, c), 1)
    incl = row >= col
    strict = row > col
    gcum_all = _dot(jnp.where(incl, 1.0, 0.0).astype(f32), g_all, precision=HIGHEST)
    gcum_t = gcum_all.T
    onorm = onorm_ref[...]

    heads = range(GDN_HEADS)
    head_cols = lambda off, h: slice(off + h * HEAD_DIM, off + (h + 1) * HEAD_DIM)
    qs = [y[:, head_cols(0, h)] for h in heads]
    ks = [y[:, head_cols(GDN_QK, h)] for h in heads]
    vs = [y[:, head_cols(2 * GDN_QK, h)] for h in heads]
    qs = [q * lax.rsqrt(jnp.sum(q * q, axis=-1, keepdims=True) + EPS) * (HEAD_DIM ** -0.5) for q in qs]
    ks = [k * lax.rsqrt(jnp.sum(k * k, axis=-1, keepdims=True) + EPS) for k in ks]
    betas = [beta_all[:, h:h + 1] for h in heads]
    gcs = [gcum_all[:, GDN_HEADS + h:GDN_HEADS + h + 1] for h in heads]
    grs = [gcum_t[GDN_HEADS + h:GDN_HEADS + h + 1, :] for h in heads]
    g_lasts = [gcum_all[c - 1:c, GDN_HEADS + h:GDN_HEADS + h + 1] for h in heads]
    decays = [jnp.exp(jnp.where(incl, gc - gr, -jnp.inf)) for gc, gr in zip(gcs, grs)]
    kbetas = [k * b for k, b in zip(ks, betas)]
    lmats = [_dot_nt(kb, k) * jnp.where(strict, dc, 0.0) for kb, k, dc in zip(kbetas, ks, decays)]
    tinvs = _unit_lower_inverses(lmats)
    egcs = [jnp.exp(gc) for gc in gcs]
    us = [_dot(t, v * b) for t, v, b in zip(tinvs, vs, betas)]
    ws = [_dot(t, kb * e) for t, kb, e in zip(tinvs, kbetas, egcs)]
    intras = [_dot_nt(q, k) * dc for q, k, dc in zip(qs, ks, decays)]
    kdecs = [k * jnp.exp(gl - gc) for k, gl, gc in zip(ks, g_lasts, gcs)]
    ss = [s_ref[h] for h in heads]
    v_news = [u - _dot(w_, s) for u, w_, s in zip(us, ws, ss)]
    os_ = [_dot(q * e, s) + _dot(it, vn) for q, e, s, it, vn in zip(qs, egcs, ss, intras, v_news)]
    for h in heads:
        s_ref[h] = ss[h] * jnp.exp(g_lasts[h]) + _dot_tn(kdecs[h], v_news[h])
    for h in heads:
        hs = head_cols(0, h)
        o_ref[:, hs] = (_rms(os_[h], onorm) * _silu(z_ref[:, hs])).astype(o_ref.dtype)

    @pl.when(step == pl.num_programs(0) - 1)
    def _():
        s_out_ref[...] = s_ref[...]


def _gdn_prompt(proj_a, proj_b, conv_w, alog_row, dtb_row, onorm, *, n_tok):
    c = GDN_CHUNK
    zblk = GDN_CONV_DIM // GDN_V
    return pl.pallas_call(
        _gdn_prompt_kernel,
        grid=(n_tok // c,),
        in_specs=[
            pl.BlockSpec((c, GDN_CONV_DIM), lambda i: (i, 0)),
            pl.BlockSpec((c, GDN_V), lambda i: (i, zblk)),
            pl.BlockSpec((c, LANES), lambda i: (i, B_BA // LANES)),
            pl.BlockSpec((GDN_CONV, GDN_CONV_DIM), lambda i: (0, 0)),
            pl.BlockSpec((1, LANES), lambda i: (0, 0)),
            pl.BlockSpec((1, LANES), lambda i: (0, 0)),
            pl.BlockSpec((1, HEAD_DIM), lambda i: (0, 0)),
        ],
        out_specs=[
            pl.BlockSpec((c, GDN_V), lambda i: (i, 0)),
            pl.BlockSpec((GDN_HEADS, HEAD_DIM, HEAD_DIM), lambda i: (0, 0, 0)),
        ],
        out_shape=[
            jax.ShapeDtypeStruct((n_tok, GDN_V), bf16),
            jax.ShapeDtypeStruct((GDN_HEADS, HEAD_DIM, HEAD_DIM), f32),
        ],
        scratch_shapes=[
            pltpu.VMEM((c + SUBLANES, GDN_CONV_DIM), f32),
            pltpu.VMEM((GDN_HEADS, HEAD_DIM, HEAD_DIM), f32),
        ],
        compiler_params=_params("arbitrary"),
        name="gdn_prompt",
    )(proj_a, proj_a, proj_b, conv_w, alog_row, dtb_row, onorm)


GDN_S_SEQ = SUBLANES


def _gdn_sample_kernel(q_ref, k_ref, v_ref, z_ref, ba_ref, bq_ref, bk_ref, bv_ref, wq_ref, wk_ref, wv_ref,
                       alog_ref, dtb_ref, onorm_ref, s_ref, o_ref, so_ref, *, n_tok):
    h = pl.program_id(1)
    nb = GDN_S_SEQ
    lane = lax.broadcasted_iota(i32, (nb, LANES), 1)

    def conv(x_ref, buf_ref, w_ref, t):
        y = None
        for j in range(GDN_CONV):
            i = t + j
            row = buf_ref[i] if i < GDN_CONV - 1 else x_ref[i - (GDN_CONV - 1)]
            term = row * w_ref[j:j + 1, :]
            y = term if y is None else y + term
        return _silu(y)

    def lane_col(x, idx):
        return jnp.sum(jnp.where(lane == idx, x, 0.0), axis=1, keepdims=True)

    qs, ks, vs, betas, egs = [], [], [], [], []
    for t in range(n_tok):
        q = conv(q_ref, bq_ref, wq_ref, t)
        k = conv(k_ref, bk_ref, wk_ref, t)
        q = q * lax.rsqrt(jnp.sum(q * q, axis=-1, keepdims=True) + EPS) * (HEAD_DIM ** -0.5)
        k = k * lax.rsqrt(jnp.sum(k * k, axis=-1, keepdims=True) + EPS)
        qs.append(q.T)
        ks.append(k.T)
        vs.append(conv(v_ref, bv_ref, wv_ref, t))
        ba = ba_ref[t]
        betas.append(lane_col(jax.nn.sigmoid(ba), h))
        g = -jnp.exp(alog_ref[...]) * jax.nn.softplus(ba + dtb_ref[...])
        egs.append(jnp.exp(lane_col(g, GDN_HEADS + h)))

    o_rows = [[None] * nb for _ in range(n_tok)]
    for b in range(nb):
        s = s_ref[b]
        for t in range(n_tok):
            kc = ks[t][:, b:b + 1]
            qc = qs[t][:, b:b + 1]
            s = s * egs[t][b:b + 1, :]
            ks_row = jnp.sum(kc * s, axis=0, keepdims=True)
            delta = (vs[t][b:b + 1, :] - ks_row) * betas[t][b:b + 1, :]
            s = s + kc * delta
            o_rows[t][b] = jnp.sum(qc * s, axis=0, keepdims=True)
        so_ref[b] = s
    for t in range(n_tok):
        o = jnp.concatenate(o_rows[t], axis=0)
        o_ref[t] = _rms(o, onorm_ref[...]) * _silu(z_ref[t])


def _gdn_sample(xa, ba, conv_buf, conv_w, alog_row, dtb_row, onorm, state):
    n_tok, n_seq, _ = xa.shape
    nb = GDN_S_SEQ
    hq, hk, hv, hz = 0, GDN_HEADS, 2 * GDN_HEADS, 3 * GDN_HEADS

    def xspec(off, rows):
        return pl.BlockSpec((rows, nb, HEAD_DIM), lambda i, h: (0, i, off + h))

    def wspec(off):
        return pl.BlockSpec((GDN_CONV, HEAD_DIM), lambda i, h: (0, off + h))

    row = pl.BlockSpec((1, LANES), lambda i, h: (0, 0))
    sspec = pl.BlockSpec((nb, None, HEAD_DIM, HEAD_DIM), lambda i, h: (i, h, 0, 0))
    return pl.pallas_call(
        functools.partial(_gdn_sample_kernel, n_tok=n_tok),
        grid=(n_seq // nb, GDN_HEADS),
        in_specs=[xspec(hq, n_tok), xspec(hk, n_tok), xspec(hv, n_tok), xspec(hz, n_tok),
                  pl.BlockSpec((n_tok, nb, LANES), lambda i, h: (0, i, 0)),
                  xspec(hq, GDN_CONV - 1), xspec(hk, GDN_CONV - 1), xspec(hv, GDN_CONV - 1),
                  wspec(hq), wspec(hk), wspec(hv), row, row, row, sspec],
        out_specs=[pl.BlockSpec((n_tok, nb, HEAD_DIM), lambda i, h: (0, i, h)), sspec],
        out_shape=[jax.ShapeDtypeStruct((n_tok, n_seq, GDN_V), f32),
                   jax.ShapeDtypeStruct(state.shape, f32)],
        compiler_params=_params("parallel", "arbitrary"),
        name="gdn_sample",
    )(xa, xa, xa, xa, ba, conv_buf, conv_buf, conv_buf, conv_w, conv_w, conv_w, alog_row, dtb_row, onorm, state)


def _bucket_thresholds():
    exact = N_BUCKETS // 2
    d = np.arange(0, 2 * MAX_DISTANCE)
    val = np.log(np.maximum(d, 1).astype(np.float64) / exact) / math.log(MAX_DISTANCE / exact) * (N_BUCKETS - exact)
    frac = np.abs(val - np.round(val))[exact + 1:MAX_DISTANCE]
    assert frac.min() > 1e-3, "a bucket boundary sits on an integer distance"
    bucket = np.where(d < exact, d, np.minimum(exact + val.astype(np.int64), N_BUCKETS - 1))
    assert np.all(np.diff(bucket) >= 0)
    return [int(np.argmax(bucket >= b)) for b in range(N_BUCKETS)]


_BUCKET_THR = _bucket_thresholds()


def _bias_from_dist(dist, rb_ref, h):
    v = jnp.full(dist.shape, rb_ref[0, h], f32)
    for b in range(1, N_BUCKETS):
        v = jnp.where(dist >= _BUCKET_THR[b], rb_ref[b, h], v)
    return v


def _topk_mask(s, blk, k):
    nblk = s.shape[1]
    sel = jnp.zeros(s.shape, f32)
    for _ in range(k):
        m = jnp.max(s, axis=1, keepdims=True)
        cand = jnp.where((s == m) & (m > -jnp.inf), blk, nblk)
        pick = blk == jnp.min(cand, axis=1, keepdims=True)
        sel = jnp.where(pick, 1.0, sel)
        s = jnp.where(pick, -jnp.inf, s)
    return sel


def _block_mean_kernel(k_ref, o_ref):
    o_ref[0] = jnp.mean(k_ref[...], axis=0, keepdims=True)


def _block_mean(proj_b, n_blk):
    return pl.pallas_call(
        _block_mean_kernel,
        grid=(n_blk,),
        in_specs=[pl.BlockSpec((MOBA_BLOCK, MOBA_W), lambda i: (i, B_MK // MOBA_W))],
        out_specs=pl.BlockSpec((1, 1, MOBA_W), lambda i: (i, 0, 0)),
        out_shape=jax.ShapeDtypeStruct((n_blk, 1, MOBA_W), f32),
        compiler_params=_params("parallel"),
        name="moba_block_mean",
    )(proj_b)


def _moba_select_kernel(q_ref, km_ref, o_ref):
    own = pl.program_id(0)
    nblk = km_ref.shape[0]
    blk = lax.broadcasted_iota(i32, (MOBA_BLOCK, nblk), 1)
    outs = []
    for h in range(MOBA_HEADS):
        hs = slice(h * HEAD_DIM, (h + 1) * HEAD_DIM)
        s = _dot_nt(q_ref[:, hs], km_ref[:, hs], precision=HIGHEST)
        s = jnp.where(blk < own, s, -jnp.inf)
        outs.append(_topk_mask(s, blk, MOBA_TOPK))
    pad = LANES - MOBA_HEADS * nblk
    if pad:
        outs.append(jnp.zeros((MOBA_BLOCK, pad), f32))
    o_ref[...] = jnp.concatenate(outs, axis=1)


def _moba_select(proj_b, kmean, n_blk):
    assert n_blk * MOBA_HEADS <= LANES
    return pl.pallas_call(
        _moba_select_kernel,
        grid=(n_blk,),
        in_specs=[pl.BlockSpec((MOBA_BLOCK, MOBA_W), lambda i: (i, B_MQ // MOBA_W)),
                  pl.BlockSpec((n_blk, MOBA_W), lambda i: (0, 0))],
        out_specs=pl.BlockSpec((MOBA_BLOCK, LANES), lambda i: (i, 0)),
        out_shape=jax.ShapeDtypeStruct((n_blk * MOBA_BLOCK, LANES), f32),
        compiler_params=_params("parallel"),
        name="moba_select",
    )(proj_b, kmean)


def _moba_prompt_kernel(qi_ref, kj_ref, rb_ref, q_ref, k_ref, v_ref, sel_ref, o_ref,
                        bias_ref, m_ref, l_ref, acc_ref, *, n_blk):
    step = pl.program_id(0)
    qi = qi_ref[step]
    kj = kj_ref[step]
    nq = MOBA_BLOCK

    @pl.when(step == 0)
    def _():
        r = lax.broadcasted_iota(i32, (nq, nq), 0)
        c = lax.broadcasted_iota(i32, (nq, nq), 1)
        for h in range(MOBA_HEADS):
            bias_ref[h, 0] = jnp.where(c <= r, _bias_from_dist(r - c, rb_ref, h), NEG_INF)
            bias_ref[h, 1] = _bias_from_dist(r - c + nq, rb_ref, h)
            bias_ref[h, 2] = jnp.full((nq, nq), rb_ref[N_BUCKETS - 1, h], f32)

    first = kj == qi

    @pl.when(first)
    def _():
        m_ref[...] = jnp.full(m_ref.shape, NEG_INF, f32)
        l_ref[...] = jnp.zeros_like(l_ref)
        acc_ref[...] = jnp.zeros_like(acc_ref)

    slot = jnp.where(first, 0, jnp.where(kj == qi - 1, 1, 2))
    sel = sel_ref[...]
    lane = lax.broadcasted_iota(i32, sel.shape, 1)
    heads = range(MOBA_HEADS)
    hcols = [slice(h * HEAD_DIM, (h + 1) * HEAD_DIM) for h in heads]
    ss = [_dot_nt(q_ref[:, hs].astype(bf16), k_ref[:, hs].astype(bf16)) for hs in hcols]
    picked = [jnp.max(jnp.where(lane == h * n_blk + kj, sel, 0.0), axis=1, keepdims=True) for h in heads]
    ss = [s * (HEAD_DIM ** -0.5) + bias_ref[h, slot] for h, s in zip(heads, ss)]
    ss = [jnp.where((pk > 0.0) | first, s, NEG_INF) for pk, s in zip(picked, ss)]
    m_prevs = [m_ref[h] for h in heads]
    m_news = [jnp.maximum(mp, jnp.max(s, axis=1, keepdims=True)) for mp, s in zip(m_prevs, ss)]
    alphas = [jnp.exp(mp - mn) for mp, mn in zip(m_prevs, m_news)]
    ps = [jnp.exp(s - mn) for s, mn in zip(ss, m_news)]
    pvs = [_dot(p.astype(bf16), v_ref[:, hs].astype(bf16)) for p, hs in zip(ps, hcols)]
    for h in heads:
        l_ref[h] = alphas[h] * l_ref[h] + jnp.sum(ps[h], axis=1, keepdims=True)
        acc_ref[h] = alphas[h] * acc_ref[h] + pvs[h]
        m_ref[h] = m_news[h]

    @pl.when(kj == 0)
    def _():
        for h in range(MOBA_HEADS):
            hs = slice(h * HEAD_DIM, (h + 1) * HEAD_DIM)
            o_ref[:, hs] = (acc_ref[h] / l_ref[h]).astype(o_ref.dtype)


def _moba_prompt(proj_b, sel, rel_bias, n_blk):
    qi = np.concatenate([np.full(i + 1, i) for i in range(n_blk)]).astype(np.int32)
    kj = np.concatenate([np.arange(i, -1, -1) for i in range(n_blk)]).astype(np.int32)
    nq = MOBA_BLOCK
    grid_spec = pltpu.PrefetchScalarGridSpec(
        num_scalar_prefetch=2,
        grid=(len(qi),),
        in_specs=[
            pl.BlockSpec(memory_space=pltpu.SMEM),
            pl.BlockSpec((nq, MOBA_W), lambda s, qi, kj: (qi[s], B_MQ // MOBA_W)),
            pl.BlockSpec((nq, MOBA_W), lambda s, qi, kj: (kj[s], B_MK // MOBA_W)),
            pl.BlockSpec((nq, MOBA_W), lambda s, qi, kj: (kj[s], B_MV // MOBA_W)),
            pl.BlockSpec((nq, LANES), lambda s, qi, kj: (qi[s], 0)),
        ],
        out_specs=pl.BlockSpec((nq, MOBA_W), lambda s, qi, kj: (qi[s], 0)),
        scratch_shapes=[
            pltpu.VMEM((MOBA_HEADS, 3, nq, nq), f32),
            pltpu.VMEM((MOBA_HEADS, nq, 1), f32),
            pltpu.VMEM((MOBA_HEADS, nq, 1), f32),
            pltpu.VMEM((MOBA_HEADS, nq, HEAD_DIM), f32),
        ],
    )
    return pl.pallas_call(
        functools.partial(_moba_prompt_kernel, n_blk=n_blk),
        grid_spec=grid_spec,
        out_shape=jax.ShapeDtypeStruct((n_blk * nq, MOBA_W), bf16),
        compiler_params=_params("arbitrary"),
        name="moba_prompt",
    )(jnp.asarray(qi), jnp.asarray(kj), rel_bias, proj_b, proj_b, proj_b, sel)


def _per_head_bias(dist, row_head, rb_ref):
    v = _bias_from_dist(dist, rb_ref, 0)
    for h in range(1, MOBA_HEADS):
        v = jnp.where(row_head == h, _bias_from_dist(dist, rb_ref, h), v)
    return v


def _moba_sample_kernel(pt_ref, rb_ref, q_ref, kn_ref, vn_ref, *rest, n_pages, page):
    kp = rest[:n_pages]
    vp = rest[n_pages:2 * n_pages]
    o_ref, bias_ref, biasn_ref = rest[2 * n_pages:]
    nh = MOBA_HEADS
    rows = q_ref.shape[0]
    prow = page * nh
    past = n_pages * page
    n_blk = past // MOBA_BLOCK
    ppb = MOBA_BLOCK // page

    @pl.when(pl.program_id(0) == 0)
    def _():
        r = lax.broadcasted_iota(i32, (rows, past * nh), 0)
        c = lax.broadcasted_iota(i32, (rows, past * nh), 1)
        bias = _per_head_bias(past + r // nh - c // nh, r % nh, rb_ref)
        bias_ref[...] = jnp.where(r % nh == c % nh, bias, NEG_INF)
        r = lax.broadcasted_iota(i32, (rows, rows), 0)
        c = lax.broadcasted_iota(i32, (rows, rows), 1)
        bias = _per_head_bias(r // nh - c // nh, r % nh, rb_ref)
        biasn_ref[...] = jnp.where((r % nh == c % nh) & (c // nh <= r // nh), bias, NEG_INF)

    kb, means = [], []
    for p in range(n_pages):
        kpage = kp[p][...]
        kb.append(kpage.astype(bf16))
        part = jnp.sum(kpage.reshape(prow // SUBLANES, SUBLANES, HEAD_DIM), axis=0)
        part = part[0:nh] + part[nh:2 * nh]
        if p % ppb == 0:
            means.append(part)
        else:
            means[-1] = means[-1] + part
    kmean = jnp.concatenate(means, axis=0) / MOBA_BLOCK

    q = q_ref[...]
    r = lax.broadcasted_iota(i32, (rows, n_blk * nh), 0)
    c = lax.broadcasted_iota(i32, (rows, n_blk * nh), 1)
    s = jnp.where(r % nh == c % nh, _dot_nt(q, kmean, precision=HIGHEST), -jnp.inf)
    sel = _topk_mask(s, c, MOBA_TOPK)
    picked = [jnp.max(jnp.where(c // nh == n, sel, 0.0), axis=1, keepdims=True) > 0.0 for n in range(n_blk)]

    qb = q.astype(bf16)
    scale = HEAD_DIM ** -0.5
    ln = _dot_nt(qb, kn_ref[...].astype(bf16)) * scale + biasn_ref[...]
    m = jnp.max(ln, axis=1, keepdims=True)
    lps = []
    for p in range(n_pages):
        lp = _dot_nt(qb, kb[p]) * scale + bias_ref[:, p * prow:(p + 1) * prow]
        lp = jnp.where(picked[p // ppb], lp, NEG_INF)
        m = jnp.maximum(m, jnp.max(lp, axis=1, keepdims=True))
        lps.append(lp)
    pn = jnp.exp(ln - m)
    den = jnp.sum(pn, axis=1, keepdims=True)
    num = _dot(pn.astype(bf16), vn_ref[...].astype(bf16))
    for p in range(n_pages):
        pp = jnp.exp(lps[p] - m)
        den = den + jnp.sum(pp, axis=1, keepdims=True)
        num = num + _dot(pp.astype(bf16), vp[p][...].astype(bf16))
    o_ref[...] = num / den


def _moba_sample(q, k_new, v_new, pool_k, pool_v, page_table, rel_bias, *, page):
    n_seq, rows, _ = q.shape
    n_pages = page_table.shape[1]
    prow = page * MOBA_HEADS
    new_spec = pl.BlockSpec((None, rows, HEAD_DIM), lambda b, pt: (b, 0, 0))

    def page_spec(p):
        return pl.BlockSpec((prow, HEAD_DIM), lambda b, pt: (pt[b, p], 0))

    grid_spec = pltpu.PrefetchScalarGridSpec(
        num_scalar_prefetch=1,
        grid=(n_seq,),
        in_specs=[pl.BlockSpec(memory_space=pltpu.SMEM), new_spec, new_spec, new_spec]
        + [page_spec(p) for p in range(n_pages)] * 2,
        out_specs=new_spec,
        scratch_shapes=[pltpu.VMEM((rows, n_pages * prow), f32), pltpu.VMEM((rows, rows), f32)],
    )
    return pl.pallas_call(
        functools.partial(_moba_sample_kernel, n_pages=n_pages, page=page),
        grid_spec=grid_spec,
        out_shape=jax.ShapeDtypeStruct((n_seq, rows, HEAD_DIM), f32),
        compiler_params=_params("arbitrary"),
        name="moba_sample",
    )(page_table, rel_bias, q, k_new, v_new, *([pool_k] * n_pages), *([pool_v] * n_pages))


def _mem_attn_kernel(q_ref, k_ref, v_ref, o_ref):
    for h in range(MEM_HEADS):
        hs = slice(h * HEAD_DIM, (h + 1) * HEAD_DIM)
        s = _dot_nt(q_ref[:, hs].astype(bf16), k_ref[:, hs].astype(bf16)) * (HEAD_DIM ** -0.5)
        p = jnp.exp(s - jnp.max(s, axis=1, keepdims=True))
        num = _dot(p.astype(bf16), v_ref[:, hs].astype(bf16))
        o_ref[:, hs] = (num / jnp.sum(p, axis=1, keepdims=True)).astype(o_ref.dtype)


def _mem_attn_prompt(proj_b, mem_kv, *, n_tok, tq):
    n_mem = mem_kv.shape[0]
    return pl.pallas_call(
        _mem_attn_kernel,
        grid=(n_tok // tq,),
        in_specs=[pl.BlockSpec((tq, MEM_W), lambda i: (i, B_CQ // MEM_W)),
                  pl.BlockSpec((n_mem, MEM_W), lambda i: (0, 0)),
                  pl.BlockSpec((n_mem, MEM_W), lambda i: (0, 1))],
        out_specs=pl.BlockSpec((tq, MEM_W), lambda i: (i, 0)),
        out_shape=jax.ShapeDtypeStruct((n_tok, MEM_W), bf16),
        compiler_params=_params("parallel"),
        name="mem_attn_prompt",
    )(proj_b, mem_kv, mem_kv)


def _mem_attn_sample_kernel(q_ref, k_ref, v_ref, o_ref):
    s = _dot_nt(q_ref[...].astype(bf16), k_ref[...].astype(bf16)) * (HEAD_DIM ** -0.5)
    r = lax.broadcasted_iota(i32, s.shape, 0)
    c = lax.broadcasted_iota(i32, s.shape, 1)
    s = jnp.where(r % MEM_HEADS == c % MEM_HEADS, s, NEG_INF)
    p = jnp.exp(s - jnp.max(s, axis=1, keepdims=True))
    o_ref[...] = _dot(p.astype(bf16), v_ref[...].astype(bf16)) / jnp.sum(p, axis=1, keepdims=True)


def _mem_attn_sample(q, mem_k, mem_v):
    n_seq, rows, _ = q.shape
    mrows = mem_k.shape[1]
    return pl.pallas_call(
        _mem_attn_sample_kernel,
        grid=(n_seq,),
        in_specs=[pl.BlockSpec((None, rows, HEAD_DIM), lambda b: (b, 0, 0)),
                  pl.BlockSpec((None, mrows, HEAD_DIM), lambda b: (b, 0, 0)),
                  pl.BlockSpec((None, mrows, HEAD_DIM), lambda b: (b, 0, 0))],
        out_specs=pl.BlockSpec((None, rows, HEAD_DIM), lambda b: (b, 0, 0)),
        out_shape=jax.ShapeDtypeStruct((n_seq, rows, HEAD_DIM), f32),
        compiler_params=_params("parallel"),
        name="mem_attn_sample",
    )(q, mem_k, mem_v)


def _outproj_router_kernel(x_ref, og_ref, om_ref, oc_ref, w_ref, g_ref, rw_ref, rb_ref,
                           y_ref, h_ref, ti_ref, tw_ref):
    mix = (_dot(og_ref[...], w_ref[0:GDN_V, :])
           + _dot(om_ref[...], w_ref[GDN_V:GDN_V + MOBA_W, :])
           + _dot(oc_ref[...], w_ref[GDN_V + MOBA_W:, :]))
    y = x_ref[...] + mix
    y_ref[...] = y
    h = _rms(y, g_ref[...])
    _store_slabs(h_ref, h)
    h_hi, h_lo = _split_bf16(h)
    w_hi, w_lo = _split_bf16(rw_ref[...])
    logits = _dot(h_hi, w_hi) + (_dot(h_hi, w_lo) + _dot(h_lo, w_hi)) + rb_ref[...]
    lane = lax.broadcasted_iota(i32, logits.shape, 1)
    s = jnp.where(lane < N_EXPERTS, logits, -jnp.inf)
    vals, idxs = [], []
    for _ in range(TOP_K):
        m = jnp.max(s, axis=1, keepdims=True)
        idx = jnp.min(jnp.where(s == m, lane, LANES), axis=1, keepdims=True)
        vals.append(m)
        idxs.append(idx)
        s = jnp.where(lane == idx, -jnp.inf, s)
    exps = [jnp.exp(v - vals[0]) for v in vals]
    den = exps[0]
    for e in exps[1:]:
        den = den + e
    ti = jnp.zeros(logits.shape, i32)
    tw = jnp.zeros(logits.shape, f32)
    for r in range(TOP_K):
        ti = jnp.where(lane == r, idxs[r], ti)
        tw = jnp.where(lane == r, exps[r] / den, tw)
    ti_ref[...] = ti
    tw_ref[...] = tw


def _outproj_router(x, og, om, oc, w_out, g, rw, rb, *, tm):
    n, d = x.shape
    row = lambda w: pl.BlockSpec((tm, w), lambda i: (i, 0))
    full = lambda a: pl.BlockSpec(a.shape, lambda i: (0,) * a.ndim)
    return pl.pallas_call(
        _outproj_router_kernel,
        grid=(n // tm,),
        in_specs=[row(d), row(GDN_V), row(MOBA_W), row(MEM_W), full(w_out), full(g), full(rw), full(rb)],
        out_specs=[row(d), pl.BlockSpec((tm * (d // LANES), LANES), lambda i: (i, 0)), row(LANES), row(LANES)],
        out_shape=[jax.ShapeDtypeStruct((n, d), f32), jax.ShapeDtypeStruct((n * (d // LANES), LANES), f32),
                   jax.ShapeDtypeStruct((n, LANES), i32), jax.ShapeDtypeStruct((n, LANES), f32)],
        compiler_params=_params("parallel"),
        name="outproj_router",
    )(x, og, om, oc, w_out, g, rw, rb)


GATHER_UNROLL = 8
SLAB_G = D_MODEL // LANES


def _store_slabs(slab_ref, x, mask=None):
    rows = x.shape[0]
    g = x.shape[1] // LANES
    for c in range(g):
        idx = (pl.ds(c, rows, stride=g), slice(None))
        blk = x[:, c * LANES:(c + 1) * LANES]
        slab_ref[idx] = blk if mask is None else jnp.where(mask, blk, slab_ref[idx])


def _load_slab_group(slab_ref, lead, c, rows, g):
    return slab_ref[lead + (pl.ds(c, rows, stride=g), slice(None))]


def _issue_slab_gather(idx_ref, idx_base, src_ref, dst_ref, sem, count, g):
    assert count % GATHER_UNROLL == 0

    def body(jj, carry):
        for u in range(GATHER_UNROLL):
            j = jj * GATHER_UNROLL + u
            tok = pl.multiple_of(idx_ref[idx_base + j] * g, g)
            dst = dst_ref.at[pl.ds(pl.multiple_of(j * g, g), g)]
            pltpu.make_async_copy(src_ref.at[pl.ds(tok, g)], dst, sem).start(priority=u % 2)
        return carry

    lax.fori_loop(0, count // GATHER_UNROLL, body, 0)


def _wait_slab_gather(src_ref, dst_ref, sem):
    pltpu.make_async_copy(src_ref.at[pl.ds(0, dst_ref.shape[0])], dst_ref, sem).wait()


def _gather_x_kernel(idx_ref, src_ref, o_ref, buf_ref, sem, *, rows, g):
    i = pl.program_id(0)
    slot = i % 2

    @pl.when(i == 0)
    def _():
        _issue_slab_gather(idx_ref, 0, src_ref, buf_ref.at[0], sem.at[0], rows, g)

    @pl.when(i + 1 < pl.num_programs(0))
    def _():
        _issue_slab_gather(idx_ref, (i + 1) * rows, src_ref, buf_ref.at[1 - slot], sem.at[1 - slot], rows, g)

    _wait_slab_gather(src_ref, buf_ref.at[slot], sem.at[slot])
    for c in range(g):
        o_ref[:, c * LANES:(c + 1) * LANES] = _load_slab_group(buf_ref, (slot,), c, rows, g).astype(o_ref.dtype)


def _gather_x(src, idx, *, rows, g):
    n_out = idx.shape[0]
    grid_spec = pltpu.PrefetchScalarGridSpec(
        num_scalar_prefetch=1,
        grid=(n_out // rows,),
        in_specs=[pl.BlockSpec(memory_space=pl.ANY)],
        out_specs=pl.BlockSpec((rows, g * LANES), lambda i, idx: (i, 0)),
        scratch_shapes=[pltpu.VMEM((2, rows * g, LANES), f32), pltpu.SemaphoreType.DMA((2,))],
    )
    return pl.pallas_call(
        functools.partial(_gather_x_kernel, rows=rows, g=g),
        grid_spec=grid_spec,
        out_shape=jax.ShapeDtypeStruct((n_out, g * LANES), bf16),
        compiler_params=_params("arbitrary"),
        name="moe_gather_x",
    )(idx, src)


def _visit_store(o_ref, new, lo, hi):
    @pl.when(lo == 0)
    def _():
        o_ref[...] = new

    @pl.when(lo > 0)
    def _():
        r = lax.broadcasted_iota(i32, new.shape, 0)
        o_ref[...] = jnp.where((r >= lo) & (r < hi), new, o_ref[...])


def _new_expert(ev_ref, v):
    return (v == 0) | (ev_ref[v] != ev_ref[jnp.maximum(v - 1, 0)])


def _moe_gate_up_kernel(tv_ref, ev_ref, lo_ref, hi_ref, x_ref, wg_ref, wu_ref, bg_ref, bu_ref, o_ref, wb_ref):
    v = pl.program_id(1)
    lo = lo_ref[v]
    hi = hi_ref[v]

    @pl.when(_new_expert(ev_ref, v))
    def _():
        wb_ref[0] = wg_ref[...].astype(bf16)
        wb_ref[1] = wu_ref[...].astype(bf16)

    @pl.when(hi > lo)
    def _():
        x = x_ref[...]
        gate = _dot(x, wb_ref[0]) + bg_ref[...]
        up = _dot(x, wb_ref[1]) + bu_ref[...]
        gate = jnp.minimum(gate, SWIGLU_LIMIT)
        up = jnp.clip(up, -SWIGLU_LIMIT, SWIGLU_LIMIT)
        act = gate * jax.nn.sigmoid(SWIGLU_ALPHA * gate) * (up + 1.0)
        _visit_store(o_ref, act.astype(o_ref.dtype), lo, hi)


def _moe_gate_up(xs, w_gu, b_gu3, tv, ev, lo, hi, *, tm, tf):
    rows, d = xs.shape
    nf = D_FF // tf
    grid_spec = pltpu.PrefetchScalarGridSpec(
        num_scalar_prefetch=4,
        grid=(nf, tv.shape[0]),
        in_specs=[
            pl.BlockSpec((tm, d), lambda f, v, tv, ev, lo, hi: (tv[v], 0)),
            pl.BlockSpec((None, d, tf), lambda f, v, tv, ev, lo, hi: (ev[v], 0, f)),
            pl.BlockSpec((None, d, tf), lambda f, v, tv, ev, lo, hi: (ev[v], 0, nf + f)),
            pl.BlockSpec((None, 1, tf), lambda f, v, tv, ev, lo, hi: (ev[v], 0, f)),
            pl.BlockSpec((None, 1, tf), lambda f, v, tv, ev, lo, hi: (ev[v], 0, nf + f)),
        ],
        out_specs=pl.BlockSpec((tm, tf), lambda f, v, tv, ev, lo, hi: (tv[v], f)),
        scratch_shapes=[pltpu.VMEM((2, d, tf), bf16)],
    )
    return pl.pallas_call(
        _moe_gate_up_kernel,
        grid_spec=grid_spec,
        out_shape=jax.ShapeDtypeStruct((rows, D_FF), bf16),
        compiler_params=_params("arbitrary", "arbitrary"),
        name="moe_gate_up",
    )(tv, ev, lo, hi, xs, w_gu, w_gu, b_gu3, b_gu3)


def _moe_down_kernel(tv_ref, ev_ref, lo_ref, hi_ref, a_ref, w_ref, b_ref, o_ref, wb_ref):
    v = pl.program_id(0)
    lo = lo_ref[v]
    hi = hi_ref[v]

    @pl.when(_new_expert(ev_ref, v))
    def _():
        wb_ref[...] = w_ref[...].astype(bf16)

    @pl.when(hi > lo)
    def _():
        y = _dot(a_ref[...], wb_ref[...]) + b_ref[...]

        @pl.when(lo == 0)
        def _():
            _store_slabs(o_ref, y)

        @pl.when(lo > 0)
        def _():
            r = lax.broadcasted_iota(i32, (y.shape[0], LANES), 0)
            _store_slabs(o_ref, y, mask=(r >= lo) & (r < hi))


def _moe_down(act, w_dn, b_dn3, tv, ev, lo, hi, *, tm):
    rows, dff = act.shape
    d = w_dn.shape[2]
    g = d // LANES
    grid_spec = pltpu.PrefetchScalarGridSpec(
        num_scalar_prefetch=4,
        grid=(tv.shape[0],),
        in_specs=[
            pl.BlockSpec((tm, dff), lambda v, tv, ev, lo, hi: (tv[v], 0)),
            pl.BlockSpec((None, dff, d), lambda v, tv, ev, lo, hi: (ev[v], 0, 0)),
            pl.BlockSpec((None, 1, d), lambda v, tv, ev, lo, hi: (ev[v], 0, 0)),
        ],
        out_specs=pl.BlockSpec((tm * g, LANES), lambda v, tv, ev, lo, hi: (tv[v], 0)),
        scratch_shapes=[pltpu.VMEM((dff, d), bf16)],
    )
    return pl.pallas_call(
        _moe_down_kernel,
        grid_spec=grid_spec,
        out_shape=jax.ShapeDtypeStruct((rows * g, LANES), f32),
        compiler_params=_params("arbitrary"),
        name="moe_down",
    )(tv, ev, lo, hi, act, w_dn, b_dn3)


def _moe_combine_kernel(pos_ref, y_ref, tw_ref, ys_ref, o_ref, buf_ref, sem, *, rows, g):
    i = pl.program_id(0)
    slot = i % 2
    n_tok = pl.num_programs(0) * rows

    def issue(tile, s):
        for k in range(TOP_K):
            _issue_slab_gather(pos_ref, k * n_tok + tile * rows, ys_ref, buf_ref.at[s, k], sem.at[s], rows, g)

    @pl.when(i == 0)
    def _():
        issue(0, 0)

    @pl.when(i + 1 < pl.num_programs(0))
    def _():
        issue(i + 1, 1 - slot)

    for k in range(TOP_K):
        _wait_slab_gather(ys_ref, buf_ref.at[slot, k], sem.at[slot])
    tw = tw_ref[...]
    for c in range(g):
        cs = slice(c * LANES, (c + 1) * LANES)
        acc = y_ref[:, cs]
        for k in range(TOP_K):
            acc = acc + _load_slab_group(buf_ref, (slot, k), c, rows, g) * tw[:, k:k + 1]
        o_ref[:, cs] = acc


def _moe_combine(y1, ys, pos_kmajor, tw, *, rows):
    n, d = y1.shape
    g = d // LANES
    grid_spec = pltpu.PrefetchScalarGridSpec(
        num_scalar_prefetch=1,
        grid=(n // rows,),
        in_specs=[pl.BlockSpec((rows, d), lambda i, pos: (i, 0)),
                  pl.BlockSpec((rows, LANES), lambda i, pos: (i, 0)),
                  pl.BlockSpec(memory_space=pl.ANY)],
        out_specs=pl.BlockSpec((rows, d), lambda i, pos: (i, 0)),
        scratch_shapes=[pltpu.VMEM((2, TOP_K, rows * g, LANES), f32), pltpu.SemaphoreType.DMA((2,))],
    )
    return pl.pallas_call(
        functools.partial(_moe_combine_kernel, rows=rows, g=g),
        grid_spec=grid_spec,
        out_shape=jax.ShapeDtypeStruct((n, d), f32),
        compiler_params=_params("arbitrary"),
        name="moe_combine",
    )(pos_kmajor, y1, tw, ys)


def _route(top_i, *, tm):
    e = top_i.reshape(-1)
    a = e.shape[0]
    order = jnp.argsort(e, stable=True).astype(i32)
    tok_sorted = order // TOP_K
    pos = jnp.zeros((a,), i32).at[order].set(jnp.arange(a, dtype=i32))
    counts = jnp.sum((e[:, None] == jnp.arange(N_EXPERTS, dtype=i32)[None, :]).astype(i32), axis=0)
    ends = jnp.cumsum(counts)
    starts = ends - counts
    first_tile = starts // tm
    last_tile = jnp.maximum(ends - 1, 0) // tm
    n_vis = jnp.where(counts > 0, last_tile - first_tile + 1, 0)
    vis_end = jnp.cumsum(n_vis)
    vis_start = vis_end - n_vis
    total = vis_end[-1]
    v = jnp.arange(a // tm + N_EXPERTS, dtype=i32)
    vc = jnp.minimum(v, total - 1)
    ev = jnp.sum((vis_end[None, :] <= vc[:, None]).astype(i32), axis=1)
    tv = first_tile[ev] + (vc - vis_start[ev])
    lo = jnp.maximum(starts[ev], tv * tm) - tv * tm
    hi = jnp.minimum(ends[ev], (tv + 1) * tm) - tv * tm
    valid = v < total
    return tok_sorted, pos, tv, ev, jnp.where(valid, lo, 0), jnp.where(valid, hi, 0)


MOE_TM = 256
MOE_TF = 1024
GATHER_ROWS = 256
COMBINE_ROWS = 128


def _moe(y1, h_slabs, top_i, top_w, w_gu, b_gu, w_dn, b_dn):
    n_tok = y1.shape[0]
    tok_sorted, pos, tv, ev, lo, hi = _route(top_i[:, :TOP_K], tm=MOE_TM)
    xs = _gather_x(h_slabs, tok_sorted, rows=GATHER_ROWS, g=SLAB_G)
    act = _moe_gate_up(xs, w_gu, b_gu[:, None, :], tv, ev, lo, hi, tm=MOE_TM, tf=MOE_TF)
    ys = _moe_down(act, w_dn, b_dn[:, None, :], tv, ev, lo, hi, tm=MOE_TM)
    pos_kmajor = pos.reshape(n_tok, TOP_K).T.reshape(-1)
    return _moe_combine(y1, ys, pos_kmajor, top_w, rows=COMBINE_ROWS)


def kernel(x_prompt, x_sample, cache_moba_k, cache_moba_v, state_gdn, state_gdn_conv, cache_mem_k, cache_mem_v, page_table, mem_prompt, rel_bias, norm_mix, w_in, conv_w, a_log, dt_bias, gdn_o_norm, moba_q_norm, moba_k_norm, mem_q_norm, mem_norm, w_mem_kv, mem_k_norm, w_out, norm_ffn, router_w, router_b, w_gu, b_gu, w_dn, b_dn):
    assert x_prompt.shape[0] == 1 and all(a.shape[0] == 1 for a in (w_in, w_out, w_gu, w_dn, state_gdn))
    n_p = x_prompt.shape[1]
    n_seq, n_st = x_sample.shape[:2]
    n_s = n_seq * n_st
    d = x_prompt.shape[2]
    n_blk = n_p // MOBA_BLOCK
    x_all = jnp.concatenate([x_prompt.reshape(n_p, d), x_sample.reshape(n_s, d)], axis=0)

    w = w_in[0]
    off_beta = A_W
    off_moba = off_beta + 2 * GDN_HEADS
    off_mem = off_moba + 3 * MOBA_W
    w_a = w[:, :A_W].astype(bf16)
    w_b = jnp.concatenate(
        [w[:, off_moba:off_moba + MOBA_W], w[:, off_moba + MOBA_W:off_moba + 2 * MOBA_W], w[:, off_mem:],
         w[:, off_moba + 2 * MOBA_W:off_mem], w[:, off_beta:off_moba],
         jnp.zeros((d, LANES - 2 * GDN_HEADS), w.dtype)], axis=1).astype(bf16)
    gain_b = jnp.concatenate([jnp.tile(moba_q_norm[0], MOBA_HEADS), jnp.tile(moba_k_norm[0], MOBA_HEADS),
                              jnp.tile(mem_q_norm[0], MEM_HEADS)])[None]
    proj_a = _norm_matmul(x_all, norm_mix, w_a, gain_b, tm=512, tn=1024, n_norm=0)
    proj_b = _norm_matmul(x_all, norm_mix, w_b, gain_b, tm=256, tn=B_W, n_norm=B_NORM_GROUPS)
    proj_b3 = proj_b[n_p:].reshape(n_seq, n_st, B_W)

    pad_row = lambda a: jnp.zeros((1, LANES), f32).at[0, GDN_HEADS:2 * GDN_HEADS].set(a[0])
    alog_row, dtb_row = pad_row(a_log), pad_row(dt_bias)
    o_gdn_p, p_gdn = _gdn_prompt(proj_a, proj_b, conv_w[0], alog_row, dtb_row, gdn_o_norm, n_tok=n_p)
    p_conv = proj_a[n_p - (GDN_CONV - 1):n_p, :GDN_CONV_DIM]
    xa_s = proj_a[n_p:].reshape(n_seq, n_st, A_W)
    o_gdn_s, s_gdn = _gdn_sample(jnp.swapaxes(xa_s, 0, 1), jnp.swapaxes(proj_b3[..., B_BA:], 0, 1),
                                 jnp.swapaxes(state_gdn_conv[0], 0, 1), conv_w[0], alog_row, dtb_row,
                                 gdn_o_norm, state_gdn[0])
    o_gdn_s = jnp.swapaxes(o_gdn_s, 0, 1).reshape(n_s, GDN_V).astype(bf16)
    s_conv = jnp.concatenate([state_gdn_conv[0], xa_s[..., :GDN_CONV_DIM]], axis=1)[:, n_st:]

    kmean = _block_mean(proj_b, n_blk).reshape(n_blk, MOBA_W)
    sel = _moba_select(proj_b, kmean, n_blk)
    o_moba_p = _moba_prompt(proj_b, sel, rel_bias, n_blk)
    pairs = lambda col: proj_b3[..., col:col + MOBA_W].reshape(n_seq, n_st * MOBA_HEADS, HEAD_DIM)
    o_moba_s = _moba_sample(pairs(B_MQ), pairs(B_MK), pairs(B_MV), cache_moba_k.reshape(-1, HEAD_DIM),
                            cache_moba_v.reshape(-1, HEAD_DIM), page_table, rel_bias, page=cache_moba_k.shape[2])
    o_moba_s = o_moba_s.reshape(n_s, MOBA_W).astype(bf16)

    n_mem = mem_prompt.shape[1]
    mem_kv = _norm_matmul(mem_prompt[0], mem_norm, w_mem_kv[0].astype(bf16),
                          jnp.tile(mem_k_norm[0], MEM_HEADS)[None], tm=n_mem, tn=2 * MEM_W, n_norm=MEM_HEADS)
    o_mem_p = _mem_attn_prompt(proj_b, mem_kv, n_tok=n_p, tq=512)
    o_mem_s = _mem_attn_sample(pairs(B_CQ), cache_mem_k.reshape(n_seq, n_mem * MEM_HEADS, HEAD_DIM),
                               cache_mem_v.reshape(n_seq, n_mem * MEM_HEADS, HEAD_DIM))
    o_mem_s = o_mem_s.reshape(n_s, MEM_W).astype(bf16)

    rw = jnp.pad(router_w[0], ((0, 0), (0, LANES - N_EXPERTS)))
    rb = jnp.pad(router_b, ((0, 0), (0, LANES - N_EXPERTS)))
    y1, h2, top_i, top_w = _outproj_router(
        x_all, jnp.concatenate([o_gdn_p, o_gdn_s]), jnp.concatenate([o_moba_p, o_moba_s]),
        jnp.concatenate([o_mem_p, o_mem_s]), w_out[0].astype(bf16), norm_ffn, rw, rb, tm=256)
    y = _moe(y1, h2, top_i, top_w, w_gu[0], b_gu[0], w_dn[0], b_dn[0])

    heads = lambda a, lead: a.reshape(lead + (MOBA_HEADS, HEAD_DIM))
    return (y[:n_p].reshape(1, n_p, d), y[n_p:].reshape(n_seq, n_st, d),
            heads(proj_b[:n_p, B_MK:B_MK + MOBA_W], (1, 1, n_p)), heads(proj_b[:n_p, B_MV:B_MV + MOBA_W], (1, 1, n_p)),
            p_gdn[None, None], p_conv[None, None],
            heads(mem_kv[:, :MEM_W], (1, 1, n_mem)), heads(mem_kv[:, MEM_W:], (1, 1, n_mem)),
            heads(proj_b3[..., B_MK:B_MK + MOBA_W], (1, n_seq, n_st)),
            heads(proj_b3[..., B_MV:B_MV + MOBA_W], (1, n_seq, n_st)),
            s_gdn[None], s_conv[None])
```

```python
import functools
import math

import numpy as np
import jax
import jax.numpy as jnp
from jax import lax
from jax.experimental import pallas as pl
from jax.experimental.pallas import tpu as pltpu

f32 = jnp.float32
bf16 = jnp.bfloat16
i32 = jnp.int32
HIGHEST = lax.Precision.HIGHEST

LANES = 128
SUBLANES = 8
VMEM_LIMIT = 56 * 1024 * 1024

D_MODEL = 2048
HEAD_DIM = 128
GDN_HEADS = 8
GDN_CONV = 4
GDN_CHUNK = 64
GDN_QK = GDN_HEADS * HEAD_DIM
GDN_V = GDN_HEADS * HEAD_DIM
GDN_CONV_DIM = 2 * GDN_QK + GDN_V
MOBA_HEADS = 4
MOBA_BLOCK = 256
MOBA_TOPK = 3
MOBA_W = MOBA_HEADS * HEAD_DIM
MEM_HEADS = 4
MEM_W = MEM_HEADS * HEAD_DIM
N_BUCKETS = 32
MAX_DISTANCE = 128
N_EXPERTS = 32
TOP_K = 4
D_FF = D_MODEL
SWIGLU_LIMIT = 7.0
SWIGLU_ALPHA = 1.702
EPS = 1e-6
NEG_INF = -1e30

A_W = GDN_CONV_DIM + GDN_V
B_MQ, B_MK, B_CQ, B_MV, B_BA = 0, MOBA_W, 2 * MOBA_W, 3 * MOBA_W, 4 * MOBA_W
B_W = B_BA + LANES
B_NORM_GROUPS = 3 * MOBA_HEADS


def _params(*sem):
    return pltpu.CompilerParams(dimension_semantics=sem, vmem_limit_bytes=VMEM_LIMIT)


def _dot(a, b, **kw):
    return jnp.dot(a, b, preferred_element_type=f32, **kw)


def _dot_nt(a, b, **kw):
    return lax.dot_general(a, b, (((1,), (1,)), ((), ())), preferred_element_type=f32, **kw)


def _dot_tn(a, b, **kw):
    return lax.dot_general(a, b, (((0,), (0,)), ((), ())), preferred_element_type=f32, **kw)


def _rms(x, gain):
    return x * lax.rsqrt(jnp.mean(x * x, axis=-1, keepdims=True) + EPS) * gain


def _silu(x):
    return x * jax.nn.sigmoid(x)


def _norm_matmul_kernel(x_ref, g_ref, w_ref, gain_ref, o_ref, h_ref, *, n_norm):
    @pl.when(pl.program_id(1) == 0)
    def _():
        h_ref[...] = _rms(x_ref[...], g_ref[...]).astype(bf16)

    acc = _dot(h_ref[...], w_ref[...])
    if n_norm == 0:
        o_ref[...] = acc
    else:
        for gi in range(acc.shape[1] // LANES):
            sl = slice(gi * LANES, (gi + 1) * LANES)
            blk = acc[:, sl]
            if gi < n_norm:
                blk = _rms(blk, gain_ref[:, sl])
            o_ref[:, sl] = blk


def _norm_matmul(x, g, w, gain, *, tm, tn, n_norm):
    n, d = x.shape
    wn = w.shape[1]
    return pl.pallas_call(
        functools.partial(_norm_matmul_kernel, n_norm=n_norm),
        grid=(n // tm, wn // tn),
        in_specs=[
            pl.BlockSpec((tm, d), lambda i, j: (i, 0)),
            pl.BlockSpec((1, d), lambda i, j: (0, 0)),
            pl.BlockSpec((d, tn), lambda i, j: (0, j)),
            pl.BlockSpec((1, gain.shape[1]), lambda i, j: (0, 0)),
        ],
        out_specs=pl.BlockSpec((tm, tn), lambda i, j: (i, j)),
        out_shape=jax.ShapeDtypeStruct((n, wn), f32),
        scratch_shapes=[pltpu.VMEM((tm, d), bf16)],
        compiler_params=_params("parallel", "arbitrary"),
        name="norm_matmul",
    )(x, g, w, gain)


INV_BLOCK = 16


def _split_bf16(x):
    hi = x.astype(bf16)
    return hi, (x - hi.astype(f32)).astype(bf16)


def _dot3(a, b):
    a_hi, a_lo = a
    b_hi, b_lo = b
    return _dot(a_hi, b_hi) + (_dot(a_hi, b_lo) + _dot(a_lo, b_hi))


def _unit_lower_inverses(lmats):
    c = lmats[0].shape[0]
    row = lax.broadcasted_iota(i32, (c, c), 0)
    col = lax.broadcasted_iota(i32, (c, c), 1)
    same = lambda n: (row // n) == (col // n)
    eye = jnp.where(row == col, 1.0, 0.0).astype(f32)
    ms = [jnp.where(same(INV_BLOCK), l, 0.0) for l in lmats]
    ps = [eye - m for m in ms]
    power = 1
    while 2 * power < INV_BLOCK:
        sm = [_split_bf16(m) for m in ms]
        ms = [_dot3(s, s) for s in sm]
        sm = [_split_bf16(m) for m in ms]
        ps = [p + _dot3(_split_bf16(p), s) for p, s in zip(ps, sm)]
        power *= 2
    n = 2 * INV_BLOCK
    while n <= c:
        offs = [_split_bf16(jnp.where(same(n) & ~same(n // 2), l, 0.0)) for l in lmats]
        sp = [_split_bf16(p) for p in ps]
        mids = [_dot3(o, s) for o, s in zip(offs, sp)]
        ps = [p - _dot3(s, _split_bf16(m)) for p, s, m in zip(ps, sp, mids)]
        n *= 2
    return ps


def _gdn_prompt_kernel(qkv_ref, z_ref, ba_ref, convw_ref, alog_ref, dtb_ref, onorm_ref,
                       o_ref, s_out_ref, xp_ref, s_ref):
    c = GDN_CHUNK
    step = pl.program_id(0)

    @pl.when(step == 0)
    def _():
        xp_ref[0:SUBLANES, :] = jnp.zeros((SUBLANES, GDN_CONV_DIM), f32)
        s_ref[...] = jnp.zeros_like(s_ref)

    xp_ref[SUBLANES:SUBLANES + c, :] = qkv_ref[...]
    w = convw_ref[...]
    y = None
    for j in range(GDN_CONV):
        lo = SUBLANES - (GDN_CONV - 1) + j
        term = xp_ref[lo:lo + c, :] * w[j:j + 1, :]
        y = term if y is None else y + term
    y = _silu(y)
    xp_ref[0:SUBLANES, :] = xp_ref[c:c + SUBLANES, :]

    ba = ba_ref[...]
    beta_all = jax.nn.sigmoid(ba)
    g_all = -jnp.exp(alog_ref[...]) * jax.nn.softplus(ba + dtb_ref[...])
    row = lax.broadcasted_iota(i32, (c, c), 0)
    col = lax.broadcasted_iota(i32, (c, c), 1)
    incl = row >= col
    strict = row > col
    gcum_all = _dot(jnp.where(incl, 1.0, 0.0).astype(f32), g_all, precision=HIGHEST)
    gcum_t = gcum_all.T
    onorm = onorm_ref[...]

    heads = range(GDN_HEADS)
    head_cols = lambda off, h: slice(off + h * HEAD_DIM, off + (h + 1) * HEAD_DIM)
    qs = [y[:, head_cols(0, h)] for h in heads]
    ks = [y[:, head_cols(GDN_QK, h)] for h in heads]
    vs = [y[:, head_cols(2 * GDN_QK, h)] for h in heads]
    qs = [q * lax.rsqrt(jnp.sum(q * q, axis=-1, keepdims=True) + EPS) * (HEAD_DIM ** -0.5) for q in qs]
    ks = [k * lax.rsqrt(jnp.sum(k * k, axis=-1, keepdims=True) + EPS) for k in ks]
    betas = [beta_all[:, h:h + 1] for h in heads]
    gcs = [gcum_all[:, GDN_HEADS + h:GDN_HEADS + h + 1] for h in heads]
    grs = [gcum_t[GDN_HEADS + h:GDN_HEADS + h + 1, :] for h in heads]
    g_lasts = [gcum_all[c - 1:c, GDN_HEADS + h:GDN_HEADS + h + 1] for h in heads]
    decays = [jnp.exp(jnp.where(incl, gc - gr, -jnp.inf)) for gc, gr in zip(gcs, grs)]
    kbetas = [k * b for k, b in zip(ks, betas)]
    lmats = [_dot_nt(kb, k) * jnp.where(strict, dc, 0.0) for kb, k, dc in zip(kbetas, ks, decays)]
    tinvs = _unit_lower_inverses(lmats)
    egcs = [jnp.exp(gc) for gc in gcs]
    us = [_dot(t, v * b) for t, v, b in zip(tinvs, vs, betas)]
    ws = [_dot(t, kb * e) for t, kb, e in zip(tinvs, kbetas, egcs)]
    intras = [_dot_nt(q, k) * dc for q, k, dc in zip(qs, ks, decays)]
    kdecs = [k * jnp.exp(gl - gc) for k, gl, gc in zip(ks, g_lasts, gcs)]
    ss = [s_ref[h] for h in heads]
    v_news = [u - _dot(w_, s) for u, w_, s in zip(us, ws, ss)]
    os_ = [_dot(q * e, s) + _dot(it, vn) for q, e, s, it, vn in zip(qs, egcs, ss, intras, v_news)]
    for h in heads:
        s_ref[h] = ss[h] * jnp.exp(g_lasts[h]) + _dot_tn(kdecs[h], v_news[h])
    for h in heads:
        hs = head_cols(0, h)
        o_ref[:, hs] = (_rms(os_[h], onorm) * _silu(z_ref[:, hs])).astype(o_ref.dtype)

    @pl.when(step == pl.num_programs(0) - 1)
    def _():
        s_out_ref[...] = s_ref[...]


def _gdn_prompt(proj_a, proj_b, conv_w, alog_row, dtb_row, onorm, *, n_tok):
    c = GDN_CHUNK
    zblk = GDN_CONV_DIM // GDN_V
    return pl.pallas_call(
        _gdn_prompt_kernel,
        grid=(n_tok // c,),
        in_specs=[
            pl.BlockSpec((c, GDN_CONV_DIM), lambda i: (i, 0)),
            pl.BlockSpec((c, GDN_V), lambda i: (i, zblk)),
            pl.BlockSpec((c, LANES), lambda i: (i, B_BA // LANES)),
            pl.BlockSpec((GDN_CONV, GDN_CONV_DIM), lambda i: (0, 0)),
            pl.BlockSpec((1, LANES), lambda i: (0, 0)),
            pl.BlockSpec((1, LANES), lambda i: (0, 0)),
            pl.BlockSpec((1, HEAD_DIM), lambda i: (0, 0)),
        ],
        out_specs=[
            pl.BlockSpec((c, GDN_V), lambda i: (i, 0)),
            pl.BlockSpec((GDN_HEADS, HEAD_DIM, HEAD_DIM), lambda i: (0, 0, 0)),
        ],
        out_shape=[
            jax.ShapeDtypeStruct((n_tok, GDN_V), bf16),
            jax.ShapeDtypeStruct((GDN_HEADS, HEAD_DIM, HEAD_DIM), f32),
        ],
        scratch_shapes=[
            pltpu.VMEM((c + SUBLANES, GDN_CONV_DIM), f32),
            pltpu.VMEM((GDN_HEADS, HEAD_DIM, HEAD_DIM), f32),
        ],
        compiler_params=_params("arbitrary"),
        name="gdn_prompt",
    )(proj_a, proj_a, proj_b, conv_w, alog_row, dtb_row, onorm)


GDN_S_SEQ = SUBLANES


def _gdn_sample_kernel(q_ref, k_ref, v_ref, z_ref, ba_ref, bq_ref, bk_ref, bv_ref, wq_ref, wk_ref, wv_ref,
                       alog_ref, dtb_ref, onorm_ref, s_ref, o_ref, so_ref, *, n_tok):
    h = pl.program_id(1)
    nb = GDN_S_SEQ
    lane = lax.broadcasted_iota(i32, (nb, LANES), 1)

    def conv(x_ref, buf_ref, w_ref, t):
        y = None
        for j in range(GDN_CONV):
            i = t + j
            row = buf_ref[i] if i < GDN_CONV - 1 else x_ref[i - (GDN_CONV - 1)]
            term = row * w_ref[j:j + 1, :]
            y = term if y is None else y + term
        return _silu(y)

    def lane_col(x, idx):
        return jnp.sum(jnp.where(lane == idx, x, 0.0), axis=1, keepdims=True)

    qs, ks, vs, betas, egs = [], [], [], [], []
    for t in range(n_tok):
        q = conv(q_ref, bq_ref, wq_ref, t)
        k = conv(k_ref, bk_ref, wk_ref, t)
        q = q * lax.rsqrt(jnp.sum(q * q, axis=-1, keepdims=True) + EPS) * (HEAD_DIM ** -0.5)
        k = k * lax.rsqrt(jnp.sum(k * k, axis=-1, keepdims=True) + EPS)
        qs.append(q.T)
        ks.append(k.T)
        vs.append(conv(v_ref, bv_ref, wv_ref, t))
        ba = ba_ref[t]
        betas.append(lane_col(jax.nn.sigmoid(ba), h))
        g = -jnp.exp(alog_ref[...]) * jax.nn.softplus(ba + dtb_ref[...])
        egs.append(jnp.exp(lane_col(g, GDN_HEADS + h)))

    states = [s_ref[b] for b in range(nb)]
    for t in range(n_tok):
        kcs = [ks[t][:, b:b + 1] for b in range(nb)]
        qcs = [qs[t][:, b:b + 1] for b in range(nb)]
        states = [s * egs[t][b:b + 1, :] for b, s in enumerate(states)]
        ks_rows = [jnp.sum(kc * s, axis=0, keepdims=True) for kc, s in zip(kcs, states)]
        deltas = [(vs[t][b:b + 1, :] - kr) * betas[t][b:b + 1, :] for b, kr in enumerate(ks_rows)]
        states = [s + kc * dl for s, kc, dl in zip(states, kcs, deltas)]
        o = jnp.concatenate([jnp.sum(qc * s, axis=0, keepdims=True) for qc, s in zip(qcs, states)], axis=0)
        o_ref[t] = _rms(o, onorm_ref[...]) * _silu(z_ref[t])
    for b in range(nb):
        so_ref[b] = states[b]


def _gdn_sample(xa, ba, conv_buf, conv_w, alog_row, dtb_row, onorm, state):
    n_tok, n_seq, _ = xa.shape
    nb = GDN_S_SEQ
    hq, hk, hv, hz = 0, GDN_HEADS, 2 * GDN_HEADS, 3 * GDN_HEADS

    def xspec(off, rows):
        return pl.BlockSpec((rows, nb, HEAD_DIM), lambda i, h: (0, i, off + h))

    def wspec(off):
        return pl.BlockSpec((GDN_CONV, HEAD_DIM), lambda i, h: (0, off + h))

    row = pl.BlockSpec((1, LANES), lambda i, h: (0, 0))
    sspec = pl.BlockSpec((nb, None, HEAD_DIM, HEAD_DIM), lambda i, h: (i, h, 0, 0))
    return pl.pallas_call(
        functools.partial(_gdn_sample_kernel, n_tok=n_tok),
        grid=(n_seq // nb, GDN_HEADS),
        in_specs=[xspec(hq, n_tok), xspec(hk, n_tok), xspec(hv, n_tok), xspec(hz, n_tok),
                  pl.BlockSpec((n_tok, nb, LANES), lambda i, h: (0, i, 0)),
                  xspec(hq, GDN_CONV - 1), xspec(hk, GDN_CONV - 1), xspec(hv, GDN_CONV - 1),
                  wspec(hq), wspec(hk), wspec(hv), row, row, row, sspec],
        out_specs=[pl.BlockSpec((n_tok, nb, HEAD_DIM), lambda i, h: (0, i, h)), sspec],
        out_shape=[jax.ShapeDtypeStruct((n_tok, n_seq, GDN_V), f32),
                   jax.ShapeDtypeStruct(state.shape, f32)],
        compiler_params=_params("parallel", "arbitrary"),
        name="gdn_sample",
    )(xa, xa, xa, xa, ba, conv_buf, conv_buf, conv_buf, conv_w, conv_w, conv_w, alog_row, dtb_row, onorm, state)


def _bucket_thresholds():
    exact = N_BUCKETS // 2
    d = np.arange(0, 2 * MAX_DISTANCE)
    val = np.log(np.maximum(d, 1).astype(np.float64) / exact) / math.log(MAX_DISTANCE / exact) * (N_BUCKETS - exact)
    frac = np.abs(val - np.round(val))[exact + 1:MAX_DISTANCE]
    assert frac.min() > 1e-3, "a bucket boundary sits on an integer distance"
    bucket = np.where(d < exact, d, np.minimum(exact + val.astype(np.int64), N_BUCKETS - 1))
    assert np.all(np.diff(bucket) >= 0)
    return [int(np.argmax(bucket >= b)) for b in range(N_BUCKETS)]


_BUCKET_THR = _bucket_thresholds()


def _bias_from_dist(dist, rb_ref, h):
    v = jnp.full(dist.shape, rb_ref[0, h], f32)
    for b in range(1, N_BUCKETS):
        v = jnp.where(dist >= _BUCKET_THR[b], rb_ref[b, h], v)
    return v


def _topk_mask(s, blk, k):
    nblk = s.shape[1]
    sel = jnp.zeros(s.shape, f32)
    for _ in range(k):
        m = jnp.max(s, axis=1, keepdims=True)
        cand = jnp.where((s == m) & (m > -jnp.inf), blk, nblk)
        pick = blk == jnp.min(cand, axis=1, keepdims=True)
        sel = jnp.where(pick, 1.0, sel)
        s = jnp.where(pick, -jnp.inf, s)
    return sel


def _block_mean_kernel(k_ref, o_ref):
    o_ref[0] = jnp.mean(k_ref[...], axis=0, keepdims=True)


def _block_mean(proj_b, n_blk):
    return pl.pallas_call(
        _block_mean_kernel,
        grid=(n_blk,),
        in_specs=[pl.BlockSpec((MOBA_BLOCK, MOBA_W), lambda i: (i, B_MK // MOBA_W))],
        out_specs=pl.BlockSpec((1, 1, MOBA_W), lambda i: (i, 0, 0)),
        out_shape=jax.ShapeDtypeStruct((n_blk, 1, MOBA_W), f32),
        compiler_params=_params("parallel"),
        name="moba_block_mean",
    )(proj_b)


def _moba_select_kernel(q_ref, km_ref, o_ref):
    own = pl.program_id(0)
    nblk = km_ref.shape[0]
    blk = lax.broadcasted_iota(i32, (MOBA_BLOCK, nblk), 1)
    outs = []
    for h in range(MOBA_HEADS):
        hs = slice(h * HEAD_DIM, (h + 1) * HEAD_DIM)
        s = _dot_nt(q_ref[:, hs], km_ref[:, hs], precision=HIGHEST)
        s = jnp.where(blk < own, s, -jnp.inf)
        outs.append(_topk_mask(s, blk, MOBA_TOPK))
    pad = LANES - MOBA_HEADS * nblk
    if pad:
        outs.append(jnp.zeros((MOBA_BLOCK, pad), f32))
    o_ref[...] = jnp.concatenate(outs, axis=1)


def _moba_select(proj_b, kmean, n_blk):
    assert n_blk * MOBA_HEADS <= LANES
    return pl.pallas_call(
        _moba_select_kernel,
        grid=(n_blk,),
        in_specs=[pl.BlockSpec((MOBA_BLOCK, MOBA_W), lambda i: (i, B_MQ // MOBA_W)),
                  pl.BlockSpec((n_blk, MOBA_W), lambda i: (0, 0))],
        out_specs=pl.BlockSpec((MOBA_BLOCK, LANES), lambda i: (i, 0)),
        out_shape=jax.ShapeDtypeStruct((n_blk * MOBA_BLOCK, LANES), f32),
        compiler_params=_params("parallel"),
        name="moba_select",
    )(proj_b, kmean)


MOBA_QCHUNK = MOBA_BLOCK


def _moba_prompt_kernel(qi_ref, kj_ref, rb_ref, q_ref, k_ref, v_ref, sel_ref, o_ref,
                        bias_ref, m_ref, l_ref, acc_ref, *, n_blk):
    step = pl.program_id(0)
    qi = qi_ref[step]
    kj = kj_ref[step]
    nq = MOBA_BLOCK

    @pl.when(step == 0)
    def _():
        r = lax.broadcasted_iota(i32, (nq, nq), 0)
        c = lax.broadcasted_iota(i32, (nq, nq), 1)
        for h in range(MOBA_HEADS):
            bias_ref[h, 0] = jnp.where(c <= r, _bias_from_dist(r - c, rb_ref, h), NEG_INF)
            bias_ref[h, 1] = _bias_from_dist(r - c + nq, rb_ref, h)
            bias_ref[h, 2] = jnp.full((nq, nq), rb_ref[N_BUCKETS - 1, h], f32)

    first = kj == qi

    @pl.when(first)
    def _():
        m_ref[...] = jnp.full(m_ref.shape, NEG_INF, f32)
        l_ref[...] = jnp.zeros_like(l_ref)
        acc_ref[...] = jnp.zeros_like(acc_ref)

    slot = jnp.where(first, 0, jnp.where(kj == qi - 1, 1, 2))
    heads = range(MOBA_HEADS)
    hcols = [slice(h * HEAD_DIM, (h + 1) * HEAD_DIM) for h in heads]
    kbs = [k_ref[:, hs].astype(bf16) for hs in hcols]
    vbs = [v_ref[:, hs].astype(bf16) for hs in hcols]
    for r0 in range(0, nq, MOBA_QCHUNK):
        rs = slice(r0, r0 + MOBA_QCHUNK)
        sel = sel_ref[rs, :]
        lane = lax.broadcasted_iota(i32, sel.shape, 1)
        ss = [_dot_nt(q_ref[rs, hs].astype(bf16), kb) for hs, kb in zip(hcols, kbs)]
        picked = [jnp.max(jnp.where(lane == h * n_blk + kj, sel, 0.0), axis=1, keepdims=True) for h in heads]
        ss = [s * (HEAD_DIM ** -0.5) + bias_ref[h, slot, rs, :] for h, s in zip(heads, ss)]
        ss = [jnp.where((pk > 0.0) | first, s, NEG_INF) for pk, s in zip(picked, ss)]
        m_prevs = [m_ref[h, rs, :] for h in heads]
        m_news = [jnp.maximum(mp, jnp.max(s, axis=1, keepdims=True)) for mp, s in zip(m_prevs, ss)]
        alphas = [jnp.exp(mp - mn) for mp, mn in zip(m_prevs, m_news)]
        ps = [jnp.exp(s - mn) for s, mn in zip(ss, m_news)]
        pvs = [_dot(p.astype(bf16), vb) for p, vb in zip(ps, vbs)]
        for h in heads:
            l_ref[h, rs, :] = alphas[h] * l_ref[h, rs, :] + jnp.sum(ps[h], axis=1, keepdims=True)
            acc_ref[h, rs, :] = alphas[h] * acc_ref[h, rs, :] + pvs[h]
            m_ref[h, rs, :] = m_news[h]

    @pl.when(kj == 0)
    def _():
        for h in range(MOBA_HEADS):
            hs = slice(h * HEAD_DIM, (h + 1) * HEAD_DIM)
            o_ref[:, hs] = (acc_ref[h] / l_ref[h]).astype(o_ref.dtype)


def _moba_prompt(proj_b, sel, rel_bias, n_blk):
    qi = np.concatenate([np.full(i + 1, i) for i in range(n_blk)]).astype(np.int32)
    kj = np.concatenate([np.arange(i, -1, -1) for i in range(n_blk)]).astype(np.int32)
    nq = MOBA_BLOCK
    grid_spec = pltpu.PrefetchScalarGridSpec(
        num_scalar_prefetch=2,
        grid=(len(qi),),
        in_specs=[
            pl.BlockSpec(memory_space=pltpu.SMEM),
            pl.BlockSpec((nq, MOBA_W), lambda s, qi, kj: (qi[s], B_MQ // MOBA_W)),
            pl.BlockSpec((nq, MOBA_W), lambda s, qi, kj: (kj[s], B_MK // MOBA_W)),
            pl.BlockSpec((nq, MOBA_W), lambda s, qi, kj: (kj[s], B_MV // MOBA_W)),
            pl.BlockSpec((nq, LANES), lambda s, qi, kj: (qi[s], 0)),
        ],
        out_specs=pl.BlockSpec((nq, MOBA_W), lambda s, qi, kj: (qi[s], 0)),
        scratch_shapes=[
            pltpu.VMEM((MOBA_HEADS, 3, nq, nq), f32),
            pltpu.VMEM((MOBA_HEADS, nq, 1), f32),
            pltpu.VMEM((MOBA_HEADS, nq, 1), f32),
            pltpu.VMEM((MOBA_HEADS, nq, HEAD_DIM), f32),
        ],
    )
    return pl.pallas_call(
        functools.partial(_moba_prompt_kernel, n_blk=n_blk),
        grid_spec=grid_spec,
        out_shape=jax.ShapeDtypeStruct((n_blk * nq, MOBA_W), bf16),
        compiler_params=_params("arbitrary"),
        name="moba_prompt",
    )(jnp.asarray(qi), jnp.asarray(kj), rel_bias, proj_b, proj_b, proj_b, sel)


def _per_head_bias(dist, row_head, rb_ref):
    v = _bias_from_dist(dist, rb_ref, 0)
    for h in range(1, MOBA_HEADS):
        v = jnp.where(row_head == h, _bias_from_dist(dist, rb_ref, h), v)
    return v


def _moba_sample_kernel(pt_ref, rb_ref, q_ref, kn_ref, vn_ref, *rest, n_pages, page):
    kp = rest[:n_pages]
    vp = rest[n_pages:2 * n_pages]
    o_ref, bias_ref, biasn_ref = rest[2 * n_pages:]
    nh = MOBA_HEADS
    rows = q_ref.shape[0]
    prow = page * nh
    past = n_pages * page
    n_blk = past // MOBA_BLOCK
    ppb = MOBA_BLOCK // page

    @pl.when(pl.program_id(0) == 0)
    def _():
        r = lax.broadcasted_iota(i32, (rows, past * nh), 0)
        c = lax.broadcasted_iota(i32, (rows, past * nh), 1)
        bias = _per_head_bias(past + r // nh - c // nh, r % nh, rb_ref)
        bias_ref[...] = jnp.where(r % nh == c % nh, bias, NEG_INF)
        r = lax.broadcasted_iota(i32, (rows, rows), 0)
        c = lax.broadcasted_iota(i32, (rows, rows), 1)
        bias = _per_head_bias(r // nh - c // nh, r % nh, rb_ref)
        biasn_ref[...] = jnp.where((r % nh == c % nh) & (c // nh <= r // nh), bias, NEG_INF)

    kb, means = [], []
    for p in range(n_pages):
        kpage = kp[p][...]
        kb.append(kpage.astype(bf16))
        part = jnp.sum(kpage.reshape(prow // SUBLANES, SUBLANES, HEAD_DIM), axis=0)
        part = part[0:nh] + part[nh:2 * nh]
        if p % ppb == 0:
            means.append(part)
        else:
            means[-1] = means[-1] + part
    kmean = jnp.concatenate(means, axis=0) / MOBA_BLOCK

    q = q_ref[...]
    r = lax.broadcasted_iota(i32, (rows, n_blk * nh), 0)
    c = lax.broadcasted_iota(i32, (rows, n_blk * nh), 1)
    s = jnp.where(r % nh == c % nh, _dot_nt(q, kmean, precision=HIGHEST), -jnp.inf)
    sel = _topk_mask(s, c, MOBA_TOPK)
    picked = [jnp.max(jnp.where(c // nh == n, sel, 0.0), axis=1, keepdims=True) > 0.0 for n in range(n_blk)]

    qb = q.astype(bf16)
    scale = HEAD_DIM ** -0.5
    ln = _dot_nt(qb, kn_ref[...].astype(bf16)) * scale + biasn_ref[...]
    m = jnp.max(ln, axis=1, keepdims=True)
    lps = []
    for p in range(n_pages):
        lp = _dot_nt(qb, kb[p]) * scale + bias_ref[:, p * prow:(p + 1) * prow]
        lp = jnp.where(picked[p // ppb], lp, NEG_INF)
        m = jnp.maximum(m, jnp.max(lp, axis=1, keepdims=True))
        lps.append(lp)
    pn = jnp.exp(ln - m)
    den = jnp.sum(pn, axis=1, keepdims=True)
    num = _dot(pn.astype(bf16), vn_ref[...].astype(bf16))
    for p in range(n_pages):
        pp = jnp.exp(lps[p] - m)
        den = den + jnp.sum(pp, axis=1, keepdims=True)
        num = num + _dot(pp.astype(bf16), vp[p][...].astype(bf16))
    o_ref[...] = num / den


def _moba_sample(q, k_new, v_new, pool_k, pool_v, page_table, rel_bias, *, page):
    n_seq, rows, _ = q.shape
    n_pages = page_table.shape[1]
    prow = page * MOBA_HEADS
    new_spec = pl.BlockSpec((None, rows, HEAD_DIM), lambda b, pt: (b, 0, 0))

    def page_spec(p):
        return pl.BlockSpec((prow, HEAD_DIM), lambda b, pt: (pt[b, p], 0))

    grid_spec = pltpu.PrefetchScalarGridSpec(
        num_scalar_prefetch=1,
        grid=(n_seq,),
        in_specs=[pl.BlockSpec(memory_space=pltpu.SMEM), new_spec, new_spec, new_spec]
        + [page_spec(p) for p in range(n_pages)] * 2,
        out_specs=new_spec,
        scratch_shapes=[pltpu.VMEM((rows, n_pages * prow), f32), pltpu.VMEM((rows, rows), f32)],
    )
    return pl.pallas_call(
        functools.partial(_moba_sample_kernel, n_pages=n_pages, page=page),
        grid_spec=grid_spec,
        out_shape=jax.ShapeDtypeStruct((n_seq, rows, HEAD_DIM), f32),
        compiler_params=_params("arbitrary"),
        name="moba_sample",
    )(page_table, rel_bias, q, k_new, v_new, *([pool_k] * n_pages), *([pool_v] * n_pages))


def _mem_attn_kernel(q_ref, k_ref, v_ref, o_ref):
    for h in range(MEM_HEADS):
        hs = slice(h * HEAD_DIM, (h + 1) * HEAD_DIM)
        s = _dot_nt(q_ref[:, hs].astype(bf16), k_ref[:, hs].astype(bf16)) * (HEAD_DIM ** -0.5)
        p = jnp.exp(s - jnp.max(s, axis=1, keepdims=True))
        num = _dot(p.astype(bf16), v_ref[:, hs].astype(bf16))
        o_ref[:, hs] = (num / jnp.sum(p, axis=1, keepdims=True)).astype(o_ref.dtype)


def _mem_attn_prompt(proj_b, mem_kv, *, n_tok, tq):
    n_mem = mem_kv.shape[0]
    return pl.pallas_call(
        _mem_attn_kernel,
        grid=(n_tok // tq,),
        in_specs=[pl.BlockSpec((tq, MEM_W), lambda i: (i, B_CQ // MEM_W)),
                  pl.BlockSpec((n_mem, MEM_W), lambda i: (0, 0)),
                  pl.BlockSpec((n_mem, MEM_W), lambda i: (0, 1))],
        out_specs=pl.BlockSpec((tq, MEM_W), lambda i: (i, 0)),
        out_shape=jax.ShapeDtypeStruct((n_tok, MEM_W), bf16),
        compiler_params=_params("parallel"),
        name="mem_attn_prompt",
    )(proj_b, mem_kv, mem_kv)


def _mem_attn_sample_kernel(q_ref, k_ref, v_ref, o_ref):
    s = _dot_nt(q_ref[...].astype(bf16), k_ref[...].astype(bf16)) * (HEAD_DIM ** -0.5)
    r = lax.broadcasted_iota(i32, s.shape, 0)
    c = lax.broadcasted_iota(i32, s.shape, 1)
    s = jnp.where(r % MEM_HEADS == c % MEM_HEADS, s, NEG_INF)
    p = jnp.exp(s - jnp.max(s, axis=1, keepdims=True))
    o_ref[...] = _dot(p.astype(bf16), v_ref[...].astype(bf16)) / jnp.sum(p, axis=1, keepdims=True)


def _mem_attn_sample(q, mem_k, mem_v):
    n_seq, rows, _ = q.shape
    mrows = mem_k.shape[1]
    return pl.pallas_call(
        _mem_attn_sample_kernel,
        grid=(n_seq,),
        in_specs=[pl.BlockSpec((None, rows, HEAD_DIM), lambda b: (b, 0, 0)),
                  pl.BlockSpec((None, mrows, HEAD_DIM), lambda b: (b, 0, 0)),
                  pl.BlockSpec((None, mrows, HEAD_DIM), lambda b: (b, 0, 0))],
        out_specs=pl.BlockSpec((None, rows, HEAD_DIM), lambda b: (b, 0, 0)),
        out_shape=jax.ShapeDtypeStruct((n_seq, rows, HEAD_DIM), f32),
        compiler_params=_params("parallel"),
        name="mem_attn_sample",
    )(q, mem_k, mem_v)


def _outproj_router_kernel(x_ref, og_ref, om_ref, oc_ref, w_ref, g_ref, rw_ref, rb_ref,
                           y_ref, h_ref, ti_ref, tw_ref):
    mix = (_dot(og_ref[...], w_ref[0:GDN_V, :])
           + _dot(om_ref[...], w_ref[GDN_V:GDN_V + MOBA_W, :])
           + _dot(oc_ref[...], w_ref[GDN_V + MOBA_W:, :]))
    y = x_ref[...] + mix
    y_ref[...] = y
    h = _rms(y, g_ref[...])
    _store_slabs(h_ref, h)
    h_hi, h_lo = _split_bf16(h)
    w_hi, w_lo = _split_bf16(rw_ref[...])
    logits = _dot(h_hi, w_hi) + (_dot(h_hi, w_lo) + _dot(h_lo, w_hi)) + rb_ref[...]
    lane = lax.broadcasted_iota(i32, logits.shape, 1)
    s = jnp.where(lane < N_EXPERTS, logits, -jnp.inf)
    vals, idxs = [], []
    for _ in range(TOP_K):
        m = jnp.max(s, axis=1, keepdims=True)
        idx = jnp.min(jnp.where(s == m, lane, LANES), axis=1, keepdims=True)
        vals.append(m)
        idxs.append(idx)
        s = jnp.where(lane == idx, -jnp.inf, s)
    exps = [jnp.exp(v - vals[0]) for v in vals]
    den = exps[0]
    for e in exps[1:]:
        den = den + e
    ti = jnp.zeros(logits.shape, i32)
    tw = jnp.zeros(logits.shape, f32)
    for r in range(TOP_K):
        ti = jnp.where(lane == r, idxs[r], ti)
        tw = jnp.where(lane == r, exps[r] / den, tw)
    ti_ref[...] = ti
    tw_ref[...] = tw


def _outproj_router(x, og, om, oc, w_out, g, rw, rb, *, tm):
    n, d = x.shape
    row = lambda w: pl.BlockSpec((tm, w), lambda i: (i, 0))
    full = lambda a: pl.BlockSpec(a.shape, lambda i: (0,) * a.ndim)
    return pl.pallas_call(
        _outproj_router_kernel,
        grid=(n // tm,),
        in_specs=[row(d), row(GDN_V), row(MOBA_W), row(MEM_W), full(w_out), full(g), full(rw), full(rb)],
        out_specs=[row(d), pl.BlockSpec((tm * (d // LANES), LANES), lambda i: (i, 0)), row(LANES), row(LANES)],
        out_shape=[jax.ShapeDtypeStruct((n, d), f32), jax.ShapeDtypeStruct((n * (d // LANES), LANES), f32),
                   jax.ShapeDtypeStruct((n, LANES), i32), jax.ShapeDtypeStruct((n, LANES), f32)],
        compiler_params=_params("parallel"),
        name="outproj_router",
    )(x, og, om, oc, w_out, g, rw, rb)


GATHER_UNROLL = 8
SLAB_G = D_MODEL // LANES


def _store_slabs(slab_ref, x, mask=None):
    rows = x.shape[0]
    g = x.shape[1] // LANES
    for c in range(g):
        idx = (pl.ds(c, rows, stride=g), slice(None))
        blk = x[:, c * LANES:(c + 1) * LANES]
        slab_ref[idx] = blk if mask is None else jnp.where(mask, blk, slab_ref[idx])


def _load_slab_group(slab_ref, lead, c, rows, g):
    return slab_ref[lead + (pl.ds(c, rows, stride=g), slice(None))]


def _issue_slab_gather(idx_ref, idx_base, src_ref, dst_ref, sem, count, g):
    assert count % GATHER_UNROLL == 0

    def body(jj, carry):
        for u in range(GATHER_UNROLL):
            j = jj * GATHER_UNROLL + u
            tok = pl.multiple_of(idx_ref[idx_base + j] * g, g)
            dst = dst_ref.at[pl.ds(pl.multiple_of(j * g, g), g)]
            pltpu.make_async_copy(src_ref.at[pl.ds(tok, g)], dst, sem).start(priority=u % 2)
        return carry

    lax.fori_loop(0, count // GATHER_UNROLL, body, 0)


def _wait_slab_gather(src_ref, dst_ref, sem):
    pltpu.make_async_copy(src_ref.at[pl.ds(0, dst_ref.shape[0])], dst_ref, sem).wait()


def _gather_x_kernel(idx_ref, src_ref, o_ref, buf_ref, sem, *, rows, g):
    i = pl.program_id(0)
    slot = i % 2

    @pl.when(i == 0)
    def _():
        _issue_slab_gather(idx_ref, 0, src_ref, buf_ref.at[0], sem.at[0], rows, g)

    @pl.when(i + 1 < pl.num_programs(0))
    def _():
        _issue_slab_gather(idx_ref, (i + 1) * rows, src_ref, buf_ref.at[1 - slot], sem.at[1 - slot], rows, g)

    _wait_slab_gather(src_ref, buf_ref.at[slot], sem.at[slot])
    for c in range(g):
        o_ref[:, c * LANES:(c + 1) * LANES] = _load_slab_group(buf_ref, (slot,), c, rows, g).astype(o_ref.dtype)


def _gather_x(src, idx, *, rows, g):
    n_out = idx.shape[0]
    grid_spec = pltpu.PrefetchScalarGridSpec(
        num_scalar_prefetch=1,
        grid=(n_out // rows,),
        in_specs=[pl.BlockSpec(memory_space=pl.ANY)],
        out_specs=pl.BlockSpec((rows, g * LANES), lambda i, idx: (i, 0)),
        scratch_shapes=[pltpu.VMEM((2, rows * g, LANES), f32), pltpu.SemaphoreType.DMA((2,))],
    )
    return pl.pallas_call(
        functools.partial(_gather_x_kernel, rows=rows, g=g),
        grid_spec=grid_spec,
        out_shape=jax.ShapeDtypeStruct((n_out, g * LANES), bf16),
        compiler_params=_params("arbitrary"),
        name="moe_gather_x",
    )(idx, src)


def _visit_store(o_ref, new, lo, hi):
    @pl.when(lo == 0)
    def _():
        o_ref[...] = new

    @pl.when(lo > 0)
    def _():
        r = lax.broadcasted_iota(i32, new.shape, 0)
        o_ref[...] = jnp.where((r >= lo) & (r < hi), new, o_ref[...])


def _new_expert(ev_ref, v):
    return (v == 0) | (ev_ref[v] != ev_ref[jnp.maximum(v - 1, 0)])


def _expert_weights_step(copies, is_new, run, n_runs, e_here, e_next, e_first, sweep, n_sweeps, on_arrival):
    slot = (run + sweep * n_runs) % 2

    @pl.when(is_new)
    def _():
        @pl.when((sweep == 0) & (run == 0))
        def _():
            for c in copies(e_here, sweep, slot):
                c.start()

        for c in copies(e_here, sweep, slot):
            c.wait()
        on_arrival(slot)
        last = run == n_runs - 1

        @pl.when(jnp.logical_not(last))
        def _():
            for c in copies(e_next, sweep, 1 - slot):
                c.start()

        @pl.when(last & (sweep + 1 < n_sweeps))
        def _():
            for c in copies(e_first, sweep + 1, 1 - slot):
                c.start()


def _moe_gate_up_kernel(tv_ref, ev_ref, lo_ref, hi_ref, run_ref, nxt_ref, nrun_ref, x_ref, w_hbm, bg_ref, bu_ref,
                        o_ref, wf_ref, wb_ref, sem, *, tf):
    f = pl.program_id(0)
    v = pl.program_id(1)
    lo = lo_ref[v]
    hi = hi_ref[v]

    def copies(e, sweep, slot):
        cols = lambda half: pl.ds(pl.multiple_of(half * D_FF + sweep * tf, tf), tf)
        return [pltpu.make_async_copy(w_hbm.at[e, :, cols(half)], wf_ref.at[slot, half], sem.at[slot])
                for half in (0, 1)]

    def on_arrival(slot):
        wb_ref[0] = wf_ref[slot, 0].astype(bf16)
        wb_ref[1] = wf_ref[slot, 1].astype(bf16)

    _expert_weights_step(copies, _new_expert(ev_ref, v), run_ref[v], nrun_ref[0], ev_ref[v], nxt_ref[v], ev_ref[0],
                         f, pl.num_programs(0), on_arrival)

    @pl.when(hi > lo)
    def _():
        x = x_ref[...]
        gate = _dot(x, wb_ref[0]) + bg_ref[...]
        up = _dot(x, wb_ref[1]) + bu_ref[...]
        gate = jnp.minimum(gate, SWIGLU_LIMIT)
        up = jnp.clip(up, -SWIGLU_LIMIT, SWIGLU_LIMIT)
        act = gate * jax.nn.sigmoid(SWIGLU_ALPHA * gate) * (up + 1.0)
        _visit_store(o_ref, act.astype(o_ref.dtype), lo, hi)


def _moe_gate_up(xs, w_gu, b_gu3, visits, *, tm, tf):
    rows, d = xs.shape
    nf = D_FF // tf
    n_pre = len(visits)
    grid_spec = pltpu.PrefetchScalarGridSpec(
        num_scalar_prefetch=n_pre,
        grid=(nf, visits[0].shape[0]),
        in_specs=[
            pl.BlockSpec((tm, d), lambda f, v, tv, *_: (tv[v], 0)),
            pl.BlockSpec(memory_space=pl.ANY),
            pl.BlockSpec((None, 1, tf), lambda f, v, tv, ev, *_: (ev[v], 0, f)),
            pl.BlockSpec((None, 1, tf), lambda f, v, tv, ev, *_: (ev[v], 0, nf + f)),
        ],
        out_specs=pl.BlockSpec((tm, tf), lambda f, v, tv, *_: (tv[v], f)),
        scratch_shapes=[pltpu.VMEM((2, 2, d, tf), f32), pltpu.VMEM((2, d, tf), bf16), pltpu.SemaphoreType.DMA((2,))],
    )
    return pl.pallas_call(
        functools.partial(_moe_gate_up_kernel, tf=tf),
        grid_spec=grid_spec,
        out_shape=jax.ShapeDtypeStruct((rows, D_FF), bf16),
        compiler_params=_params("arbitrary", "arbitrary"),
        name="moe_gate_up",
    )(*visits, xs, w_gu, b_gu3, b_gu3)


def _moe_down_kernel(tv_ref, ev_ref, lo_ref, hi_ref, run_ref, nxt_ref, nrun_ref, a_ref, w_hbm, b_ref,
                     o_ref, wf_ref, wb_ref, sem):
    v = pl.program_id(0)
    lo = lo_ref[v]
    hi = hi_ref[v]

    def copies(e, sweep, slot):
        return [pltpu.make_async_copy(w_hbm.at[e], wf_ref.at[slot], sem.at[slot])]

    def on_arrival(slot):
        wb_ref[...] = wf_ref[slot].astype(bf16)

    _expert_weights_step(copies, _new_expert(ev_ref, v), run_ref[v], nrun_ref[0], ev_ref[v], nxt_ref[v], ev_ref[0],
                         0, 1, on_arrival)

    @pl.when(hi > lo)
    def _():
        y = _dot(a_ref[...], wb_ref[...]) + b_ref[...]

        @pl.when(lo == 0)
        def _():
            _store_slabs(o_ref, y)

        @pl.when(lo > 0)
        def _():
            r = lax.broadcasted_iota(i32, (y.shape[0], LANES), 0)
            _store_slabs(o_ref, y, mask=(r >= lo) & (r < hi))


def _moe_down(act, w_dn, b_dn3, visits, *, tm):
    rows, dff = act.shape
    d = w_dn.shape[2]
    g = d // LANES
    grid_spec = pltpu.PrefetchScalarGridSpec(
        num_scalar_prefetch=len(visits),
        grid=(visits[0].shape[0],),
        in_specs=[
            pl.BlockSpec((tm, dff), lambda v, tv, *_: (tv[v], 0)),
            pl.BlockSpec(memory_space=pl.ANY),
            pl.BlockSpec((None, 1, d), lambda v, tv, ev, *_: (ev[v], 0, 0)),
        ],
        out_specs=pl.BlockSpec((tm * g, LANES), lambda v, tv, *_: (tv[v], 0)),
        scratch_shapes=[pltpu.VMEM((2, dff, d), f32), pltpu.VMEM((dff, d), bf16), pltpu.SemaphoreType.DMA((2,))],
    )
    return pl.pallas_call(
        _moe_down_kernel,
        grid_spec=grid_spec,
        out_shape=jax.ShapeDtypeStruct((rows * g, LANES), f32),
        compiler_params=_params("arbitrary"),
        name="moe_down",
    )(*visits, act, w_dn, b_dn3)


def _moe_combine_kernel(pos_ref, y_ref, tw_ref, ys_ref, o_ref, buf_ref, sem, *, rows, g):
    i = pl.program_id(0)
    slot = i % 2
    n_tok = pl.num_programs(0) * rows

    def issue(tile, s):
        for k in range(TOP_K):
            _issue_slab_gather(pos_ref, k * n_tok + tile * rows, ys_ref, buf_ref.at[s, k], sem.at[s], rows, g)

    @pl.when(i == 0)
    def _():
        issue(0, 0)

    @pl.when(i + 1 < pl.num_programs(0))
    def _():
        issue(i + 1, 1 - slot)

    for k in range(TOP_K):
        _wait_slab_gather(ys_ref, buf_ref.at[slot, k], sem.at[slot])
    tw = tw_ref[...]
    for c in range(g):
        cs = slice(c * LANES, (c + 1) * LANES)
        acc = y_ref[:, cs]
        for k in range(TOP_K):
            acc = acc + _load_slab_group(buf_ref, (slot, k), c, rows, g) * tw[:, k:k + 1]
        o_ref[:, cs] = acc


def _moe_combine(y1, ys, pos_kmajor, tw, *, rows):
    n, d = y1.shape
    g = d // LANES
    grid_spec = pltpu.PrefetchScalarGridSpec(
        num_scalar_prefetch=1,
        grid=(n // rows,),
        in_specs=[pl.BlockSpec((rows, d), lambda i, pos: (i, 0)),
                  pl.BlockSpec((rows, LANES), lambda i, pos: (i, 0)),
                  pl.BlockSpec(memory_space=pl.ANY)],
        out_specs=pl.BlockSpec((rows, d), lambda i, pos: (i, 0)),
        scratch_shapes=[pltpu.VMEM((2, TOP_K, rows * g, LANES), f32), pltpu.SemaphoreType.DMA((2,))],
    )
    return pl.pallas_call(
        functools.partial(_moe_combine_kernel, rows=rows, g=g),
        grid_spec=grid_spec,
        out_shape=jax.ShapeDtypeStruct((n, d), f32),
        compiler_params=_params("arbitrary"),
        name="moe_combine",
    )(pos_kmajor, y1, tw, ys)


def _route(top_i, *, tm):
    e = top_i.reshape(-1)
    a = e.shape[0]
    order = jnp.argsort(e, stable=True).astype(i32)
    tok_sorted = order // TOP_K
    pos = jnp.zeros((a,), i32).at[order].set(jnp.arange(a, dtype=i32))
    counts = jnp.sum((e[:, None] == jnp.arange(N_EXPERTS, dtype=i32)[None, :]).astype(i32), axis=0)
    ends = jnp.cumsum(counts)
    starts = ends - counts
    first_tile = starts // tm
    last_tile = jnp.maximum(ends - 1, 0) // tm
    n_vis = jnp.where(counts > 0, last_tile - first_tile + 1, 0)
    vis_end = jnp.cumsum(n_vis)
    vis_start = vis_end - n_vis
    total = vis_end[-1]
    v = jnp.arange(a // tm + N_EXPERTS, dtype=i32)
    vc = jnp.minimum(v, total - 1)
    ev = jnp.sum((vis_end[None, :] <= vc[:, None]).astype(i32), axis=1)
    tv = first_tile[ev] + (vc - vis_start[ev])
    lo = jnp.maximum(starts[ev], tv * tm) - tv * tm
    hi = jnp.minimum(ends[ev], (tv + 1) * tm) - tv * tm
    valid = v < total
    experts = jnp.arange(N_EXPERTS, dtype=i32)
    present = n_vis > 0
    n_runs = jnp.sum(present.astype(i32))
    run = jnp.sum((present[None, :] & (experts[None, :] < ev[:, None])).astype(i32), axis=1)
    rank = jnp.cumsum(present.astype(i32))
    run_expert = jnp.sum((rank[None, :] <= experts[:, None]).astype(i32), axis=1)
    nxt = run_expert[jnp.minimum(run + 1, n_runs - 1)]
    visits = (tv, ev, jnp.where(valid, lo, 0), jnp.where(valid, hi, 0), run, nxt, n_runs[None])
    return tok_sorted, pos, visits


MOE_TM = 256
MOE_TF = 1024
GATHER_ROWS = 256
COMBINE_ROWS = 128


def _moe(y1, h_slabs, top_i, top_w, w_gu, b_gu, w_dn, b_dn):
    n_tok = y1.shape[0]
    tok_sorted, pos, visits = _route(top_i[:, :TOP_K], tm=MOE_TM)
    xs = _gather_x(h_slabs, tok_sorted, rows=GATHER_ROWS, g=SLAB_G)
    act = _moe_gate_up(xs, w_gu, b_gu[:, None, :], visits, tm=MOE_TM, tf=MOE_TF)
    ys = _moe_down(act, w_dn, b_dn[:, None, :], visits, tm=MOE_TM)
    pos_kmajor = pos.reshape(n_tok, TOP_K).T.reshape(-1)
    return _moe_combine(y1, ys, pos_kmajor, top_w, rows=COMBINE_ROWS)


def kernel(x_prompt, x_sample, cache_moba_k, cache_moba_v, state_gdn, state_gdn_conv, cache_mem_k, cache_mem_v, page_table, mem_prompt, rel_bias, norm_mix, w_in, conv_w, a_log, dt_bias, gdn_o_norm, moba_q_norm, moba_k_norm, mem_q_norm, mem_norm, w_mem_kv, mem_k_norm, w_out, norm_ffn, router_w, router_b, w_gu, b_gu, w_dn, b_dn):
    assert x_prompt.shape[0] == 1 and all(a.shape[0] == 1 for a in (w_in, w_out, w_gu, w_dn, state_gdn))
    n_p = x_prompt.shape[1]
    n_seq, n_st = x_sample.shape[:2]
    n_s = n_seq * n_st
    d = x_prompt.shape[2]
    n_blk = n_p // MOBA_BLOCK
    x_all = jnp.concatenate([x_prompt.reshape(n_p, d), x_sample.reshape(n_s, d)], axis=0)

    w = w_in[0]
    off_beta = A_W
    off_moba = off_beta + 2 * GDN_HEADS
    off_mem = off_moba + 3 * MOBA_W
    w_a = w[:, :A_W].astype(bf16)
    w_b = jnp.concatenate(
        [w[:, off_moba:off_moba + MOBA_W], w[:, off_moba + MOBA_W:off_moba + 2 * MOBA_W], w[:, off_mem:],
         w[:, off_moba + 2 * MOBA_W:off_mem], w[:, off_beta:off_moba],
         jnp.zeros((d, LANES - 2 * GDN_HEADS), w.dtype)], axis=1).astype(bf16)
    gain_b = jnp.concatenate([jnp.tile(moba_q_norm[0], MOBA_HEADS), jnp.tile(moba_k_norm[0], MOBA_HEADS),
                              jnp.tile(mem_q_norm[0], MEM_HEADS)])[None]
    proj_a = _norm_matmul(x_all, norm_mix, w_a, gain_b, tm=512, tn=1024, n_norm=0)
    proj_b = _norm_matmul(x_all, norm_mix, w_b, gain_b, tm=256, tn=B_W, n_norm=B_NORM_GROUPS)
    proj_b3 = proj_b[n_p:].reshape(n_seq, n_st, B_W)

    pad_row = lambda a: jnp.zeros((1, LANES), f32).at[0, GDN_HEADS:2 * GDN_HEADS].set(a[0])
    alog_row, dtb_row = pad_row(a_log), pad_row(dt_bias)
    o_gdn_p, p_gdn = _gdn_prompt(proj_a, proj_b, conv_w[0], alog_row, dtb_row, gdn_o_norm, n_tok=n_p)
    p_conv = proj_a[n_p - (GDN_CONV - 1):n_p, :GDN_CONV_DIM]
    xa_s = proj_a[n_p:].reshape(n_seq, n_st, A_W)
    o_gdn_s, s_gdn = _gdn_sample(jnp.swapaxes(xa_s, 0, 1), jnp.swapaxes(proj_b3[..., B_BA:], 0, 1),
                                 jnp.swapaxes(state_gdn_conv[0], 0, 1), conv_w[0], alog_row, dtb_row,
                                 gdn_o_norm, state_gdn[0])
    o_gdn_s = jnp.swapaxes(o_gdn_s, 0, 1).reshape(n_s, GDN_V).astype(bf16)
    s_conv = jnp.concatenate([state_gdn_conv[0], xa_s[..., :GDN_CONV_DIM]], axis=1)[:, n_st:]

    kmean = _block_mean(proj_b, n_blk).reshape(n_blk, MOBA_W)
    sel = _moba_select(proj_b, kmean, n_blk)
    o_moba_p = _moba_prompt(proj_b, sel, rel_bias, n_blk)
    pairs = lambda col: proj_b3[..., col:col + MOBA_W].reshape(n_seq, n_st * MOBA_HEADS, HEAD_DIM)
    o_moba_s = _moba_sample(pairs(B_MQ), pairs(B_MK), pairs(B_MV), cache_moba_k.reshape(-1, HEAD_DIM),
                            cache_moba_v.reshape(-1, HEAD_DIM), page_table, rel_bias, page=cache_moba_k.shape[2])
    o_moba_s = o_moba_s.reshape(n_s, MOBA_W).astype(bf16)

    n_mem = mem_prompt.shape[1]
    mem_kv = _norm_matmul(mem_prompt[0], mem_norm, w_mem_kv[0].astype(bf16),
                          jnp.tile(mem_k_norm[0], MEM_HEADS)[None], tm=n_mem, tn=2 * MEM_W, n_norm=MEM_HEADS)
    o_mem_p = _mem_attn_prompt(proj_b, mem_kv, n_tok=n_p, tq=512)
    o_mem_s = _mem_attn_sample(pairs(B_CQ), cache_mem_k.reshape(n_seq, n_mem * MEM_HEADS, HEAD_DIM),
                               cache_mem_v.reshape(n_seq, n_mem * MEM_HEADS, HEAD_DIM))
    o_mem_s = o_mem_s.reshape(n_s, MEM_W).astype(bf16)

    rw = jnp.pad(router_w[0], ((0, 0), (0, LANES - N_EXPERTS)))
    rb = jnp.pad(router_b, ((0, 0), (0, LANES - N_EXPERTS)))
    y1, h2, top_i, top_w = _outproj_router(
        x_all, jnp.concatenate([o_gdn_p, o_gdn_s]), jnp.concatenate([o_moba_p, o_moba_s]),
        jnp.concatenate([o_mem_p, o_mem_s]), w_out[0].astype(bf16), norm_ffn, rw, rb, tm=256)
    y = _moe(y1, h2, top_i, top_w, w_gu[0], b_gu[0], w_dn[0], b_dn[0])

    heads = lambda a, lead: a.reshape(lead + (MOBA_HEADS, HEAD_DIM))
    return (y[:n_p].reshape(1, n_p, d), y[n_p:].reshape(n_seq, n_st, d),
            heads(proj_b[:n_p, B_MK:B_MK + MOBA_W], (1, 1, n_p)), heads(proj_b[:n_p, B_MV:B_MV + MOBA_W], (1, 1, n_p)),
            p_gdn[None, None], p_conv[None, None],
            heads(mem_kv[:, :MEM_W], (1, 1, n_mem)), heads(mem_kv[:, MEM_W:], (1, 1, n_mem)),
            heads(proj_b3[..., B_MK:B_MK + MOBA_W], (1, n_seq, n_st)),
            heads(proj_b3[..., B_MV:B_MV + MOBA_W], (1, n_seq, n_st)),
            s_gdn[None], s_conv[None])
```

```python
import functools
import math

import numpy as np
import jax
import jax.numpy as jnp
from jax import lax
from jax.experimental import pallas as pl
from jax.experimental.pallas import tpu as pltpu

f32 = jnp.float32
bf16 = jnp.bfloat16
i32 = jnp.int32
HIGHEST = lax.Precision.HIGHEST

LANES = 128
SUBLANES = 8
VMEM_LIMIT = 60 * 1024 * 1024

D_MODEL = 2048
HEAD_DIM = 128
GDN_HEADS = 8
GDN_CONV = 4
GDN_CHUNK = 64
GDN_QK = GDN_HEADS * HEAD_DIM
GDN_V = GDN_HEADS * HEAD_DIM
GDN_CONV_DIM = 2 * GDN_QK + GDN_V
MOBA_HEADS = 4
MOBA_BLOCK = 256
MOBA_TOPK = 3
MOBA_W = MOBA_HEADS * HEAD_DIM
MEM_HEADS = 4
MEM_W = MEM_HEADS * HEAD_DIM
N_BUCKETS = 32
MAX_DISTANCE = 128
N_EXPERTS = 32
TOP_K = 4
D_FF = D_MODEL
SWIGLU_LIMIT = 7.0
SWIGLU_ALPHA = 1.702
EPS = 1e-6
NEG_INF = -1e30

A_W = GDN_CONV_DIM + GDN_V
B_MQ, B_MK, B_CQ, B_MV, B_BA = 0, MOBA_W, 2 * MOBA_W, 3 * MOBA_W, 4 * MOBA_W
B_W = B_BA + LANES
B_NORM_GROUPS = 3 * MOBA_HEADS


def _params(*sem):
    return pltpu.CompilerParams(dimension_semantics=sem, vmem_limit_bytes=VMEM_LIMIT)


def _dot(a, b, **kw):
    return jnp.dot(a, b, preferred_element_type=f32, **kw)


def _dot_nt(a, b, **kw):
    return lax.dot_general(a, b, (((1,), (1,)), ((), ())), preferred_element_type=f32, **kw)


def _dot_tn(a, b, **kw):
    return lax.dot_general(a, b, (((0,), (0,)), ((), ())), preferred_element_type=f32, **kw)


def _rms(x, gain):
    return x * lax.rsqrt(jnp.mean(x * x, axis=-1, keepdims=True) + EPS) * gain


def _silu(x):
    return x * jax.nn.sigmoid(x)


def _stacked_specs(arrays, tm, grid_rank):
    counts = [a.shape[0] // tm for a in arrays]
    starts = [sum(counts[:k]) for k in range(len(arrays))]

    def spec(k):
        def index_map(i, *_):
            return (jnp.clip(i - starts[k], 0, counts[k] - 1), 0)
        return pl.BlockSpec((tm, arrays[k].shape[1]), index_map)

    return [spec(k) for k in range(len(arrays))], starts, sum(counts)


def _stacked_tile(refs, starts, i):
    x = refs[0][...]
    for ref, start in zip(refs[1:], starts[1:]):
        x = jnp.where(i >= start, ref[...], x)
    return x


def _norm_matmul_kernel(*refs, n_norm, starts):
    n_x = len(starts)
    x_refs = refs[:n_x]
    g_ref, w_ref, gain_ref, o_ref, h_ref = refs[n_x:]

    @pl.when(pl.program_id(1) == 0)
    def _():
        h_ref[...] = _rms(_stacked_tile(x_refs, starts, pl.program_id(0)), g_ref[...]).astype(bf16)

    acc = _dot(h_ref[...], w_ref[...])
    if n_norm == 0:
        o_ref[...] = acc
    else:
        for gi in range(acc.shape[1] // LANES):
            sl = slice(gi * LANES, (gi + 1) * LANES)
            blk = acc[:, sl]
            if gi < n_norm:
                blk = _rms(blk, gain_ref[:, sl])
            o_ref[:, sl] = blk


def _norm_matmul(xs, g, w, gain, *, tm, tn, n_norm):
    d = xs[0].shape[1]
    wn = w.shape[1]
    x_specs, starts, n_tiles = _stacked_specs(xs, tm, 2)
    return pl.pallas_call(
        functools.partial(_norm_matmul_kernel, n_norm=n_norm, starts=starts),
        grid=(n_tiles, wn // tn),
        in_specs=x_specs + [
            pl.BlockSpec((1, d), lambda i, j: (0, 0)),
            pl.BlockSpec((d, tn), lambda i, j: (0, j)),
            pl.BlockSpec((1, gain.shape[1]), lambda i, j: (0, 0)),
        ],
        out_specs=pl.BlockSpec((tm, tn), lambda i, j: (i, j)),
        out_shape=jax.ShapeDtypeStruct((n_tiles * tm, wn), f32),
        scratch_shapes=[pltpu.VMEM((tm, d), bf16)],
        compiler_params=_params("parallel", "arbitrary"),
        name="norm_matmul",
    )(*xs, g, w, gain)


INV_BLOCK = 16


def _split_bf16(x):
    hi = x.astype(bf16)
    return hi, (x - hi.astype(f32)).astype(bf16)


def _dot3(a, b):
    a_hi, a_lo = a
    b_hi, b_lo = b
    return _dot(a_hi, b_hi) + (_dot(a_hi, b_lo) + _dot(a_lo, b_hi))


def _unit_lower_inverses(lmats):
    c = lmats[0].shape[0]
    row = lax.broadcasted_iota(i32, (c, c), 0)
    col = lax.broadcasted_iota(i32, (c, c), 1)
    same = lambda n: (row // n) == (col // n)
    eye = jnp.where(row == col, 1.0, 0.0).astype(f32)
    ms = [jnp.where(same(INV_BLOCK), l, 0.0) for l in lmats]
    ps = [eye - m for m in ms]
    power = 1
    while 2 * power < INV_BLOCK:
        sm = [_split_bf16(m) for m in ms]
        ms = [_dot3(s, s) for s in sm]
        sm = [_split_bf16(m) for m in ms]
        ps = [p + _dot3(_split_bf16(p), s) for p, s in zip(ps, sm)]
        power *= 2
    n = 2 * INV_BLOCK
    while n <= c:
        offs = [_split_bf16(jnp.where(same(n) & ~same(n // 2), l, 0.0)) for l in lmats]
        sp = [_split_bf16(p) for p in ps]
        mids = [_dot3(o, s) for o, s in zip(offs, sp)]
        ps = [p - _dot3(s, _split_bf16(m)) for p, s, m in zip(ps, sp, mids)]
        n *= 2
    return ps


def _gdn_prompt_kernel(qkv_ref, z_ref, ba_ref, convw_ref, alog_ref, dtb_ref, onorm_ref,
                       o_ref, s_out_ref, xp_ref, s_ref):
    c = GDN_CHUNK
    step = pl.program_id(0)

    @pl.when(step == 0)
    def _():
        xp_ref[0:SUBLANES, :] = jnp.zeros((SUBLANES, GDN_CONV_DIM), f32)
        s_ref[...] = jnp.zeros_like(s_ref)

    xp_ref[SUBLANES:SUBLANES + c, :] = qkv_ref[...]
    w = convw_ref[...]
    y = None
    for j in range(GDN_CONV):
        lo = SUBLANES - (GDN_CONV - 1) + j
        term = xp_ref[lo:lo + c, :] * w[j:j + 1, :]
        y = term if y is None else y + term
    y = _silu(y)
    xp_ref[0:SUBLANES, :] = xp_ref[c:c + SUBLANES, :]

    ba = ba_ref[...]
    beta_all = jax.nn.sigmoid(ba)
    g_all = -jnp.exp(alog_ref[...]) * jax.nn.softplus(ba + dtb_ref[...])
    row = lax.broadcasted_iota(i32, (c, c), 0)
    col = lax.broadcasted_iota(i32, (c, c), 1)
    incl = row >= col
    strict = row > col
    gcum_all = _dot(jnp.where(incl, 1.0, 0.0).astype(f32), g_all, precision=HIGHEST)
    gcum_t = gcum_all.T
    onorm = onorm_ref[...]

    heads = range(GDN_HEADS)
    head_cols = lambda off, h: slice(off + h * HEAD_DIM, off + (h + 1) * HEAD_DIM)
    qs = [y[:, head_cols(0, h)] for h in heads]
    ks = [y[:, head_cols(GDN_QK, h)] for h in heads]
    vs = [y[:, head_cols(2 * GDN_QK, h)] for h in heads]
    qs = [q * lax.rsqrt(jnp.sum(q * q, axis=-1, keepdims=True) + EPS) * (HEAD_DIM ** -0.5) for q in qs]
    ks = [k * lax.rsqrt(jnp.sum(k * k, axis=-1, keepdims=True) + EPS) for k in ks]
    betas = [beta_all[:, h:h + 1] for h in heads]
    gcs = [gcum_all[:, GDN_HEADS + h:GDN_HEADS + h + 1] for h in heads]
    grs = [gcum_t[GDN_HEADS + h:GDN_HEADS + h + 1, :] for h in heads]
    g_lasts = [gcum_all[c - 1:c, GDN_HEADS + h:GDN_HEADS + h + 1] for h in heads]
    decays = [jnp.exp(jnp.where(incl, gc - gr, -jnp.inf)) for gc, gr in zip(gcs, grs)]
    kbetas = [k * b for k, b in zip(ks, betas)]
    lmats = [_dot_nt(kb, k) * jnp.where(strict, dc, 0.0) for kb, k, dc in zip(kbetas, ks, decays)]
    tinvs = _unit_lower_inverses(lmats)
    egcs = [jnp.exp(gc) for gc in gcs]
    us = [_dot(t, v * b) for t, v, b in zip(tinvs, vs, betas)]
    ws = [_dot(t, kb * e) for t, kb, e in zip(tinvs, kbetas, egcs)]
    intras = [_dot_nt(q, k) * dc for q, k, dc in zip(qs, ks, decays)]
    kdecs = [k * jnp.exp(gl - gc) for k, gl, gc in zip(ks, g_lasts, gcs)]
    ss = [s_ref[h] for h in heads]
    v_news = [u - _dot(w_, s) for u, w_, s in zip(us, ws, ss)]
    os_ = [_dot(q * e, s) + _dot(it, vn) for q, e, s, it, vn in zip(qs, egcs, ss, intras, v_news)]
    for h in heads:
        s_ref[h] = ss[h] * jnp.exp(g_lasts[h]) + _dot_tn(kdecs[h], v_news[h])
    for h in heads:
        hs = head_cols(0, h)
        o_ref[:, hs] = (_rms(os_[h], onorm) * _silu(z_ref[:, hs])).astype(o_ref.dtype)

    @pl.when(step == pl.num_programs(0) - 1)
    def _():
        s_out_ref[...] = s_ref[...]


def _gdn_prompt(proj_a, proj_b, conv_w, alog_row, dtb_row, onorm, *, n_tok):
    c = GDN_CHUNK
    zblk = GDN_CONV_DIM // GDN_V
    return pl.pallas_call(
        _gdn_prompt_kernel,
        grid=(n_tok // c,),
        in_specs=[
            pl.BlockSpec((c, GDN_CONV_DIM), lambda i: (i, 0)),
            pl.BlockSpec((c, GDN_V), lambda i: (i, zblk)),
            pl.BlockSpec((c, LANES), lambda i: (i, B_BA // LANES)),
            pl.BlockSpec((GDN_CONV, GDN_CONV_DIM), lambda i: (0, 0)),
            pl.BlockSpec((1, LANES), lambda i: (0, 0)),
            pl.BlockSpec((1, LANES), lambda i: (0, 0)),
            pl.BlockSpec((1, HEAD_DIM), lambda i: (0, 0)),
        ],
        out_specs=[
            pl.BlockSpec((c, GDN_V), lambda i: (i, 0)),
            pl.BlockSpec((GDN_HEADS, HEAD_DIM, HEAD_DIM), lambda i: (0, 0, 0)),
        ],
        out_shape=[
            jax.ShapeDtypeStruct((n_tok, GDN_V), bf16),
            jax.ShapeDtypeStruct((GDN_HEADS, HEAD_DIM, HEAD_DIM), f32),
        ],
        scratch_shapes=[
            pltpu.VMEM((c + SUBLANES, GDN_CONV_DIM), f32),
            pltpu.VMEM((GDN_HEADS, HEAD_DIM, HEAD_DIM), f32),
        ],
        compiler_params=_params("arbitrary"),
        name="gdn_prompt",
    )(proj_a, proj_a, proj_b, conv_w, alog_row, dtb_row, onorm)


GDN_S_SEQ = SUBLANES


def _gdn_sample_kernel(q_ref, k_ref, v_ref, z_ref, ba_ref, bq_ref, bk_ref, bv_ref, wq_ref, wk_ref, wv_ref,
                       alog_ref, dtb_ref, onorm_ref, s_ref, o_ref, so_ref, *, n_tok):
    h = pl.program_id(1)
    nb = GDN_S_SEQ
    lane = lax.broadcasted_iota(i32, (nb, LANES), 1)

    def conv(x_ref, buf_ref, w_ref, t):
        y = None
        for j in range(GDN_CONV):
            i = t + j
            row = buf_ref[i] if i < GDN_CONV - 1 else x_ref[i - (GDN_CONV - 1)]
            term = row * w_ref[j:j + 1, :]
            y = term if y is None else y + term
        return _silu(y)

    def lane_col(x, idx):
        return jnp.sum(jnp.where(lane == idx, x, 0.0), axis=1, keepdims=True)

    qs, ks, vs, betas, egs = [], [], [], [], []
    for t in range(n_tok):
        q = conv(q_ref, bq_ref, wq_ref, t)
        k = conv(k_ref, bk_ref, wk_ref, t)
        q = q * lax.rsqrt(jnp.sum(q * q, axis=-1, keepdims=True) + EPS) * (HEAD_DIM ** -0.5)
        k = k * lax.rsqrt(jnp.sum(k * k, axis=-1, keepdims=True) + EPS)
        qs.append(q.T)
        ks.append(k.T)
        vs.append(conv(v_ref, bv_ref, wv_ref, t))
        ba = ba_ref[t]
        betas.append(lane_col(jax.nn.sigmoid(ba), h))
        g = -jnp.exp(alog_ref[...]) * jax.nn.softplus(ba + dtb_ref[...])
        egs.append(jnp.exp(lane_col(g, GDN_HEADS + h)))

    o_rows = [[None] * nb for _ in range(n_tok)]
    for b in range(nb):
        s = s_ref[b]
        for t in range(n_tok):
            kc = ks[t][:, b:b + 1]
            qc = qs[t][:, b:b + 1]
            s = s * egs[t][b:b + 1, :]
            ks_row = jnp.sum(kc * s, axis=0, keepdims=True)
            delta = (vs[t][b:b + 1, :] - ks_row) * betas[t][b:b + 1, :]
            s = s + kc * delta
            o_rows[t][b] = jnp.sum(qc * s, axis=0, keepdims=True)
        so_ref[b] = s
    for t in range(n_tok):
        o = jnp.concatenate(o_rows[t], axis=0)
        o_ref[t] = _rms(o, onorm_ref[...]) * _silu(z_ref[t])


def _gdn_sample(xa, ba, conv_buf, conv_w, alog_row, dtb_row, onorm, state):
    n_tok, n_seq, _ = xa.shape
    nb = GDN_S_SEQ
    hq, hk, hv, hz = 0, GDN_HEADS, 2 * GDN_HEADS, 3 * GDN_HEADS

    def xspec(off, rows):
        return pl.BlockSpec((rows, nb, HEAD_DIM), lambda i, h: (0, i, off + h))

    def wspec(off):
        return pl.BlockSpec((GDN_CONV, HEAD_DIM), lambda i, h: (0, off + h))

    row = pl.BlockSpec((1, LANES), lambda i, h: (0, 0))
    sspec = pl.BlockSpec((nb, None, HEAD_DIM, HEAD_DIM), lambda i, h: (i, h, 0, 0))
    return pl.pallas_call(
        functools.partial(_gdn_sample_kernel, n_tok=n_tok),
        grid=(n_seq // nb, GDN_HEADS),
        in_specs=[xspec(hq, n_tok), xspec(hk, n_tok), xspec(hv, n_tok), xspec(hz, n_tok),
                  pl.BlockSpec((n_tok, nb, LANES), lambda i, h: (0, i, 0)),
                  xspec(hq, GDN_CONV - 1), xspec(hk, GDN_CONV - 1), xspec(hv, GDN_CONV - 1),
                  wspec(hq), wspec(hk), wspec(hv), row, row, row, sspec],
        out_specs=[pl.BlockSpec((n_tok, nb, HEAD_DIM), lambda i, h: (0, i, h)), sspec],
        out_shape=[jax.ShapeDtypeStruct((n_tok, n_seq, GDN_V), f32),
                   jax.ShapeDtypeStruct(state.shape, f32)],
        compiler_params=_params("parallel", "arbitrary"),
        name="gdn_sample",
    )(xa, xa, xa, xa, ba, conv_buf, conv_buf, conv_buf, conv_w, conv_w, conv_w, alog_row, dtb_row, onorm, state)


def _bucket_thresholds():
    exact = N_BUCKETS // 2
    d = np.arange(0, 2 * MAX_DISTANCE)
    val = np.log(np.maximum(d, 1).astype(np.float64) / exact) / math.log(MAX_DISTANCE / exact) * (N_BUCKETS - exact)
    frac = np.abs(val - np.round(val))[exact + 1:MAX_DISTANCE]
    assert frac.min() > 1e-3, "a bucket boundary sits on an integer distance"
    bucket = np.where(d < exact, d, np.minimum(exact + val.astype(np.int64), N_BUCKETS - 1))
    assert np.all(np.diff(bucket) >= 0)
    return [int(np.argmax(bucket >= b)) for b in range(N_BUCKETS)]


_BUCKET_THR = _bucket_thresholds()


def _bias_from_dist(dist, rb_ref, h):
    v = jnp.full(dist.shape, rb_ref[0, h], f32)
    for b in range(1, N_BUCKETS):
        v = jnp.where(dist >= _BUCKET_THR[b], rb_ref[b, h], v)
    return v


def _topk_mask(s, blk, k):
    nblk = s.shape[1]
    sel = jnp.zeros(s.shape, f32)
    for _ in range(k):
        m = jnp.max(s, axis=1, keepdims=True)
        cand = jnp.where((s == m) & (m > -jnp.inf), blk, nblk)
        pick = blk == jnp.min(cand, axis=1, keepdims=True)
        sel = jnp.where(pick, 1.0, sel)
        s = jnp.where(pick, -jnp.inf, s)
    return sel


def _block_mean_kernel(k_ref, o_ref):
    o_ref[0] = jnp.mean(k_ref[...], axis=0, keepdims=True)


def _block_mean(proj_b, n_blk):
    return pl.pallas_call(
        _block_mean_kernel,
        grid=(n_blk,),
        in_specs=[pl.BlockSpec((MOBA_BLOCK, MOBA_W), lambda i: (i, B_MK // MOBA_W))],
        out_specs=pl.BlockSpec((1, 1, MOBA_W), lambda i: (i, 0, 0)),
        out_shape=jax.ShapeDtypeStruct((n_blk, 1, MOBA_W), f32),
        compiler_params=_params("parallel"),
        name="moba_block_mean",
    )(proj_b)


def _moba_select_kernel(q_ref, km_ref, o_ref):
    own = pl.program_id(0)
    nblk = km_ref.shape[0]
    blk = lax.broadcasted_iota(i32, (MOBA_BLOCK, nblk), 1)
    outs = []
    for h in range(MOBA_HEADS):
        hs = slice(h * HEAD_DIM, (h + 1) * HEAD_DIM)
        s = _dot_nt(q_ref[:, hs], km_ref[:, hs], precision=HIGHEST)
        s = jnp.where(blk < own, s, -jnp.inf)
        outs.append(_topk_mask(s, blk, MOBA_TOPK))
    pad = LANES - MOBA_HEADS * nblk
    if pad:
        outs.append(jnp.zeros((MOBA_BLOCK, pad), f32))
    o_ref[...] = jnp.concatenate(outs, axis=1)


def _moba_select(proj_b, kmean, n_blk):
    assert n_blk * MOBA_HEADS <= LANES
    return pl.pallas_call(
        _moba_select_kernel,
        grid=(n_blk,),
        in_specs=[pl.BlockSpec((MOBA_BLOCK, MOBA_W), lambda i: (i, B_MQ // MOBA_W)),
                  pl.BlockSpec((n_blk, MOBA_W), lambda i: (0, 0))],
        out_specs=pl.BlockSpec((MOBA_BLOCK, LANES), lambda i: (i, 0)),
        out_shape=jax.ShapeDtypeStruct((n_blk * MOBA_BLOCK, LANES), f32),
        compiler_params=_params("parallel"),
        name="moba_select",
    )(proj_b, kmean)


def _moba_prompt_kernel(qi_ref, kj_ref, rb_ref, q_ref, k_ref, v_ref, sel_ref, o_ref,
                        bias_ref, m_ref, l_ref, acc_ref, *, n_blk):
    step = pl.program_id(0)
    qi = qi_ref[step]
    kj = kj_ref[step]
    nq = MOBA_BLOCK

    @pl.when(step == 0)
    def _():
        r = lax.broadcasted_iota(i32, (nq, nq), 0)
        c = lax.broadcasted_iota(i32, (nq, nq), 1)
        for h in range(MOBA_HEADS):
            bias_ref[h, 0] = jnp.where(c <= r, _bias_from_dist(r - c, rb_ref, h), NEG_INF)
            bias_ref[h, 1] = _bias_from_dist(r - c + nq, rb_ref, h)

    first = kj == qi

    @pl.when(first)
    def _():
        m_ref[...] = jnp.full(m_ref.shape, NEG_INF, f32)
        l_ref[...] = jnp.zeros_like(l_ref)
        acc_ref[...] = jnp.zeros_like(acc_ref)

    heads = range(MOBA_HEADS)
    hcols = [slice(h * HEAD_DIM, (h + 1) * HEAD_DIM) for h in heads]
    scale = HEAD_DIM ** -0.5

    def attend(far):
        sel = sel_ref[...]
        lane = lax.broadcasted_iota(i32, sel.shape, 1)
        ss = [_dot_nt(q_ref[:, hs].astype(bf16), k_ref[:, hs].astype(bf16)) for hs in hcols]
        picked = [jnp.max(jnp.where(lane == h * n_blk + kj, sel, 0.0), axis=1, keepdims=True) > 0.0 for h in heads]
        if far:
            cols = [jnp.where(pk, rb_ref[N_BUCKETS - 1, h], NEG_INF) for h, pk in zip(heads, picked)]
            ss = [s * scale + c for s, c in zip(ss, cols)]
        else:
            slot = jnp.where(first, 0, 1)
            ss = [s * scale + bias_ref[h, slot] for h, s in zip(heads, ss)]
            ss = [jnp.where(pk | first, s, NEG_INF) for pk, s in zip(picked, ss)]
        m_prevs = [m_ref[h] for h in heads]
        m_news = [jnp.maximum(mp, jnp.max(s, axis=1, keepdims=True)) for mp, s in zip(m_prevs, ss)]
        alphas = [jnp.exp(mp - mn) for mp, mn in zip(m_prevs, m_news)]
        ps = [jnp.exp(s - mn) for s, mn in zip(ss, m_news)]
        pvs = [_dot(p.astype(bf16), v_ref[:, hs].astype(bf16)) for p, hs in zip(ps, hcols)]
        for h in heads:
            l_ref[h] = alphas[h] * l_ref[h] + jnp.sum(ps[h], axis=1, keepdims=True)
            acc_ref[h] = alphas[h] * acc_ref[h] + pvs[h]
            m_ref[h] = m_news[h]

    is_far = kj < qi - 1
    pl.when(is_far)(lambda: attend(True))
    pl.when(jnp.logical_not(is_far))(lambda: attend(False))

    @pl.when(kj == 0)
    def _():
        for h in range(MOBA_HEADS):
            hs = slice(h * HEAD_DIM, (h + 1) * HEAD_DIM)
            o_ref[:, hs] = (acc_ref[h] / l_ref[h]).astype(o_ref.dtype)


def _moba_prompt(proj_b, sel, rel_bias, n_blk):
    qi = np.concatenate([np.full(i + 1, i) for i in range(n_blk)]).astype(np.int32)
    kj = np.concatenate([np.arange(i, -1, -1) for i in range(n_blk)]).astype(np.int32)
    nq = MOBA_BLOCK
    grid_spec = pltpu.PrefetchScalarGridSpec(
        num_scalar_prefetch=2,
        grid=(len(qi),),
        in_specs=[
            pl.BlockSpec(memory_space=pltpu.SMEM),
            pl.BlockSpec((nq, MOBA_W), lambda s, qi, kj: (qi[s], B_MQ // MOBA_W)),
            pl.BlockSpec((nq, MOBA_W), lambda s, qi, kj: (kj[s], B_MK // MOBA_W)),
            pl.BlockSpec((nq, MOBA_W), lambda s, qi, kj: (kj[s], B_MV // MOBA_W)),
            pl.BlockSpec((nq, LANES), lambda s, qi, kj: (qi[s], 0)),
        ],
        out_specs=pl.BlockSpec((nq, MOBA_W), lambda s, qi, kj: (qi[s], 0)),
        scratch_shapes=[
            pltpu.VMEM((MOBA_HEADS, 2, nq, nq), f32),
            pltpu.VMEM((MOBA_HEADS, nq, 1), f32),
            pltpu.VMEM((MOBA_HEADS, nq, 1), f32),
            pltpu.VMEM((MOBA_HEADS, nq, HEAD_DIM), f32),
        ],
    )
    return pl.pallas_call(
        functools.partial(_moba_prompt_kernel, n_blk=n_blk),
        grid_spec=grid_spec,
        out_shape=jax.ShapeDtypeStruct((n_blk * nq, MOBA_W), bf16),
        compiler_params=_params("arbitrary"),
        name="moba_prompt",
    )(jnp.asarray(qi), jnp.asarray(kj), rel_bias, proj_b, proj_b, proj_b, sel)


def _per_head_bias(dist, row_head, rb_ref):
    v = _bias_from_dist(dist, rb_ref, 0)
    for h in range(1, MOBA_HEADS):
        v = jnp.where(row_head == h, _bias_from_dist(dist, rb_ref, h), v)
    return v


def _moba_sample_kernel(pt_ref, rb_ref, q_ref, kn_ref, vn_ref, *rest, n_pages, page):
    kp = rest[:n_pages]
    vp = rest[n_pages:2 * n_pages]
    o_ref, bias_ref, biasn_ref = rest[2 * n_pages:]
    nh = MOBA_HEADS
    rows = q_ref.shape[0]
    prow = page * nh
    past = n_pages * page
    n_blk = past // MOBA_BLOCK
    ppb = MOBA_BLOCK // page

    @pl.when(pl.program_id(0) == 0)
    def _():
        r = lax.broadcasted_iota(i32, (rows, past * nh), 0)
        c = lax.broadcasted_iota(i32, (rows, past * nh), 1)
        bias = _per_head_bias(past + r // nh - c // nh, r % nh, rb_ref)
        bias_ref[...] = jnp.where(r % nh == c % nh, bias, NEG_INF)
        r = lax.broadcasted_iota(i32, (rows, rows), 0)
        c = lax.broadcasted_iota(i32, (rows, rows), 1)
        bias = _per_head_bias(r // nh - c // nh, r % nh, rb_ref)
        biasn_ref[...] = jnp.where((r % nh == c % nh) & (c // nh <= r // nh), bias, NEG_INF)

    kb, means = [], []
    for p in range(n_pages):
        kpage = kp[p][...]
        kb.append(kpage.astype(bf16))
        part = jnp.sum(kpage.reshape(prow // SUBLANES, SUBLANES, HEAD_DIM), axis=0)
        part = part[0:nh] + part[nh:2 * nh]
        if p % ppb == 0:
            means.append(part)
        else:
            means[-1] = means[-1] + part
    kmean = jnp.concatenate(means, axis=0) / MOBA_BLOCK

    q = q_ref[...]
    r = lax.broadcasted_iota(i32, (rows, n_blk * nh), 0)
    c = lax.broadcasted_iota(i32, (rows, n_blk * nh), 1)
    s = jnp.where(r % nh == c % nh, _dot_nt(q, kmean, precision=HIGHEST), -jnp.inf)
    sel = _topk_mask(s, c, MOBA_TOPK)
    picked = [jnp.max(jnp.where(c // nh == n, sel, 0.0), axis=1, keepdims=True) > 0.0 for n in range(n_blk)]

    qb = q.astype(bf16)
    scale = HEAD_DIM ** -0.5
    ln = _dot_nt(qb, kn_ref[...].astype(bf16)) * scale + biasn_ref[...]
    m = jnp.max(ln, axis=1, keepdims=True)
    lps = []
    for p in range(n_pages):
        lp = _dot_nt(qb, kb[p]) * scale + bias_ref[:, p * prow:(p + 1) * prow]
        lp = jnp.where(picked[p // ppb], lp, NEG_INF)
        m = jnp.maximum(m, jnp.max(lp, axis=1, keepdims=True))
        lps.append(lp)
    pn = jnp.exp(ln - m)
    den = jnp.sum(pn, axis=1, keepdims=True)
    num = _dot(pn.astype(bf16), vn_ref[...].astype(bf16))
    for p in range(n_pages):
        pp = jnp.exp(lps[p] - m)
        den = den + jnp.sum(pp, axis=1, keepdims=True)
        num = num + _dot(pp.astype(bf16), vp[p][...].astype(bf16))
    o_ref[...] = num / den


def _moba_sample(q, k_new, v_new, pool_k, pool_v, page_table, rel_bias, *, page):
    n_seq, rows, _ = q.shape
    n_pages = page_table.shape[1]
    prow = page * MOBA_HEADS
    new_spec = pl.BlockSpec((None, rows, HEAD_DIM), lambda b, pt: (b, 0, 0))

    def page_spec(p):
        return pl.BlockSpec((prow, HEAD_DIM), lambda b, pt: (pt[b, p], 0))

    grid_spec = pltpu.PrefetchScalarGridSpec(
        num_scalar_prefetch=1,
        grid=(n_seq,),
        in_specs=[pl.BlockSpec(memory_space=pltpu.SMEM), new_spec, new_spec, new_spec]
        + [page_spec(p) for p in range(n_pages)] * 2,
        out_specs=new_spec,
        scratch_shapes=[pltpu.VMEM((rows, n_pages * prow), f32), pltpu.VMEM((rows, rows), f32)],
    )
    return pl.pallas_call(
        functools.partial(_moba_sample_kernel, n_pages=n_pages, page=page),
        grid_spec=grid_spec,
        out_shape=jax.ShapeDtypeStruct((n_seq, rows, HEAD_DIM), f32),
        compiler_params=_params("arbitrary"),
        name="moba_sample",
    )(page_table, rel_bias, q, k_new, v_new, *([pool_k] * n_pages), *([pool_v] * n_pages))


def _mem_attn_kernel(q_ref, k_ref, v_ref, o_ref):
    for h in range(MEM_HEADS):
        hs = slice(h * HEAD_DIM, (h + 1) * HEAD_DIM)
        s = _dot_nt(q_ref[:, hs].astype(bf16), k_ref[:, hs].astype(bf16)) * (HEAD_DIM ** -0.5)
        p = jnp.exp(s - jnp.max(s, axis=1, keepdims=True))
        num = _dot(p.astype(bf16), v_ref[:, hs].astype(bf16))
        o_ref[:, hs] = (num / jnp.sum(p, axis=1, keepdims=True)).astype(o_ref.dtype)


def _mem_attn_prompt(proj_b, mem_kv, *, n_tok, tq):
    n_mem = mem_kv.shape[0]
    return pl.pallas_call(
        _mem_attn_kernel,
        grid=(n_tok // tq,),
        in_specs=[pl.BlockSpec((tq, MEM_W), lambda i: (i, B_CQ // MEM_W)),
                  pl.BlockSpec((n_mem, MEM_W), lambda i: (0, 0)),
                  pl.BlockSpec((n_mem, MEM_W), lambda i: (0, 1))],
        out_specs=pl.BlockSpec((tq, MEM_W), lambda i: (i, 0)),
        out_shape=jax.ShapeDtypeStruct((n_tok, MEM_W), bf16),
        compiler_params=_params("parallel"),
        name="mem_attn_prompt",
    )(proj_b, mem_kv, mem_kv)


def _mem_attn_sample_kernel(q_ref, k_ref, v_ref, o_ref):
    s = _dot_nt(q_ref[...].astype(bf16), k_ref[...].astype(bf16)) * (HEAD_DIM ** -0.5)
    r = lax.broadcasted_iota(i32, s.shape, 0)
    c = lax.broadcasted_iota(i32, s.shape, 1)
    s = jnp.where(r % MEM_HEADS == c % MEM_HEADS, s, NEG_INF)
    p = jnp.exp(s - jnp.max(s, axis=1, keepdims=True))
    o_ref[...] = _dot(p.astype(bf16), v_ref[...].astype(bf16)) / jnp.sum(p, axis=1, keepdims=True)


def _mem_attn_sample(q, mem_k, mem_v):
    n_seq, rows, _ = q.shape
    mrows = mem_k.shape[1]
    return pl.pallas_call(
        _mem_attn_sample_kernel,
        grid=(n_seq,),
        in_specs=[pl.BlockSpec((None, rows, HEAD_DIM), lambda b: (b, 0, 0)),
                  pl.BlockSpec((None, mrows, HEAD_DIM), lambda b: (b, 0, 0)),
                  pl.BlockSpec((None, mrows, HEAD_DIM), lambda b: (b, 0, 0))],
        out_specs=pl.BlockSpec((None, rows, HEAD_DIM), lambda b: (b, 0, 0)),
        out_shape=jax.ShapeDtypeStruct((n_seq, rows, HEAD_DIM), f32),
        compiler_params=_params("parallel"),
        name="mem_attn_sample",
    )(q, mem_k, mem_v)


def _outproj_router_kernel(*refs, starts):
    n_g = len(starts)
    i = pl.program_id(0)
    x, og, om, oc = [_stacked_tile(refs[k * n_g:(k + 1) * n_g], starts, i) for k in range(4)]
    w_ref, g_ref, rw_ref, rb_ref, y_ref, h_ref, ti_ref, tw_ref = refs[4 * n_g:]
    mix = (_dot(og, w_ref[0:GDN_V, :])
           + _dot(om, w_ref[GDN_V:GDN_V + MOBA_W, :])
           + _dot(oc, w_ref[GDN_V + MOBA_W:, :]))
    y = x + mix
    y_ref[...] = y
    h = _rms(y, g_ref[...])
    _store_slabs(h_ref, h)
    h_hi, h_lo = _split_bf16(h)
    w_hi, w_lo = _split_bf16(rw_ref[...])
    logits = _dot(h_hi, w_hi) + (_dot(h_hi, w_lo) + _dot(h_lo, w_hi)) + rb_ref[...]
    lane = lax.broadcasted_iota(i32, logits.shape, 1)
    s = jnp.where(lane < N_EXPERTS, logits, -jnp.inf)
    vals, idxs = [], []
    for _ in range(TOP_K):
        m = jnp.max(s, axis=1, keepdims=True)
        idx = jnp.min(jnp.where(s == m, lane, LANES), axis=1, keepdims=True)
        vals.append(m)
        idxs.append(idx)
        s = jnp.where(lane == idx, -jnp.inf, s)
    exps = [jnp.exp(v - vals[0]) for v in vals]
    den = exps[0]
    for e in exps[1:]:
        den = den + e
    ti = jnp.zeros(logits.shape, i32)
    tw = jnp.zeros(logits.shape, f32)
    for r in range(TOP_K):
        ti = jnp.where(lane == r, idxs[r], ti)
        tw = jnp.where(lane == r, exps[r] / den, tw)
    ti_ref[...] = ti
    tw_ref[...] = tw


def _outproj_router(xs, ogs, oms, ocs, w_out, g, rw, rb, *, tm):
    d = xs[0].shape[1]
    row = lambda w: pl.BlockSpec((tm, w), lambda i: (i, 0))
    full = lambda a: pl.BlockSpec(a.shape, lambda i: (0,) * a.ndim)
    specs, starts, n_tiles = [], None, None
    for group in (xs, ogs, oms, ocs):
        group_specs, starts, n_tiles = _stacked_specs(group, tm, 1)
        specs += group_specs
    n = n_tiles * tm
    return pl.pallas_call(
        functools.partial(_outproj_router_kernel, starts=starts),
        grid=(n_tiles,),
        in_specs=specs + [full(w_out), full(g), full(rw), full(rb)],
        out_specs=[row(d), pl.BlockSpec((tm * (d // LANES), LANES), lambda i: (i, 0)), row(LANES), row(LANES)],
        out_shape=[jax.ShapeDtypeStruct((n, d), f32), jax.ShapeDtypeStruct((n * (d // LANES), LANES), f32),
                   jax.ShapeDtypeStruct((n, LANES), i32), jax.ShapeDtypeStruct((n, LANES), f32)],
        compiler_params=_params("parallel"),
        name="outproj_router",
    )(*xs, *ogs, *oms, *ocs, w_out, g, rw, rb)


GATHER_UNROLL = 8
SLAB_G = D_MODEL // LANES


def _store_slabs(slab_ref, x, mask=None):
    rows = x.shape[0]
    g = x.shape[1] // LANES
    for c in range(g):
        idx = (pl.ds(c, rows, stride=g), slice(None))
        blk = x[:, c * LANES:(c + 1) * LANES]
        slab_ref[idx] = blk if mask is None else jnp.where(mask, blk, slab_ref[idx])


def _load_slab_group(slab_ref, lead, c, rows, g):
    return slab_ref[lead + (pl.ds(c, rows, stride=g), slice(None))]


def _issue_slab_gather(idx_ref, idx_base, src_ref, dst_ref, sem, count, g):
    assert count % GATHER_UNROLL == 0

    def body(jj, carry):
        for u in range(GATHER_UNROLL):
            j = jj * GATHER_UNROLL + u
            tok = pl.multiple_of(idx_ref[idx_base + j] * g, g)
            dst = dst_ref.at[pl.ds(pl.multiple_of(j * g, g), g)]
            pltpu.make_async_copy(src_ref.at[pl.ds(tok, g)], dst, sem).start(priority=u % 2)
        return carry

    lax.fori_loop(0, count // GATHER_UNROLL, body, 0)


def _wait_slab_gather(src_ref, dst_ref, sem):
    pltpu.make_async_copy(src_ref.at[pl.ds(0, dst_ref.shape[0])], dst_ref, sem).wait()


def _gather_x_kernel(idx_ref, src_ref, o_ref, buf_ref, sem, *, rows, g):
    i = pl.program_id(0)
    slot = i % 2

    @pl.when(i == 0)
    def _():
        _issue_slab_gather(idx_ref, 0, src_ref, buf_ref.at[0], sem.at[0], rows, g)

    @pl.when(i + 1 < pl.num_programs(0))
    def _():
        _issue_slab_gather(idx_ref, (i + 1) * rows, src_ref, buf_ref.at[1 - slot], sem.at[1 - slot], rows, g)

    _wait_slab_gather(src_ref, buf_ref.at[slot], sem.at[slot])
    for c in range(g):
        o_ref[:, c * LANES:(c + 1) * LANES] = _load_slab_group(buf_ref, (slot,), c, rows, g).astype(o_ref.dtype)


def _gather_x(src, idx, *, rows, g):
    n_out = idx.shape[0]
    grid_spec = pltpu.PrefetchScalarGridSpec(
        num_scalar_prefetch=1,
        grid=(n_out // rows,),
        in_specs=[pl.BlockSpec(memory_space=pl.ANY)],
        out_specs=pl.BlockSpec((rows, g * LANES), lambda i, idx: (i, 0)),
        scratch_shapes=[pltpu.VMEM((2, rows * g, LANES), f32), pltpu.SemaphoreType.DMA((2,))],
    )
    return pl.pallas_call(
        functools.partial(_gather_x_kernel, rows=rows, g=g),
        grid_spec=grid_spec,
        out_shape=jax.ShapeDtypeStruct((n_out, g * LANES), bf16),
        compiler_params=_params("arbitrary"),
        name="moe_gather_x",
    )(idx, src)


def _visit_store(o_ref, new, lo, hi):
    @pl.when(lo == 0)
    def _():
        o_ref[...] = new

    @pl.when(lo > 0)
    def _():
        r = lax.broadcasted_iota(i32, new.shape, 0)
        o_ref[...] = jnp.where((r >= lo) & (r < hi), new, o_ref[...])


def _new_expert(ev_ref, v):
    return (v == 0) | (ev_ref[v] != ev_ref[jnp.maximum(v - 1, 0)])


def _expert_weights_step(copies, is_new, run, n_runs, e_here, e_next, e_first, sweep, n_sweeps, on_arrival):
    slot = (run + sweep * n_runs) % 2

    @pl.when(is_new)
    def _():
        @pl.when((sweep == 0) & (run == 0))
        def _():
            for c in copies(e_here, sweep, slot):
                c.start()

        for c in copies(e_here, sweep, slot):
            c.wait()
        on_arrival(slot)
        last = run == n_runs - 1

        @pl.when(jnp.logical_not(last))
        def _():
            for c in copies(e_next, sweep, 1 - slot):
                c.start()

        @pl.when(last & (sweep + 1 < n_sweeps))
        def _():
            for c in copies(e_first, sweep + 1, 1 - slot):
                c.start()


def _moe_gate_up_kernel(tv_ref, ev_ref, lo_ref, hi_ref, run_ref, nxt_ref, nrun_ref, x_ref, w_hbm, bg_ref, bu_ref,
                        o_ref, wf_ref, wb_ref, sem, *, tf):
    f = pl.program_id(0)
    v = pl.program_id(1)
    lo = lo_ref[v]
    hi = hi_ref[v]

    def copies(e, sweep, slot):
        cols = lambda half: pl.ds(pl.multiple_of(half * D_FF + sweep * tf, tf), tf)
        return [pltpu.make_async_copy(w_hbm.at[e, :, cols(half)], wf_ref.at[slot, half], sem.at[slot])
                for half in (0, 1)]

    def on_arrival(slot):
        wb_ref[0] = wf_ref[slot, 0].astype(bf16)
        wb_ref[1] = wf_ref[slot, 1].astype(bf16)

    _expert_weights_step(copies, _new_expert(ev_ref, v), run_ref[v], nrun_ref[0], ev_ref[v], nxt_ref[v], ev_ref[0],
                         f, pl.num_programs(0), on_arrival)

    @pl.when(hi > lo)
    def _():
        x = x_ref[...]
        gate = _dot(x, wb_ref[0]) + bg_ref[...]
        up = _dot(x, wb_ref[1]) + bu_ref[...]
        gate = jnp.minimum(gate, SWIGLU_LIMIT)
        up = jnp.clip(up, -SWIGLU_LIMIT, SWIGLU_LIMIT)
        act = gate * jax.nn.sigmoid(SWIGLU_ALPHA * gate) * (up + 1.0)
        _visit_store(o_ref, act.astype(o_ref.dtype), lo, hi)


def _moe_gate_up(xs, w_gu, b_gu3, visits, *, tm, tf):
    rows, d = xs.shape
    nf = D_FF // tf
    n_pre = len(visits)
    grid_spec = pltpu.PrefetchScalarGridSpec(
        num_scalar_prefetch=n_pre,
        grid=(nf, visits[0].shape[0]),
        in_specs=[
            pl.BlockSpec((tm, d), lambda f, v, tv, *_: (tv[v], 0)),
            pl.BlockSpec(memory_space=pl.ANY),
            pl.BlockSpec((None, 1, tf), lambda f, v, tv, ev, *_: (ev[v], 0, f)),
            pl.BlockSpec((None, 1, tf), lambda f, v, tv, ev, *_: (ev[v], 0, nf + f)),
        ],
        out_specs=pl.BlockSpec((tm, tf), lambda f, v, tv, *_: (tv[v], f)),
        scratch_shapes=[pltpu.VMEM((2, 2, d, tf), f32), pltpu.VMEM((2, d, tf), bf16), pltpu.SemaphoreType.DMA((2,))],
    )
    return pl.pallas_call(
        functools.partial(_moe_gate_up_kernel, tf=tf),
        grid_spec=grid_spec,
        out_shape=jax.ShapeDtypeStruct((rows, D_FF), bf16),
        compiler_params=_params("arbitrary", "arbitrary"),
        name="moe_gate_up",
    )(*visits, xs, w_gu, b_gu3, b_gu3)


def _moe_down_kernel(tv_ref, ev_ref, lo_ref, hi_ref, run_ref, nxt_ref, nrun_ref, a_ref, w_hbm, b_ref,
                     o_ref, wf_ref, wb_ref, sem):
    v = pl.program_id(0)
    lo = lo_ref[v]
    hi = hi_ref[v]

    def copies(e, sweep, slot):
        return [pltpu.make_async_copy(w_hbm.at[e], wf_ref.at[slot], sem.at[slot])]

    def on_arrival(slot):
        wb_ref[...] = wf_ref[slot].astype(bf16)

    _expert_weights_step(copies, _new_expert(ev_ref, v), run_ref[v], nrun_ref[0], ev_ref[v], nxt_ref[v], ev_ref[0],
                         0, 1, on_arrival)

    @pl.when(hi > lo)
    def _():
        y = _dot(a_ref[...], wb_ref[...]) + b_ref[...]

        @pl.when(lo == 0)
        def _():
            _store_slabs(o_ref, y)

        @pl.when(lo > 0)
        def _():
            r = lax.broadcasted_iota(i32, (y.shape[0], LANES), 0)
            _store_slabs(o_ref, y, mask=(r >= lo) & (r < hi))


def _moe_down(act, w_dn, b_dn3, visits, *, tm):
    rows, dff = act.shape
    d = w_dn.shape[2]
    g = d // LANES
    grid_spec = pltpu.PrefetchScalarGridSpec(
        num_scalar_prefetch=len(visits),
        grid=(visits[0].shape[0],),
        in_specs=[
            pl.BlockSpec((tm, dff), lambda v, tv, *_: (tv[v], 0)),
            pl.BlockSpec(memory_space=pl.ANY),
            pl.BlockSpec((None, 1, d), lambda v, tv, ev, *_: (ev[v], 0, 0)),
        ],
        out_specs=pl.BlockSpec((tm * g, LANES), lambda v, tv, *_: (tv[v], 0)),
        scratch_shapes=[pltpu.VMEM((2, dff, d), f32), pltpu.VMEM((dff, d), bf16), pltpu.SemaphoreType.DMA((2,))],
    )
    return pl.pallas_call(
        _moe_down_kernel,
        grid_spec=grid_spec,
        out_shape=jax.ShapeDtypeStruct((rows * g, LANES), f32),
        compiler_params=_params("arbitrary"),
        name="moe_down",
    )(*visits, act, w_dn, b_dn3)


def _moe_combine_kernel(pos_ref, y_ref, tw_ref, ys_ref, oa_ref, ob_ref, buf_ref, sem, *, rows, g, tiles_a):
    i = pl.program_id(0)
    slot = i % 2
    n_tok = pl.num_programs(0) * rows

    def issue(tile, s):
        for k in range(TOP_K):
            _issue_slab_gather(pos_ref, k * n_tok + tile * rows, ys_ref, buf_ref.at[s, k], sem.at[s], rows, g)

    @pl.when(i == 0)
    def _():
        issue(0, 0)

    @pl.when(i + 1 < pl.num_programs(0))
    def _():
        issue(i + 1, 1 - slot)

    for k in range(TOP_K):
        _wait_slab_gather(ys_ref, buf_ref.at[slot, k], sem.at[slot])
    tw = tw_ref[...]

    def combine(o_ref):
        for c in range(g):
            cs = slice(c * LANES, (c + 1) * LANES)
            acc = y_ref[:, cs]
            for k in range(TOP_K):
                acc = acc + _load_slab_group(buf_ref, (slot, k), c, rows, g) * tw[:, k:k + 1]
            o_ref[:, cs] = acc

    pl.when(i < tiles_a)(lambda: combine(oa_ref))
    pl.when(i >= tiles_a)(lambda: combine(ob_ref))


def _moe_combine(y1, ys, pos_kmajor, tw, *, rows, n_a):
    n, d = y1.shape
    g = d // LANES
    tiles_a = n_a // rows
    tiles = n // rows
    grid_spec = pltpu.PrefetchScalarGridSpec(
        num_scalar_prefetch=1,
        grid=(tiles,),
        in_specs=[pl.BlockSpec((rows, d), lambda i, pos: (i, 0)),
                  pl.BlockSpec((rows, LANES), lambda i, pos: (i, 0)),
                  pl.BlockSpec(memory_space=pl.ANY)],
        out_specs=[pl.BlockSpec((rows, d), lambda i, pos: (jnp.minimum(i, tiles_a - 1), 0)),
                   pl.BlockSpec((rows, d), lambda i, pos: (jnp.maximum(i - tiles_a, 0), 0))],
        scratch_shapes=[pltpu.VMEM((2, TOP_K, rows * g, LANES), f32), pltpu.SemaphoreType.DMA((2,))],
    )
    return pl.pallas_call(
        functools.partial(_moe_combine_kernel, rows=rows, g=g, tiles_a=tiles_a),
        grid_spec=grid_spec,
        out_shape=[jax.ShapeDtypeStruct((n_a, d), f32), jax.ShapeDtypeStruct((n - n_a, d), f32)],
        compiler_params=_params("arbitrary"),
        name="moe_combine",
    )(pos_kmajor, y1, tw, ys)


def _sort_rows(top_i):
    e = top_i.reshape(-1)
    a = e.shape[0]
    order = jnp.argsort(e, stable=True).astype(i32)
    tok_sorted = order // TOP_K
    blk = LANES
    onehot = (e[:, None] == jnp.arange(N_EXPERTS, dtype=i32)[None, :]).astype(f32).reshape(a // blk, blk, N_EXPERTS)
    earlier = (jnp.arange(blk)[:, None] > jnp.arange(blk)[None, :]).astype(f32)
    within = jnp.einsum("ij,bjk->bik", earlier, onehot)
    block_sums = jnp.sum(onehot, axis=1)
    block_off = jnp.cumsum(block_sums, axis=0) - block_sums
    counts = jnp.sum(block_sums, axis=0)
    starts = jnp.cumsum(counts) - counts
    pos = jnp.sum(onehot * (within + block_off[:, None, :] + starts[None, None, :]), axis=2).reshape(a)
    return tok_sorted, pos.astype(i32), counts.astype(i32)


def _visits(counts, n_rows, *, tm):
    ends = jnp.cumsum(counts)
    starts = ends - counts
    first_tile = starts // tm
    last_tile = jnp.maximum(ends - 1, 0) // tm
    n_vis = jnp.where(counts > 0, last_tile - first_tile + 1, 0)
    vis_end = jnp.cumsum(n_vis)
    vis_start = vis_end - n_vis
    total = vis_end[-1]
    v = jnp.arange(n_rows // tm + N_EXPERTS, dtype=i32)
    vc = jnp.minimum(v, total - 1)
    ev = jnp.sum((vis_end[None, :] <= vc[:, None]).astype(i32), axis=1)
    tv = first_tile[ev] + (vc - vis_start[ev])
    lo = jnp.maximum(starts[ev], tv * tm) - tv * tm
    hi = jnp.minimum(ends[ev], (tv + 1) * tm) - tv * tm
    valid = v < total
    experts = jnp.arange(N_EXPERTS, dtype=i32)
    present = n_vis > 0
    n_runs = jnp.sum(present.astype(i32))
    run = jnp.sum((present[None, :] & (experts[None, :] < ev[:, None])).astype(i32), axis=1)
    rank = jnp.cumsum(present.astype(i32))
    run_expert = jnp.sum((rank[None, :] <= experts[:, None]).astype(i32), axis=1)
    nxt = run_expert[jnp.minimum(run + 1, n_runs - 1)]
    return tv, ev, jnp.where(valid, lo, 0), jnp.where(valid, hi, 0), run, nxt, n_runs[None]


GATE_UP_TM = 256
GATE_UP_TF = 1024
DOWN_TM = 256
GATHER_ROWS = 256
COMBINE_ROWS = 128


def _moe(y1, h_slabs, top_i, top_w, w_gu, b_gu, w_dn, b_dn, *, n_a):
    n_tok = y1.shape[0]
    tok_sorted, pos, counts = _sort_rows(top_i[:, :TOP_K])
    n_rows = tok_sorted.shape[0]
    xs = _gather_x(h_slabs, tok_sorted, rows=GATHER_ROWS, g=SLAB_G)
    act = _moe_gate_up(xs, w_gu, b_gu[:, None, :], _visits(counts, n_rows, tm=GATE_UP_TM),
                       tm=GATE_UP_TM, tf=GATE_UP_TF)
    ys = _moe_down(act, w_dn, b_dn[:, None, :], _visits(counts, n_rows, tm=DOWN_TM), tm=DOWN_TM)
    pos_kmajor = pos.reshape(n_tok, TOP_K).T.reshape(-1)
    return _moe_combine(y1, ys, pos_kmajor, top_w, rows=COMBINE_ROWS, n_a=n_a)


def kernel(x_prompt, x_sample, cache_moba_k, cache_moba_v, state_gdn, state_gdn_conv, cache_mem_k, cache_mem_v, page_table, mem_prompt, rel_bias, norm_mix, w_in, conv_w, a_log, dt_bias, gdn_o_norm, moba_q_norm, moba_k_norm, mem_q_norm, mem_norm, w_mem_kv, mem_k_norm, w_out, norm_ffn, router_w, router_b, w_gu, b_gu, w_dn, b_dn):
    assert x_prompt.shape[0] == 1 and all(a.shape[0] == 1 for a in (w_in, w_out, w_gu, w_dn, state_gdn))
    n_p = x_prompt.shape[1]
    n_seq, n_st = x_sample.shape[:2]
    n_s = n_seq * n_st
    d = x_prompt.shape[2]
    n_blk = n_p // MOBA_BLOCK
    xs = (x_prompt.reshape(n_p, d), x_sample.reshape(n_s, d))

    w = w_in[0]
    off_beta = A_W
    off_moba = off_beta + 2 * GDN_HEADS
    off_mem = off_moba + 3 * MOBA_W
    w_a = w[:, :A_W].astype(bf16)
    w_b = jnp.concatenate(
        [w[:, off_moba:off_moba + MOBA_W], w[:, off_moba + MOBA_W:off_moba + 2 * MOBA_W], w[:, off_mem:],
         w[:, off_moba + 2 * MOBA_W:off_mem], w[:, off_beta:off_moba],
         jnp.zeros((d, LANES - 2 * GDN_HEADS), w.dtype)], axis=1).astype(bf16)
    gain_b = jnp.concatenate([jnp.tile(moba_q_norm[0], MOBA_HEADS), jnp.tile(moba_k_norm[0], MOBA_HEADS),
                              jnp.tile(mem_q_norm[0], MEM_HEADS)])[None]
    proj_a = _norm_matmul(xs, norm_mix, w_a, gain_b, tm=512, tn=1024, n_norm=0)
    proj_b = _norm_matmul(xs, norm_mix, w_b, gain_b, tm=256, tn=B_W, n_norm=B_NORM_GROUPS)
    proj_b3 = proj_b[n_p:].reshape(n_seq, n_st, B_W)

    pad_row = lambda a: jnp.zeros((1, LANES), f32).at[0, GDN_HEADS:2 * GDN_HEADS].set(a[0])
    alog_row, dtb_row = pad_row(a_log), pad_row(dt_bias)
    o_gdn_p, p_gdn = _gdn_prompt(proj_a, proj_b, conv_w[0], alog_row, dtb_row, gdn_o_norm, n_tok=n_p)
    p_conv = proj_a[n_p - (GDN_CONV - 1):n_p, :GDN_CONV_DIM]
    xa_s = proj_a[n_p:].reshape(n_seq, n_st, A_W)
    o_gdn_s, s_gdn = _gdn_sample(jnp.swapaxes(xa_s, 0, 1), jnp.swapaxes(proj_b3[..., B_BA:], 0, 1),
                                 jnp.swapaxes(state_gdn_conv[0], 0, 1), conv_w[0], alog_row, dtb_row,
                                 gdn_o_norm, state_gdn[0])
    o_gdn_s = jnp.swapaxes(o_gdn_s, 0, 1).reshape(n_s, GDN_V).astype(bf16)
    s_conv = jnp.concatenate([state_gdn_conv[0], xa_s[..., :GDN_CONV_DIM]], axis=1)[:, n_st:]

    kmean = _block_mean(proj_b, n_blk).reshape(n_blk, MOBA_W)
    sel = _moba_select(proj_b, kmean, n_blk)
    o_moba_p = _moba_prompt(proj_b, sel, rel_bias, n_blk)
    pairs = lambda col: proj_b3[..., col:col + MOBA_W].reshape(n_seq, n_st * MOBA_HEADS, HEAD_DIM)
    o_moba_s = _moba_sample(pairs(B_MQ), pairs(B_MK), pairs(B_MV), cache_moba_k.reshape(-1, HEAD_DIM),
                            cache_moba_v.reshape(-1, HEAD_DIM), page_table, rel_bias, page=cache_moba_k.shape[2])
    o_moba_s = o_moba_s.reshape(n_s, MOBA_W).astype(bf16)

    n_mem = mem_prompt.shape[1]
    mem_kv = _norm_matmul((mem_prompt[0],), mem_norm, w_mem_kv[0].astype(bf16),
                          jnp.tile(mem_k_norm[0], MEM_HEADS)[None], tm=n_mem, tn=2 * MEM_W, n_norm=MEM_HEADS)
    o_mem_p = _mem_attn_prompt(proj_b, mem_kv, n_tok=n_p, tq=512)
    o_mem_s = _mem_attn_sample(pairs(B_CQ), cache_mem_k.reshape(n_seq, n_mem * MEM_HEADS, HEAD_DIM),
                               cache_mem_v.reshape(n_seq, n_mem * MEM_HEADS, HEAD_DIM))
    o_mem_s = o_mem_s.reshape(n_s, MEM_W).astype(bf16)

    rw = jnp.pad(router_w[0], ((0, 0), (0, LANES - N_EXPERTS)))
    rb = jnp.pad(router_b, ((0, 0), (0, LANES - N_EXPERTS)))
    y1, h2, top_i, top_w = _outproj_router(
        xs, (o_gdn_p, o_gdn_s), (o_moba_p, o_moba_s), (o_mem_p, o_mem_s),
        w_out[0].astype(bf16), norm_ffn, rw, rb, tm=256)
    y_p, y_s = _moe(y1, h2, top_i, top_w, w_gu[0], b_gu[0], w_dn[0], b_dn[0], n_a=n_p)

    heads = lambda a, lead: a.reshape(lead + (MOBA_HEADS, HEAD_DIM))
    return (y_p.reshape(1, n_p, d), y_s.reshape(n_seq, n_st, d),
            heads(proj_b[:n_p, B_MK:B_MK + MOBA_W], (1, 1, n_p)), heads(proj_b[:n_p, B_MV:B_MV + MOBA_W], (1, 1, n_p)),
            p_gdn[None, None], p_conv[None, None],
            heads(mem_kv[:, :MEM_W], (1, 1, n_mem)), heads(mem_kv[:, MEM_W:], (1, 1, n_mem)),
            heads(proj_b3[..., B_MK:B_MK + MOBA_W], (1, n_seq, n_st)),
            heads(proj_b3[..., B_MV:B_MV + MOBA_W], (1, n_seq, n_st)),
            s_gdn[None], s_conv[None])
```

```python
import functools
import math

import numpy as np
import jax
import jax.numpy as jnp
from jax import lax
from jax.experimental import pallas as pl
from jax.experimental.pallas import tpu as pltpu

f32 = jnp.float32
bf16 = jnp.bfloat16
i32 = jnp.int32
HIGHEST = lax.Precision.HIGHEST

LANES = 128
SUBLANES = 8
VMEM_LIMIT = 60 * 1024 * 1024

D_MODEL = 2048
HEAD_DIM = 128
GDN_HEADS = 8
GDN_CONV = 4
GDN_CHUNK = 64
GDN_QK = GDN_HEADS * HEAD_DIM
GDN_V = GDN_HEADS * HEAD_DIM
GDN_CONV_DIM = 2 * GDN_QK + GDN_V
MOBA_HEADS = 4
MOBA_BLOCK = 256
MOBA_TOPK = 3
MOBA_W = MOBA_HEADS * HEAD_DIM
MEM_HEADS = 4
MEM_W = MEM_HEADS * HEAD_DIM
N_BUCKETS = 32
MAX_DISTANCE = 128
N_EXPERTS = 32
TOP_K = 4
D_FF = D_MODEL
SWIGLU_LIMIT = 7.0
SWIGLU_ALPHA = 1.702
EPS = 1e-6
NEG_INF = -1e30

A_W = GDN_CONV_DIM + GDN_V
B_MQ, B_MK, B_CQ, B_MV, B_BA = 0, MOBA_W, 2 * MOBA_W, 3 * MOBA_W, 4 * MOBA_W
B_W = B_BA + LANES
B_NORM_GROUPS = 3 * MOBA_HEADS


def _params(*sem):
    return pltpu.CompilerParams(dimension_semantics=sem, vmem_limit_bytes=VMEM_LIMIT)


def _dot(a, b, **kw):
    return jnp.dot(a, b, preferred_element_type=f32, **kw)


def _dot_nt(a, b, **kw):
    return lax.dot_general(a, b, (((1,), (1,)), ((), ())), preferred_element_type=f32, **kw)


def _dot_tn(a, b, **kw):
    return lax.dot_general(a, b, (((0,), (0,)), ((), ())), preferred_element_type=f32, **kw)


def _rms(x, gain):
    return x * lax.rsqrt(jnp.mean(x * x, axis=-1, keepdims=True) + EPS) * gain


def _silu(x):
    return x * jax.nn.sigmoid(x)


def _stacked_specs(arrays, tm, grid_rank):
    counts = [a.shape[0] // tm for a in arrays]
    starts = [sum(counts[:k]) for k in range(len(arrays))]

    def spec(k):
        def index_map(i, *_):
            return (jnp.clip(i - starts[k], 0, counts[k] - 1), 0)
        return pl.BlockSpec((tm, arrays[k].shape[1]), index_map)

    return [spec(k) for k in range(len(arrays))], starts, sum(counts)


def _stacked_tile(refs, starts, i):
    x = refs[0][...]
    for ref, start in zip(refs[1:], starts[1:]):
        x = jnp.where(i >= start, ref[...], x)
    return x


def _norm_matmul_kernel(*refs, n_norm, starts, head_outs, n_heads):
    n_x = len(starts)
    x_refs = refs[:n_x]
    g_ref, w_ref, gain_ref, o_ref = refs[n_x:n_x + 4]
    pair_refs = refs[n_x + 4:-1]
    h_ref = refs[-1]
    i = pl.program_id(0)

    @pl.when(pl.program_id(1) == 0)
    def _():
        h_ref[...] = _rms(_stacked_tile(x_refs, starts, i), g_ref[...]).astype(bf16)

    acc = _dot(h_ref[...], w_ref[...])
    if n_norm == 0 and not head_outs:
        o_ref[...] = acc
        return
    tm = acc.shape[0]
    blks = []
    for gi in range(acc.shape[1] // LANES):
        sl = slice(gi * LANES, (gi + 1) * LANES)
        blk = acc[:, sl]
        if gi < n_norm:
            blk = _rms(blk, gain_ref[:, sl])
        o_ref[:, sl] = blk
        blks.append(blk)
    bounds = list(starts[1:]) + [None]
    for e, first in enumerate(head_outs):
        for k in range(n_x):
            ref = pair_refs[e * n_x + k]
            mine = (i >= starts[k]) if bounds[k] is None else ((i >= starts[k]) & (i < bounds[k]))

            @pl.when(mine)
            def _(ref=ref, first=first):
                for h in range(n_heads):
                    ref[pl.ds(h, tm, stride=n_heads), :] = blks[first + h]


def _norm_matmul(xs, g, w, gain, *, tm, tn, n_norm, head_outs=(), n_heads=MOBA_HEADS):
    d = xs[0].shape[1]
    wn = w.shape[1]
    assert not head_outs or tn == wn
    x_specs, starts, n_tiles = _stacked_specs(xs, tm, 2)
    counts = [x.shape[0] // tm for x in xs]
    out_specs = [pl.BlockSpec((tm, tn), lambda i, j: (i, j))]
    out_shape = [jax.ShapeDtypeStruct((n_tiles * tm, wn), f32)]
    for _ in head_outs:
        for k, x in enumerate(xs):
            out_specs.append(pl.BlockSpec(
                (tm * n_heads, LANES), lambda i, j, k=k: (jnp.clip(i - starts[k], 0, counts[k] - 1), 0)))
            out_shape.append(jax.ShapeDtypeStruct((x.shape[0] * n_heads, LANES), f32))
    outs = pl.pallas_call(
        functools.partial(_norm_matmul_kernel, n_norm=n_norm, starts=starts, head_outs=tuple(head_outs),
                          n_heads=n_heads),
        grid=(n_tiles, wn // tn),
        in_specs=x_specs + [
            pl.BlockSpec((1, d), lambda i, j: (0, 0)),
            pl.BlockSpec((d, tn), lambda i, j: (0, j)),
            pl.BlockSpec((1, gain.shape[1]), lambda i, j: (0, 0)),
        ],
        out_specs=out_specs,
        out_shape=out_shape,
        scratch_shapes=[pltpu.VMEM((tm, d), bf16)],
        compiler_params=_params("arbitrary", "arbitrary"),
        name="norm_matmul",
    )(*xs, g, w, gain)
    return outs if head_outs else outs[0]


INV_BLOCK = 16


def _split_bf16(x):
    hi = x.astype(bf16)
    return hi, (x - hi.astype(f32)).astype(bf16)


def _dot3(a, b, dot=_dot):
    a_hi, a_lo = a
    b_hi, b_lo = b
    return dot(a_hi, b_hi) + (dot(a_hi, b_lo) + dot(a_lo, b_hi))


def _unit_lower_inverses(lmats):
    c = lmats[0].shape[0]
    row = lax.broadcasted_iota(i32, (c, c), 0)
    col = lax.broadcasted_iota(i32, (c, c), 1)
    same = lambda n: (row // n) == (col // n)
    eye = jnp.where(row == col, 1.0, 0.0).astype(f32)
    ms = [jnp.where(same(INV_BLOCK), l, 0.0) for l in lmats]
    ps = [eye - m for m in ms]
    power = 1
    while 2 * power < INV_BLOCK:
        sm = [_split_bf16(m) for m in ms]
        ms = [_dot3(s, s) for s in sm]
        sm = [_split_bf16(m) for m in ms]
        ps = [p + _dot3(_split_bf16(p), s) for p, s in zip(ps, sm)]
        power *= 2
    n = 2 * INV_BLOCK
    while n <= c:
        offs = [_split_bf16(jnp.where(same(n) & ~same(n // 2), l, 0.0)) for l in lmats]
        sp = [_split_bf16(p) for p in ps]
        mids = [_dot3(o, s) for o, s in zip(offs, sp)]
        ps = [p - _dot3(s, _split_bf16(m)) for p, s, m in zip(ps, sp, mids)]
        n *= 2
    return ps


def _gdn_prompt_kernel(qkv_ref, z_ref, ba_ref, convw_ref, alog_ref, dtb_ref, onorm_ref,
                       o_ref, s_out_ref, xp_ref, s_ref):
    c = GDN_CHUNK
    step = pl.program_id(0)

    @pl.when(step == 0)
    def _():
        xp_ref[0:SUBLANES, :] = jnp.zeros((SUBLANES, GDN_CONV_DIM), f32)
        s_ref[...] = jnp.zeros_like(s_ref)

    xp_ref[SUBLANES:SUBLANES + c, :] = qkv_ref[...]
    w = convw_ref[...]
    y = None
    for j in range(GDN_CONV):
        lo = SUBLANES - (GDN_CONV - 1) + j
        term = xp_ref[lo:lo + c, :] * w[j:j + 1, :]
        y = term if y is None else y + term
    y = _silu(y)
    xp_ref[0:SUBLANES, :] = xp_ref[c:c + SUBLANES, :]

    ba = ba_ref[...]
    beta_all = jax.nn.sigmoid(ba)
    g_all = -jnp.exp(alog_ref[...]) * jax.nn.softplus(ba + dtb_ref[...])
    row = lax.broadcasted_iota(i32, (c, c), 0)
    col = lax.broadcasted_iota(i32, (c, c), 1)
    incl = row >= col
    strict = row > col
    gcum_all = _dot(jnp.where(incl, 1.0, 0.0).astype(f32), g_all, precision=HIGHEST)
    gcum_t = gcum_all.T
    onorm = onorm_ref[...]

    heads = range(GDN_HEADS)
    head_cols = lambda off, h: slice(off + h * HEAD_DIM, off + (h + 1) * HEAD_DIM)
    qs = [y[:, head_cols(0, h)] for h in heads]
    ks = [y[:, head_cols(GDN_QK, h)] for h in heads]
    vs = [y[:, head_cols(2 * GDN_QK, h)] for h in heads]
    qs = [q * lax.rsqrt(jnp.sum(q * q, axis=-1, keepdims=True) + EPS) * (HEAD_DIM ** -0.5) for q in qs]
    ks = [k * lax.rsqrt(jnp.sum(k * k, axis=-1, keepdims=True) + EPS) for k in ks]
    betas = [beta_all[:, h:h + 1] for h in heads]
    gcs = [gcum_all[:, GDN_HEADS + h:GDN_HEADS + h + 1] for h in heads]
    grs = [gcum_t[GDN_HEADS + h:GDN_HEADS + h + 1, :] for h in heads]
    g_lasts = [gcum_all[c - 1:c, GDN_HEADS + h:GDN_HEADS + h + 1] for h in heads]
    decays = [jnp.exp(jnp.where(incl, gc - gr, -jnp.inf)) for gc, gr in zip(gcs, grs)]
    kbetas = [k * b for k, b in zip(ks, betas)]
    lmats = [_dot_nt(kb, k) * jnp.where(strict, dc, 0.0) for kb, k, dc in zip(kbetas, ks, decays)]
    tinvs = _unit_lower_inverses(lmats)
    egcs = [jnp.exp(gc) for gc in gcs]
    us = [_dot(t, v * b) for t, v, b in zip(tinvs, vs, betas)]
    ws = [_dot(t, kb * e) for t, kb, e in zip(tinvs, kbetas, egcs)]
    intras = [_dot_nt(q, k) * dc for q, k, dc in zip(qs, ks, decays)]
    kdecs = [k * jnp.exp(gl - gc) for k, gl, gc in zip(ks, g_lasts, gcs)]
    ss = [s_ref[h] for h in heads]
    v_news = [u - _dot(w_, s) for u, w_, s in zip(us, ws, ss)]
    os_ = [_dot(q * e, s) + _dot(it, vn) for q, e, s, it, vn in zip(qs, egcs, ss, intras, v_news)]
    for h in heads:
        s_ref[h] = ss[h] * jnp.exp(g_lasts[h]) + _dot_tn(kdecs[h], v_news[h])
    for h in heads:
        hs = head_cols(0, h)
        o_ref[:, hs] = (_rms(os_[h], onorm) * _silu(z_ref[:, hs])).astype(o_ref.dtype)

    @pl.when(step == pl.num_programs(0) - 1)
    def _():
        s_out_ref[...] = s_ref[...]


def _gdn_prompt(proj_a, proj_b, conv_w, alog_row, dtb_row, onorm, *, n_tok):
    c = GDN_CHUNK
    zblk = GDN_CONV_DIM // GDN_V
    return pl.pallas_call(
        _gdn_prompt_kernel,
        grid=(n_tok // c,),
        in_specs=[
            pl.BlockSpec((c, GDN_CONV_DIM), lambda i: (i, 0)),
            pl.BlockSpec((c, GDN_V), lambda i: (i, zblk)),
            pl.BlockSpec((c, LANES), lambda i: (i, B_BA // LANES)),
            pl.BlockSpec((GDN_CONV, GDN_CONV_DIM), lambda i: (0, 0)),
            pl.BlockSpec((1, LANES), lambda i: (0, 0)),
            pl.BlockSpec((1, LANES), lambda i: (0, 0)),
            pl.BlockSpec((1, HEAD_DIM), lambda i: (0, 0)),
        ],
        out_specs=[
            pl.BlockSpec((c, GDN_V), lambda i: (i, 0)),
            pl.BlockSpec((GDN_HEADS, HEAD_DIM, HEAD_DIM), lambda i: (0, 0, 0)),
        ],
        out_shape=[
            jax.ShapeDtypeStruct((n_tok, GDN_V), bf16),
            jax.ShapeDtypeStruct((GDN_HEADS, HEAD_DIM, HEAD_DIM), f32),
        ],
        scratch_shapes=[
            pltpu.VMEM((c + SUBLANES, GDN_CONV_DIM), f32),
            pltpu.VMEM((GDN_HEADS, HEAD_DIM, HEAD_DIM), f32),
        ],
        compiler_params=_params("arbitrary"),
        name="gdn_prompt",
    )(proj_a, proj_a, proj_b, conv_w, alog_row, dtb_row, onorm)


GDN_S_SEQ = SUBLANES


def _gdn_sample_kernel(q_ref, k_ref, v_ref, z_ref, ba_ref, bq_ref, bk_ref, bv_ref, wq_ref, wk_ref, wv_ref,
                       alog_ref, dtb_ref, onorm_ref, s_ref, o_ref, so_ref, *, n_tok):
    h = pl.program_id(1)
    nb = GDN_S_SEQ
    lane = lax.broadcasted_iota(i32, (nb, LANES), 1)

    def conv(x_ref, buf_ref, w_ref, t):
        y = None
        for j in range(GDN_CONV):
            i = t + j
            row = buf_ref[i] if i < GDN_CONV - 1 else x_ref[i - (GDN_CONV - 1)]
            term = row * w_ref[j:j + 1, :]
            y = term if y is None else y + term
        return _silu(y)

    def lane_col(x, idx):
        return jnp.sum(jnp.where(lane == idx, x, 0.0), axis=1, keepdims=True)

    qs, ks, vs, betas, egs = [], [], [], [], []
    for t in range(n_tok):
        q = conv(q_ref, bq_ref, wq_ref, t)
        k = conv(k_ref, bk_ref, wk_ref, t)
        q = q * lax.rsqrt(jnp.sum(q * q, axis=-1, keepdims=True) + EPS) * (HEAD_DIM ** -0.5)
        k = k * lax.rsqrt(jnp.sum(k * k, axis=-1, keepdims=True) + EPS)
        qs.append(q.T)
        ks.append(k.T)
        vs.append(conv(v_ref, bv_ref, wv_ref, t))
        ba = ba_ref[t]
        betas.append(lane_col(jax.nn.sigmoid(ba), h))
        g = -jnp.exp(alog_ref[...]) * jax.nn.softplus(ba + dtb_ref[...])
        egs.append(jnp.exp(lane_col(g, GDN_HEADS + h)))

    o_rows = [[None] * nb for _ in range(n_tok)]
    for b in range(nb):
        s = s_ref[b]
        for t in range(n_tok):
            kc = ks[t][:, b:b + 1]
            qc = qs[t][:, b:b + 1]
            s = s * egs[t][b:b + 1, :]
            ks_row = jnp.sum(kc * s, axis=0, keepdims=True)
            delta = (vs[t][b:b + 1, :] - ks_row) * betas[t][b:b + 1, :]
            s = s + kc * delta
            o_rows[t][b] = jnp.sum(qc * s, axis=0, keepdims=True)
        so_ref[b] = s
    for t in range(n_tok):
        o = jnp.concatenate(o_rows[t], axis=0)
        o_ref[t] = _rms(o, onorm_ref[...]) * _silu(z_ref[t])


def _gdn_sample(xa, ba, conv_buf, conv_w, alog_row, dtb_row, onorm, state):
    n_tok, n_seq, _ = xa.shape
    nb = GDN_S_SEQ
    hq, hk, hv, hz = 0, GDN_HEADS, 2 * GDN_HEADS, 3 * GDN_HEADS

    def xspec(off, rows):
        return pl.BlockSpec((rows, nb, HEAD_DIM), lambda i, h: (0, i, off + h))

    def wspec(off):
        return pl.BlockSpec((GDN_CONV, HEAD_DIM), lambda i, h: (0, off + h))

    row = pl.BlockSpec((1, LANES), lambda i, h: (0, 0))
    sspec = pl.BlockSpec((nb, None, HEAD_DIM, HEAD_DIM), lambda i, h: (i, h, 0, 0))
    return pl.pallas_call(
        functools.partial(_gdn_sample_kernel, n_tok=n_tok),
        grid=(n_seq // nb, GDN_HEADS),
        in_specs=[xspec(hq, n_tok), xspec(hk, n_tok), xspec(hv, n_tok), xspec(hz, n_tok),
                  pl.BlockSpec((n_tok, nb, LANES), lambda i, h: (0, i, 0)),
                  xspec(hq, GDN_CONV - 1), xspec(hk, GDN_CONV - 1), xspec(hv, GDN_CONV - 1),
                  wspec(hq), wspec(hk), wspec(hv), row, row, row, sspec],
        out_specs=[pl.BlockSpec((n_tok, nb, HEAD_DIM), lambda i, h: (0, i, h)), sspec],
        out_shape=[jax.ShapeDtypeStruct((n_tok, n_seq, GDN_V), f32),
                   jax.ShapeDtypeStruct(state.shape, f32)],
        compiler_params=_params("parallel", "arbitrary"),
        name="gdn_sample",
    )(xa, xa, xa, xa, ba, conv_buf, conv_buf, conv_buf, conv_w, conv_w, conv_w, alog_row, dtb_row, onorm, state)


def _bucket_thresholds():
    exact = N_BUCKETS // 2
    d = np.arange(0, 2 * MAX_DISTANCE)
    val = np.log(np.maximum(d, 1).astype(np.float64) / exact) / math.log(MAX_DISTANCE / exact) * (N_BUCKETS - exact)
    frac = np.abs(val - np.round(val))[exact + 1:MAX_DISTANCE]
    assert frac.min() > 1e-3, "a bucket boundary sits on an integer distance"
    bucket = np.where(d < exact, d, np.minimum(exact + val.astype(np.int64), N_BUCKETS - 1))
    assert np.all(np.diff(bucket) >= 0)
    return [int(np.argmax(bucket >= b)) for b in range(N_BUCKETS)]


_BUCKET_THR = _bucket_thresholds()


def _bias_from_dist(dist, rb_ref, h):
    v = jnp.full(dist.shape, rb_ref[0, h], f32)
    for b in range(1, N_BUCKETS):
        v = jnp.where(dist >= _BUCKET_THR[b], rb_ref[b, h], v)
    return v


def _topk_mask(s, blk, k):
    nblk = s.shape[1]
    sel = jnp.zeros(s.shape, f32)
    for _ in range(k):
        m = jnp.max(s, axis=1, keepdims=True)
        cand = jnp.where((s == m) & (m > -jnp.inf), blk, nblk)
        pick = blk == jnp.min(cand, axis=1, keepdims=True)
        sel = jnp.where(pick, 1.0, sel)
        s = jnp.where(pick, -jnp.inf, s)
    return sel


def _block_mean_kernel(k_ref, o_ref):
    o_ref[0] = jnp.mean(k_ref[...], axis=0, keepdims=True)


def _block_mean(proj_b, n_blk):
    return pl.pallas_call(
        _block_mean_kernel,
        grid=(n_blk,),
        in_specs=[pl.BlockSpec((MOBA_BLOCK, MOBA_W), lambda i: (i, B_MK // MOBA_W))],
        out_specs=pl.BlockSpec((1, 1, MOBA_W), lambda i: (i, 0, 0)),
        out_shape=jax.ShapeDtypeStruct((n_blk, 1, MOBA_W), f32),
        compiler_params=_params("parallel"),
        name="moba_block_mean",
    )(proj_b)


def _moba_select_kernel(q_ref, km_ref, o_ref):
    own = pl.program_id(0)
    nblk = km_ref.shape[0]
    blk = lax.broadcasted_iota(i32, (MOBA_BLOCK, nblk), 1)
    outs = []
    for h in range(MOBA_HEADS):
        hs = slice(h * HEAD_DIM, (h + 1) * HEAD_DIM)
        s = _dot3(_split_bf16(q_ref[:, hs]), _split_bf16(km_ref[:, hs]), dot=_dot_nt)
        s = jnp.where(blk < own, s, -jnp.inf)
        outs.append(_topk_mask(s, blk, MOBA_TOPK))
    pad = LANES - MOBA_HEADS * nblk
    if pad:
        outs.append(jnp.zeros((MOBA_BLOCK, pad), f32))
    o_ref[...] = jnp.concatenate(outs, axis=1)


def _moba_select(proj_b, kmean, n_blk):
    assert n_blk * MOBA_HEADS <= LANES
    return pl.pallas_call(
        _moba_select_kernel,
        grid=(n_blk,),
        in_specs=[pl.BlockSpec((MOBA_BLOCK, MOBA_W), lambda i: (i, B_MQ // MOBA_W)),
                  pl.BlockSpec((n_blk, MOBA_W), lambda i: (0, 0))],
        out_specs=pl.BlockSpec((MOBA_BLOCK, LANES), lambda i: (i, 0)),
        out_shape=jax.ShapeDtypeStruct((n_blk * MOBA_BLOCK, LANES), f32),
        compiler_params=_params("parallel"),
        name="moba_select",
    )(proj_b, kmean)


def _moba_prompt_kernel(qi_ref, kj_ref, rb_ref, q_ref, k_ref, v_ref, sel_ref, o_ref,
                        bias_ref, m_ref, l_ref, acc_ref, *, n_blk):
    step = pl.program_id(0)
    qi = qi_ref[step]
    kj = kj_ref[step]
    nq = MOBA_BLOCK

    @pl.when(step == 0)
    def _():
        r = lax.broadcasted_iota(i32, (nq, nq), 0)
        c = lax.broadcasted_iota(i32, (nq, nq), 1)
        for h in range(MOBA_HEADS):
            bias_ref[h, 0] = jnp.where(c <= r, _bias_from_dist(r - c, rb_ref, h), NEG_INF)
            bias_ref[h, 1] = _bias_from_dist(r - c + nq, rb_ref, h)

    first = kj == qi

    @pl.when(first)
    def _():
        m_ref[...] = jnp.full(m_ref.shape, NEG_INF, f32)
        l_ref[...] = jnp.zeros_like(l_ref)
        acc_ref[...] = jnp.zeros_like(acc_ref)

    heads = range(MOBA_HEADS)
    hcols = [slice(h * HEAD_DIM, (h + 1) * HEAD_DIM) for h in heads]
    scale = HEAD_DIM ** -0.5

    def attend(far):
        sel = sel_ref[...]
        lane = lax.broadcasted_iota(i32, sel.shape, 1)
        ss = [_dot_nt(q_ref[:, hs].astype(bf16), k_ref[:, hs].astype(bf16)) for hs in hcols]
        picked = [jnp.max(jnp.where(lane == h * n_blk + kj, sel, 0.0), axis=1, keepdims=True) > 0.0 for h in heads]
        if far:
            cols = [jnp.where(pk, rb_ref[N_BUCKETS - 1, h], NEG_INF) for h, pk in zip(heads, picked)]
            ss = [s * scale + c for s, c in zip(ss, cols)]
        else:
            slot = jnp.where(first, 0, 1)
            ss = [s * scale + bias_ref[h, slot] for h, s in zip(heads, ss)]
            ss = [jnp.where(pk | first, s, NEG_INF) for pk, s in zip(picked, ss)]
        m_prevs = [m_ref[h] for h in heads]
        m_news = [jnp.maximum(mp, jnp.max(s, axis=1, keepdims=True)) for mp, s in zip(m_prevs, ss)]
        alphas = [jnp.exp(mp - mn) for mp, mn in zip(m_prevs, m_news)]
        ps = [jnp.exp(s - mn) for s, mn in zip(ss, m_news)]
        pvs = [_dot(p.astype(bf16), v_ref[:, hs].astype(bf16)) for p, hs in zip(ps, hcols)]
        for h in heads:
            l_ref[h] = alphas[h] * l_ref[h] + jnp.sum(ps[h], axis=1, keepdims=True)
            acc_ref[h] = alphas[h] * acc_ref[h] + pvs[h]
            m_ref[h] = m_news[h]

    is_far = kj < qi - 1
    pl.when(is_far)(lambda: attend(True))
    pl.when(jnp.logical_not(is_far))(lambda: attend(False))

    @pl.when(kj == 0)
    def _():
        for h in range(MOBA_HEADS):
            hs = slice(h * HEAD_DIM, (h + 1) * HEAD_DIM)
            o_ref[:, hs] = (acc_ref[h] / l_ref[h]).astype(o_ref.dtype)


def _moba_prompt(proj_b, sel, rel_bias, n_blk):
    qi = np.concatenate([np.full(i + 1, i) for i in range(n_blk)]).astype(np.int32)
    kj = np.concatenate([np.arange(i, -1, -1) for i in range(n_blk)]).astype(np.int32)
    nq = MOBA_BLOCK
    grid_spec = pltpu.PrefetchScalarGridSpec(
        num_scalar_prefetch=2,
        grid=(len(qi),),
        in_specs=[
            pl.BlockSpec(memory_space=pltpu.SMEM),
            pl.BlockSpec((nq, MOBA_W), lambda s, qi, kj: (qi[s], B_MQ // MOBA_W)),
            pl.BlockSpec((nq, MOBA_W), lambda s, qi, kj: (kj[s], B_MK // MOBA_W)),
            pl.BlockSpec((nq, MOBA_W), lambda s, qi, kj: (kj[s], B_MV // MOBA_W)),
            pl.BlockSpec((nq, LANES), lambda s, qi, kj: (qi[s], 0)),
        ],
        out_specs=pl.BlockSpec((nq, MOBA_W), lambda s, qi, kj: (qi[s], 0)),
        scratch_shapes=[
            pltpu.VMEM((MOBA_HEADS, 2, nq, nq), f32),
            pltpu.VMEM((MOBA_HEADS, nq, 1), f32),
            pltpu.VMEM((MOBA_HEADS, nq, 1), f32),
            pltpu.VMEM((MOBA_HEADS, nq, HEAD_DIM), f32),
        ],
    )
    return pl.pallas_call(
        functools.partial(_moba_prompt_kernel, n_blk=n_blk),
        grid_spec=grid_spec,
        out_shape=jax.ShapeDtypeStruct((n_blk * nq, MOBA_W), bf16),
        compiler_params=_params("arbitrary"),
        name="moba_prompt",
    )(jnp.asarray(qi), jnp.asarray(kj), rel_bias, proj_b, proj_b, proj_b, sel)


def _per_head_bias(dist, row_head, rb_ref):
    v = _bias_from_dist(dist, rb_ref, 0)
    for h in range(1, MOBA_HEADS):
        v = jnp.where(row_head == h, _bias_from_dist(dist, rb_ref, h), v)
    return v


def _moba_sample_kernel(pt_ref, rb_ref, q_ref, kn_ref, vn_ref, *rest, n_pages, page):
    kp = rest[:n_pages]
    vp = rest[n_pages:2 * n_pages]
    o_ref, bias_ref, biasn_ref = rest[2 * n_pages:]
    nh = MOBA_HEADS
    rows = q_ref.shape[0]
    prow = page * nh
    past = n_pages * page
    n_blk = past // MOBA_BLOCK
    ppb = MOBA_BLOCK // page

    @pl.when(pl.program_id(0) == 0)
    def _():
        r = lax.broadcasted_iota(i32, (rows, past * nh), 0)
        c = lax.broadcasted_iota(i32, (rows, past * nh), 1)
        bias = _per_head_bias(past + r // nh - c // nh, r % nh, rb_ref)
        bias_ref[...] = jnp.where(r % nh == c % nh, bias, NEG_INF)
        r = lax.broadcasted_iota(i32, (rows, rows), 0)
        c = lax.broadcasted_iota(i32, (rows, rows), 1)
        bias = _per_head_bias(r // nh - c // nh, r % nh, rb_ref)
        biasn_ref[...] = jnp.where((r % nh == c % nh) & (c // nh <= r // nh), bias, NEG_INF)

    kb, means = [], []
    for p in range(n_pages):
        kpage = kp[p][...]
        kb.append(kpage.astype(bf16))
        part = jnp.sum(kpage.reshape(prow // SUBLANES, SUBLANES, HEAD_DIM), axis=0)
        part = part[0:nh] + part[nh:2 * nh]
        if p % ppb == 0:
            means.append(part)
        else:
            means[-1] = means[-1] + part
    kmean = jnp.concatenate(means, axis=0) / MOBA_BLOCK

    q = q_ref[...]
    r = lax.broadcasted_iota(i32, (rows, n_blk * nh), 0)
    c = lax.broadcasted_iota(i32, (rows, n_blk * nh), 1)
    s = jnp.where(r % nh == c % nh, _dot_nt(q, kmean, precision=HIGHEST), -jnp.inf)
    sel = _topk_mask(s, c, MOBA_TOPK)
    picked = [jnp.max(jnp.where(c // nh == n, sel, 0.0), axis=1, keepdims=True) > 0.0 for n in range(n_blk)]

    qb = q.astype(bf16)
    scale = HEAD_DIM ** -0.5
    ln = _dot_nt(qb, kn_ref[...].astype(bf16)) * scale + biasn_ref[...]
    m = jnp.max(ln, axis=1, keepdims=True)
    lps = []
    for p in range(n_pages):
        lp = _dot_nt(qb, kb[p]) * scale + bias_ref[:, p * prow:(p + 1) * prow]
        lp = jnp.where(picked[p // ppb], lp, NEG_INF)
        m = jnp.maximum(m, jnp.max(lp, axis=1, keepdims=True))
        lps.append(lp)
    pn = jnp.exp(ln - m)
    den = jnp.sum(pn, axis=1, keepdims=True)
    num = _dot(pn.astype(bf16), vn_ref[...].astype(bf16))
    for p in range(n_pages):
        pp = jnp.exp(lps[p] - m)
        den = den + jnp.sum(pp, axis=1, keepdims=True)
        num = num + _dot(pp.astype(bf16), vp[p][...].astype(bf16))
    o_ref[...] = num / den


def _moba_sample(q, k_new, v_new, pool_k, pool_v, page_table, rel_bias, *, page):
    n_seq, rows, _ = q.shape
    n_pages = page_table.shape[1]
    prow = page * MOBA_HEADS
    new_spec = pl.BlockSpec((None, rows, HEAD_DIM), lambda b, pt: (b, 0, 0))

    def page_spec(p):
        return pl.BlockSpec((prow, HEAD_DIM), lambda b, pt: (pt[b, p], 0))

    grid_spec = pltpu.PrefetchScalarGridSpec(
        num_scalar_prefetch=1,
        grid=(n_seq,),
        in_specs=[pl.BlockSpec(memory_space=pltpu.SMEM), new_spec, new_spec, new_spec]
        + [page_spec(p) for p in range(n_pages)] * 2,
        out_specs=new_spec,
        scratch_shapes=[pltpu.VMEM((rows, n_pages * prow), f32), pltpu.VMEM((rows, rows), f32)],
    )
    return pl.pallas_call(
        functools.partial(_moba_sample_kernel, n_pages=n_pages, page=page),
        grid_spec=grid_spec,
        out_shape=jax.ShapeDtypeStruct((n_seq, rows, HEAD_DIM), f32),
        compiler_params=_params("arbitrary"),
        name="moba_sample",
    )(page_table, rel_bias, q, k_new, v_new, *([pool_k] * n_pages), *([pool_v] * n_pages))


def _mem_attn_kernel(q_ref, k_ref, v_ref, o_ref):
    for h in range(MEM_HEADS):
        hs = slice(h * HEAD_DIM, (h + 1) * HEAD_DIM)
        s = _dot_nt(q_ref[:, hs].astype(bf16), k_ref[:, hs].astype(bf16)) * (HEAD_DIM ** -0.5)
        p = jnp.exp(s - jnp.max(s, axis=1, keepdims=True))
        num = _dot(p.astype(bf16), v_ref[:, hs].astype(bf16))
        o_ref[:, hs] = (num / jnp.sum(p, axis=1, keepdims=True)).astype(o_ref.dtype)


def _mem_attn_prompt(proj_b, mem_kv, *, n_tok, tq):
    n_mem = mem_kv.shape[0]
    return pl.pallas_call(
        _mem_attn_kernel,
        grid=(n_tok // tq,),
        in_specs=[pl.BlockSpec((tq, MEM_W), lambda i: (i, B_CQ // MEM_W)),
                  pl.BlockSpec((n_mem, MEM_W), lambda i: (0, 0)),
                  pl.BlockSpec((n_mem, MEM_W), lambda i: (0, 1))],
        out_specs=pl.BlockSpec((tq, MEM_W), lambda i: (i, 0)),
        out_shape=jax.ShapeDtypeStruct((n_tok, MEM_W), bf16),
        compiler_params=_params("parallel"),
        name="mem_attn_prompt",
    )(proj_b, mem_kv, mem_kv)


def _mem_attn_sample_kernel(q_ref, k_ref, v_ref, o_ref):
    s = _dot_nt(q_ref[...].astype(bf16), k_ref[...].astype(bf16)) * (HEAD_DIM ** -0.5)
    r = lax.broadcasted_iota(i32, s.shape, 0)
    c = lax.broadcasted_iota(i32, s.shape, 1)
    s = jnp.where(r % MEM_HEADS == c % MEM_HEADS, s, NEG_INF)
    p = jnp.exp(s - jnp.max(s, axis=1, keepdims=True))
    o_ref[...] = _dot(p.astype(bf16), v_ref[...].astype(bf16)) / jnp.sum(p, axis=1, keepdims=True)


def _mem_attn_sample(q, mem_k, mem_v):
    n_seq, rows, _ = q.shape
    mrows = mem_k.shape[1]
    return pl.pallas_call(
        _mem_attn_sample_kernel,
        grid=(n_seq,),
        in_specs=[pl.BlockSpec((None, rows, HEAD_DIM), lambda b: (b, 0, 0)),
                  pl.BlockSpec((None, mrows, HEAD_DIM), lambda b: (b, 0, 0)),
                  pl.BlockSpec((None, mrows, HEAD_DIM), lambda b: (b, 0, 0))],
        out_specs=pl.BlockSpec((None, rows, HEAD_DIM), lambda b: (b, 0, 0)),
        out_shape=jax.ShapeDtypeStruct((n_seq, rows, HEAD_DIM), f32),
        compiler_params=_params("parallel"),
        name="mem_attn_sample",
    )(q, mem_k, mem_v)


def _outproj_router_kernel(*refs, starts):
    n_g = len(starts)
    i = pl.program_id(0)
    x, og, om, oc = [_stacked_tile(refs[k * n_g:(k + 1) * n_g], starts, i) for k in range(4)]
    w_ref, g_ref, rw_ref, rb_ref, y_ref, h_ref, ti_ref, tw_ref = refs[4 * n_g:]
    mix = (_dot(og, w_ref[0:GDN_V, :])
           + _dot(om, w_ref[GDN_V:GDN_V + MOBA_W, :])
           + _dot(oc, w_ref[GDN_V + MOBA_W:, :]))
    y = x + mix
    y_ref[...] = y
    h = _rms(y, g_ref[...])
    _store_slabs(h_ref, h)
    h_hi, h_lo = _split_bf16(h)
    w_hi, w_lo = _split_bf16(rw_ref[...])
    logits = _dot(h_hi, w_hi) + (_dot(h_hi, w_lo) + _dot(h_lo, w_hi)) + rb_ref[...]
    lane = lax.broadcasted_iota(i32, logits.shape, 1)
    s = jnp.where(lane < N_EXPERTS, logits, -jnp.inf)
    vals, idxs = [], []
    for _ in range(TOP_K):
        m = jnp.max(s, axis=1, keepdims=True)
        idx = jnp.min(jnp.where(s == m, lane, LANES), axis=1, keepdims=True)
        vals.append(m)
        idxs.append(idx)
        s = jnp.where(lane == idx, -jnp.inf, s)
    exps = [jnp.exp(v - vals[0]) for v in vals]
    den = exps[0]
    for e in exps[1:]:
        den = den + e
    ti = jnp.zeros(logits.shape, i32)
    tw = jnp.zeros(logits.shape, f32)
    for r in range(TOP_K):
        ti = jnp.where(lane == r, idxs[r], ti)
        tw = jnp.where(lane == r, exps[r] / den, tw)
    ti_ref[...] = ti
    tw_ref[...] = tw


def _outproj_router(xs, ogs, oms, ocs, w_out, g, rw, rb, *, tm):
    d = xs[0].shape[1]
    row = lambda w: pl.BlockSpec((tm, w), lambda i: (i, 0))
    full = lambda a: pl.BlockSpec(a.shape, lambda i: (0,) * a.ndim)
    specs, starts, n_tiles = [], None, None
    for group in (xs, ogs, oms, ocs):
        group_specs, starts, n_tiles = _stacked_specs(group, tm, 1)
        specs += group_specs
    n = n_tiles * tm
    return pl.pallas_call(
        functools.partial(_outproj_router_kernel, starts=starts),
        grid=(n_tiles,),
        in_specs=specs + [full(w_out), full(g), full(rw), full(rb)],
        out_specs=[row(d), pl.BlockSpec((tm * (d // LANES), LANES), lambda i: (i, 0)), row(LANES), row(LANES)],
        out_shape=[jax.ShapeDtypeStruct((n, d), f32), jax.ShapeDtypeStruct((n * (d // LANES), LANES), f32),
                   jax.ShapeDtypeStruct((n, LANES), i32), jax.ShapeDtypeStruct((n, LANES), f32)],
        compiler_params=_params("parallel"),
        name="outproj_router",
    )(*xs, *ogs, *oms, *ocs, w_out, g, rw, rb)


GATHER_UNROLL = 8
SLAB_G = D_MODEL // LANES


def _store_slabs(slab_ref, x, mask=None):
    rows = x.shape[0]
    g = x.shape[1] // LANES
    for c in range(g):
        idx = (pl.ds(c, rows, stride=g), slice(None))
        blk = x[:, c * LANES:(c + 1) * LANES]
        slab_ref[idx] = blk if mask is None else jnp.where(mask, blk, slab_ref[idx])


def _load_slab_group(slab_ref, lead, c, rows, g):
    return slab_ref[lead + (pl.ds(c, rows, stride=g), slice(None))]


def _issue_slab_gather(idx_ref, idx_base, src_ref, dst_ref, sem, count, g):
    assert count % GATHER_UNROLL == 0

    def body(jj, carry):
        for u in range(GATHER_UNROLL):
            j = jj * GATHER_UNROLL + u
            tok = pl.multiple_of(idx_ref[idx_base + j] * g, g)
            dst = dst_ref.at[pl.ds(pl.multiple_of(j * g, g), g)]
            pltpu.make_async_copy(src_ref.at[pl.ds(tok, g)], dst, sem).start(priority=u % 2)
        return carry

    lax.fori_loop(0, count // GATHER_UNROLL, body, 0)


def _wait_slab_gather(src_ref, dst_ref, sem):
    pltpu.make_async_copy(src_ref.at[pl.ds(0, dst_ref.shape[0])], dst_ref, sem).wait()


def _gather_x_kernel(idx_ref, src_ref, o_ref, buf_ref, sem, *, rows, g):
    i = pl.program_id(0)
    slot = i % 2

    @pl.when(i == 0)
    def _():
        _issue_slab_gather(idx_ref, 0, src_ref, buf_ref.at[0], sem.at[0], rows, g)

    @pl.when(i + 1 < pl.num_programs(0))
    def _():
        _issue_slab_gather(idx_ref, (i + 1) * rows, src_ref, buf_ref.at[1 - slot], sem.at[1 - slot], rows, g)

    _wait_slab_gather(src_ref, buf_ref.at[slot], sem.at[slot])
    for c in range(g):
        o_ref[:, c * LANES:(c + 1) * LANES] = _load_slab_group(buf_ref, (slot,), c, rows, g).astype(o_ref.dtype)


def _gather_x(src, idx, *, rows, g):
    n_out = idx.shape[0]
    grid_spec = pltpu.PrefetchScalarGridSpec(
        num_scalar_prefetch=1,
        grid=(n_out // rows,),
        in_specs=[pl.BlockSpec(memory_space=pl.ANY)],
        out_specs=pl.BlockSpec((rows, g * LANES), lambda i, idx: (i, 0)),
        scratch_shapes=[pltpu.VMEM((2, rows * g, LANES), f32), pltpu.SemaphoreType.DMA((2,))],
    )
    return pl.pallas_call(
        functools.partial(_gather_x_kernel, rows=rows, g=g),
        grid_spec=grid_spec,
        out_shape=jax.ShapeDtypeStruct((n_out, g * LANES), bf16),
        compiler_params=_params("arbitrary"),
        name="moe_gather_x",
    )(idx, src)


def _new_expert(ev_ref, v):
    return (v == 0) | (ev_ref[v] != ev_ref[jnp.maximum(v - 1, 0)])


def _expert_weights_step(copies, is_new, run, n_runs, e_here, e_next, e_first, sweep, n_sweeps, on_arrival):
    slot = (run + sweep * n_runs) % 2

    @pl.when(is_new)
    def _():
        @pl.when((sweep == 0) & (run == 0))
        def _():
            for c in copies(e_here, sweep, slot):
                c.start()

        for c in copies(e_here, sweep, slot):
            c.wait()
        on_arrival(slot)
        last = run == n_runs - 1

        @pl.when(jnp.logical_not(last))
        def _():
            for c in copies(e_next, sweep, 1 - slot):
                c.start()

        @pl.when(last & (sweep + 1 < n_sweeps))
        def _():
            for c in copies(e_first, sweep + 1, 1 - slot):
                c.start()


def _moe_gate_up_kernel(tv_ref, ev_ref, lo_ref, hi_ref, run_ref, nxt_ref, nrun_ref, x_ref, w_hbm, bg_ref, bu_ref,
                        o_ref, wf_ref, wb_ref, sem, *, tf):
    f = pl.program_id(0)
    v = pl.program_id(1)
    lo = lo_ref[v]
    hi = hi_ref[v]

    def copies(e, sweep, slot):
        cols = lambda half: pl.ds(pl.multiple_of(half * D_FF + sweep * tf, tf), tf)
        return [pltpu.make_async_copy(w_hbm.at[e, :, cols(half)], wf_ref.at[slot, half], sem.at[slot])
                for half in (0, 1)]

    def on_arrival(slot):
        wb_ref[0] = wf_ref[slot, 0].astype(bf16)
        wb_ref[1] = wf_ref[slot, 1].astype(bf16)

    _expert_weights_step(copies, _new_expert(ev_ref, v), run_ref[v], nrun_ref[0], ev_ref[v], nxt_ref[v], ev_ref[0],
                         f, pl.num_programs(0), on_arrival)

    def activations():
        x = x_ref[...]
        gate = _dot(x, wb_ref[0]) + bg_ref[...]
        up = _dot(x, wb_ref[1]) + bu_ref[...]
        gate = jnp.minimum(gate, SWIGLU_LIMIT)
        up = jnp.clip(up, -SWIGLU_LIMIT, SWIGLU_LIMIT)
        return (gate * jax.nn.sigmoid(SWIGLU_ALPHA * gate) * (up + 1.0)).astype(o_ref.dtype)

    @pl.when(hi > lo)
    def _():
        @pl.when(lo == 0)
        def _():
            o_ref[...] = activations()

        @pl.when(lo > 0)
        def _():
            r = lax.broadcasted_iota(i32, o_ref.shape, 0)
            o_ref[...] = jnp.where((r >= lo) & (r < hi), activations(), o_ref[...])


def _moe_gate_up(xs, w_gu, b_gu3, visits, *, tm, tf):
    rows, d = xs.shape
    nf = D_FF // tf
    n_pre = len(visits)
    grid_spec = pltpu.PrefetchScalarGridSpec(
        num_scalar_prefetch=n_pre,
        grid=(nf, visits[0].shape[0]),
        in_specs=[
            pl.BlockSpec((tm, d), lambda f, v, tv, *_: (tv[v], 0)),
            pl.BlockSpec(memory_space=pl.ANY),
            pl.BlockSpec((None, 1, tf), lambda f, v, tv, ev, *_: (ev[v], 0, f)),
            pl.BlockSpec((None, 1, tf), lambda f, v, tv, ev, *_: (ev[v], 0, nf + f)),
        ],
        out_specs=pl.BlockSpec((tm, tf), lambda f, v, tv, *_: (tv[v], f)),
        scratch_shapes=[pltpu.VMEM((2, 2, d, tf), f32), pltpu.VMEM((2, d, tf), bf16), pltpu.SemaphoreType.DMA((2,))],
    )
    return pl.pallas_call(
        functools.partial(_moe_gate_up_kernel, tf=tf),
        grid_spec=grid_spec,
        out_shape=jax.ShapeDtypeStruct((rows, D_FF), bf16),
        compiler_params=_params("arbitrary", "arbitrary"),
        name="moe_gate_up",
    )(*visits, xs, w_gu, b_gu3, b_gu3)


def _moe_down_kernel(tv_ref, ev_ref, lo_ref, hi_ref, run_ref, nxt_ref, nrun_ref, a_ref, w_hbm, b_ref,
                     o_ref, wf_ref, wb_ref, sem):
    v = pl.program_id(0)
    lo = lo_ref[v]
    hi = hi_ref[v]

    def copies(e, sweep, slot):
        return [pltpu.make_async_copy(w_hbm.at[e], wf_ref.at[slot], sem.at[slot])]

    def on_arrival(slot):
        wb_ref[...] = wf_ref[slot].astype(bf16)

    _expert_weights_step(copies, _new_expert(ev_ref, v), run_ref[v], nrun_ref[0], ev_ref[v], nxt_ref[v], ev_ref[0],
                         0, 1, on_arrival)

    @pl.when(hi > lo)
    def _():
        @pl.when(lo == 0)
        def _():
            _store_slabs(o_ref, _dot(a_ref[...], wb_ref[...]) + b_ref[...])

        @pl.when(lo > 0)
        def _():
            r = lax.broadcasted_iota(i32, (a_ref.shape[0], LANES), 0)
            _store_slabs(o_ref, _dot(a_ref[...], wb_ref[...]) + b_ref[...], mask=(r >= lo) & (r < hi))


def _moe_down(act, w_dn, b_dn3, visits, *, tm):
    rows, dff = act.shape
    d = w_dn.shape[2]
    g = d // LANES
    grid_spec = pltpu.PrefetchScalarGridSpec(
        num_scalar_prefetch=len(visits),
        grid=(visits[0].shape[0],),
        in_specs=[
            pl.BlockSpec((tm, dff), lambda v, tv, *_: (tv[v], 0)),
            pl.BlockSpec(memory_space=pl.ANY),
            pl.BlockSpec((None, 1, d), lambda v, tv, ev, *_: (ev[v], 0, 0)),
        ],
        out_specs=pl.BlockSpec((tm * g, LANES), lambda v, tv, *_: (tv[v], 0)),
        scratch_shapes=[pltpu.VMEM((2, dff, d), f32), pltpu.VMEM((dff, d), bf16), pltpu.SemaphoreType.DMA((2,))],
    )
    return pl.pallas_call(
        _moe_down_kernel,
        grid_spec=grid_spec,
        out_shape=jax.ShapeDtypeStruct((rows * g, LANES), f32),
        compiler_params=_params("arbitrary"),
        name="moe_down",
    )(*visits, act, w_dn, b_dn3)


def _moe_combine_kernel(pos_ref, y_ref, tw_ref, ys_ref, oa_ref, ob_ref, buf_ref, sem, *, rows, g, tiles_a):
    i = pl.program_id(0)
    slot = i % 2
    n_tok = pl.num_programs(0) * rows

    def issue(tile, s):
        for k in range(TOP_K):
            _issue_slab_gather(pos_ref, k * n_tok + tile * rows, ys_ref, buf_ref.at[s, k], sem.at[s], rows, g)

    @pl.when(i == 0)
    def _():
        issue(0, 0)

    @pl.when(i + 1 < pl.num_programs(0))
    def _():
        issue(i + 1, 1 - slot)

    for k in range(TOP_K):
        _wait_slab_gather(ys_ref, buf_ref.at[slot, k], sem.at[slot])
    tw = tw_ref[...]

    def combine(o_ref):
        for c in range(g):
            cs = slice(c * LANES, (c + 1) * LANES)
            acc = y_ref[:, cs]
            for k in range(TOP_K):
                acc = acc + _load_slab_group(buf_ref, (slot, k), c, rows, g) * tw[:, k:k + 1]
            o_ref[:, cs] = acc

    pl.when(i < tiles_a)(lambda: combine(oa_ref))
    pl.when(i >= tiles_a)(lambda: combine(ob_ref))


def _moe_combine(y1, ys, pos_kmajor, tw, *, rows, n_a):
    n, d = y1.shape
    g = d // LANES
    tiles_a = n_a // rows
    tiles = n // rows
    grid_spec = pltpu.PrefetchScalarGridSpec(
        num_scalar_prefetch=1,
        grid=(tiles,),
        in_specs=[pl.BlockSpec((rows, d), lambda i, pos: (i, 0)),
                  pl.BlockSpec((rows, LANES), lambda i, pos: (i, 0)),
                  pl.BlockSpec(memory_space=pl.ANY)],
        out_specs=[pl.BlockSpec((rows, d), lambda i, pos: (jnp.minimum(i, tiles_a - 1), 0)),
                   pl.BlockSpec((rows, d), lambda i, pos: (jnp.maximum(i - tiles_a, 0), 0))],
        scratch_shapes=[pltpu.VMEM((2, TOP_K, rows * g, LANES), f32), pltpu.SemaphoreType.DMA((2,))],
    )
    return pl.pallas_call(
        functools.partial(_moe_combine_kernel, rows=rows, g=g, tiles_a=tiles_a),
        grid_spec=grid_spec,
        out_shape=[jax.ShapeDtypeStruct((n_a, d), f32), jax.ShapeDtypeStruct((n - n_a, d), f32)],
        compiler_params=_params("arbitrary"),
        name="moe_combine",
    )(pos_kmajor, y1, tw, ys)


def _sort_rows(top_i):
    e = top_i.reshape(-1)
    a = e.shape[0]
    order = jnp.argsort(e, stable=True).astype(i32)
    tok_sorted = order // TOP_K
    blk = LANES
    onehot = (e[:, None] == jnp.arange(N_EXPERTS, dtype=i32)[None, :]).astype(f32).reshape(a // blk, blk, N_EXPERTS)
    earlier = (jnp.arange(blk)[:, None] > jnp.arange(blk)[None, :]).astype(f32)
    within = jnp.einsum("ij,bjk->bik", earlier, onehot)
    block_sums = jnp.sum(onehot, axis=1)
    block_off = jnp.cumsum(block_sums, axis=0) - block_sums
    counts = jnp.sum(block_sums, axis=0)
    starts = jnp.cumsum(counts) - counts
    pos = jnp.sum(onehot * (within + block_off[:, None, :] + starts[None, None, :]), axis=2).reshape(a)
    return tok_sorted, pos.astype(i32), counts.astype(i32)


def _visits(counts, n_rows, *, tm):
    ends = jnp.cumsum(counts)
    starts = ends - counts
    first_tile = starts // tm
    last_tile = jnp.maximum(ends - 1, 0) // tm
    n_vis = jnp.where(counts > 0, last_tile - first_tile + 1, 0)
    vis_end = jnp.cumsum(n_vis)
    vis_start = vis_end - n_vis
    total = vis_end[-1]
    v = jnp.arange(n_rows // tm + N_EXPERTS, dtype=i32)
    vc = jnp.minimum(v, total - 1)
    ev = jnp.sum((vis_end[None, :] <= vc[:, None]).astype(i32), axis=1)
    tv = first_tile[ev] + (vc - vis_start[ev])
    lo = jnp.maximum(starts[ev], tv * tm) - tv * tm
    hi = jnp.minimum(ends[ev], (tv + 1) * tm) - tv * tm
    valid = v < total
    experts = jnp.arange(N_EXPERTS, dtype=i32)
    present = n_vis > 0
    n_runs = jnp.sum(present.astype(i32))
    run = jnp.sum((present[None, :] & (experts[None, :] < ev[:, None])).astype(i32), axis=1)
    rank = jnp.cumsum(present.astype(i32))
    run_expert = jnp.sum((rank[None, :] <= experts[:, None]).astype(i32), axis=1)
    nxt = run_expert[jnp.minimum(run + 1, n_runs - 1)]
    return tv, ev, jnp.where(valid, lo, 0), jnp.where(valid, hi, 0), run, nxt, n_runs[None]


GATE_UP_TM = 256
GATE_UP_TF = 1024
DOWN_TM = 256
GATHER_ROWS = 256
COMBINE_ROWS = 128


def _moe(y1, h_slabs, top_i, top_w, w_gu, b_gu, w_dn, b_dn, *, n_a):
    n_tok = y1.shape[0]
    tok_sorted, pos, counts = _sort_rows(top_i[:, :TOP_K])
    n_rows = tok_sorted.shape[0]
    xs = _gather_x(h_slabs, tok_sorted, rows=GATHER_ROWS, g=SLAB_G)
    act = _moe_gate_up(xs, w_gu, b_gu[:, None, :], _visits(counts, n_rows, tm=GATE_UP_TM),
                       tm=GATE_UP_TM, tf=GATE_UP_TF)
    ys = _moe_down(act, w_dn, b_dn[:, None, :], _visits(counts, n_rows, tm=DOWN_TM), tm=DOWN_TM)
    pos_kmajor = pos.reshape(n_tok, TOP_K).T.reshape(-1)
    return _moe_combine(y1, ys, pos_kmajor, top_w, rows=COMBINE_ROWS, n_a=n_a)


def kernel(x_prompt, x_sample, cache_moba_k, cache_moba_v, state_gdn, state_gdn_conv, cache_mem_k, cache_mem_v, page_table, mem_prompt, rel_bias, norm_mix, w_in, conv_w, a_log, dt_bias, gdn_o_norm, moba_q_norm, moba_k_norm, mem_q_norm, mem_norm, w_mem_kv, mem_k_norm, w_out, norm_ffn, router_w, router_b, w_gu, b_gu, w_dn, b_dn):
    assert x_prompt.shape[0] == 1 and all(a.shape[0] == 1 for a in (w_in, w_out, w_gu, w_dn, state_gdn))
    n_p = x_prompt.shape[1]
    n_seq, n_st = x_sample.shape[:2]
    n_s = n_seq * n_st
    d = x_prompt.shape[2]
    n_blk = n_p // MOBA_BLOCK
    xs = (x_prompt.reshape(n_p, d), x_sample.reshape(n_s, d))

    w = w_in[0]
    off_beta = A_W
    off_moba = off_beta + 2 * GDN_HEADS
    off_mem = off_moba + 3 * MOBA_W
    w_a = w[:, :A_W].astype(bf16)
    w_b = jnp.concatenate(
        [w[:, off_moba:off_moba + MOBA_W], w[:, off_moba + MOBA_W:off_moba + 2 * MOBA_W], w[:, off_mem:],
         w[:, off_moba + 2 * MOBA_W:off_mem], w[:, off_beta:off_moba],
         jnp.zeros((d, LANES - 2 * GDN_HEADS), w.dtype)], axis=1).astype(bf16)
    gain_b = jnp.concatenate([jnp.tile(moba_q_norm[0], MOBA_HEADS), jnp.tile(moba_k_norm[0], MOBA_HEADS),
                              jnp.tile(mem_q_norm[0], MEM_HEADS)])[None]
    proj_a = _norm_matmul(xs, norm_mix, w_a, gain_b, tm=512, tn=1024, n_norm=0)
    proj_b, mk_p, mk_s, mv_p, mv_s = _norm_matmul(
        xs, norm_mix, w_b, gain_b, tm=256, tn=B_W, n_norm=B_NORM_GROUPS, head_outs=(B_MK // LANES, B_MV // LANES))
    proj_b3 = proj_b[n_p:].reshape(n_seq, n_st, B_W)

    pad_row = lambda a: jnp.zeros((1, LANES), f32).at[0, GDN_HEADS:2 * GDN_HEADS].set(a[0])
    alog_row, dtb_row = pad_row(a_log), pad_row(dt_bias)
    o_gdn_p, p_gdn = _gdn_prompt(proj_a, proj_b, conv_w[0], alog_row, dtb_row, gdn_o_norm, n_tok=n_p)
    p_conv = proj_a[n_p - (GDN_CONV - 1):n_p, :GDN_CONV_DIM]
    xa_s = proj_a[n_p:].reshape(n_seq, n_st, A_W)
    o_gdn_s, s_gdn = _gdn_sample(jnp.swapaxes(xa_s, 0, 1), jnp.swapaxes(proj_b3[..., B_BA:], 0, 1),
                                 jnp.swapaxes(state_gdn_conv[0], 0, 1), conv_w[0], alog_row, dtb_row,
                                 gdn_o_norm, state_gdn[0])
    o_gdn_s = jnp.swapaxes(o_gdn_s, 0, 1).reshape(n_s, GDN_V).astype(bf16)
    s_conv = jnp.concatenate([state_gdn_conv[0], xa_s[..., :GDN_CONV_DIM]], axis=1)[:, n_st:]

    kmean = _block_mean(proj_b, n_blk).reshape(n_blk, MOBA_W)
    sel = _moba_select(proj_b, kmean, n_blk)
    o_moba_p = _moba_prompt(proj_b, sel, rel_bias, n_blk)
    pairs = lambda col: proj_b3[..., col:col + MOBA_W].reshape(n_seq, n_st * MOBA_HEADS, HEAD_DIM)
    pair_rows = n_st * MOBA_HEADS
    o_moba_s = _moba_sample(pairs(B_MQ), mk_s.reshape(n_seq, pair_rows, HEAD_DIM),
                            mv_s.reshape(n_seq, pair_rows, HEAD_DIM), cache_moba_k.reshape(-1, HEAD_DIM),
                            cache_moba_v.reshape(-1, HEAD_DIM), page_table, rel_bias, page=cache_moba_k.shape[2])
    o_moba_s = o_moba_s.reshape(n_s, MOBA_W).astype(bf16)

    n_mem = mem_prompt.shape[1]
    mem_kv = _norm_matmul((mem_prompt[0],), mem_norm, w_mem_kv[0].astype(bf16),
                          jnp.tile(mem_k_norm[0], MEM_HEADS)[None], tm=n_mem, tn=2 * MEM_W, n_norm=MEM_HEADS)
    o_mem_p = _mem_attn_prompt(proj_b, mem_kv, n_tok=n_p, tq=512)
    o_mem_s = _mem_attn_sample(pairs(B_CQ), cache_mem_k.reshape(n_seq, n_mem * MEM_HEADS, HEAD_DIM),
                               cache_mem_v.reshape(n_seq, n_mem * MEM_HEADS, HEAD_DIM))
    o_mem_s = o_mem_s.reshape(n_s, MEM_W).astype(bf16)

    rw = jnp.pad(router_w[0], ((0, 0), (0, LANES - N_EXPERTS)))
    rb = jnp.pad(router_b, ((0, 0), (0, LANES - N_EXPERTS)))
    y1, h2, top_i, top_w = _outproj_router(
        xs, (o_gdn_p, o_gdn_s), (o_moba_p, o_moba_s), (o_mem_p, o_mem_s),
        w_out[0].astype(bf16), norm_ffn, rw, rb, tm=256)
    y_p, y_s = _moe(y1, h2, top_i, top_w, w_gu[0], b_gu[0], w_dn[0], b_dn[0], n_a=n_p)

    heads = lambda a, lead: a.reshape(lead + (MOBA_HEADS, HEAD_DIM))
    return (y_p.reshape(1, n_p, d), y_s.reshape(n_seq, n_st, d),
            heads(mk_p, (1, 1, n_p)), heads(mv_p, (1, 1, n_p)),
            p_gdn[None, None], p_conv[None, None],
            heads(mem_kv[:, :MEM_W], (1, 1, n_mem)), heads(mem_kv[:, MEM_W:], (1, 1, n_mem)),
            heads(mk_s, (1, n_seq, n_st)), heads(mv_s, (1, n_seq, n_st)),
            s_gdn[None], s_conv[None])
```

```python
import functools
import math

import numpy as np
import jax
import jax.numpy as jnp
from jax import lax
from jax.experimental import pallas as pl
from jax.experimental.pallas import tpu as pltpu

f32 = jnp.float32
bf16 = jnp.bfloat16
i32 = jnp.int32
HIGHEST = lax.Precision.HIGHEST

LANES = 128
SUBLANES = 8
VMEM_LIMIT = 60 * 1024 * 1024

D_MODEL = 2048
HEAD_DIM = 128
GDN_HEADS = 8
GDN_CONV = 4
GDN_CHUNK = 64
GDN_QK = GDN_HEADS * HEAD_DIM
GDN_V = GDN_HEADS * HEAD_DIM
GDN_CONV_DIM = 2 * GDN_QK + GDN_V
MOBA_HEADS = 4
MOBA_BLOCK = 256
MOBA_TOPK = 3
MOBA_W = MOBA_HEADS * HEAD_DIM
MEM_HEADS = 4
MEM_W = MEM_HEADS * HEAD_DIM
N_BUCKETS = 32
MAX_DISTANCE = 128
N_EXPERTS = 32
TOP_K = 4
D_FF = D_MODEL
SWIGLU_LIMIT = 7.0
SWIGLU_ALPHA = 1.702
EPS = 1e-6
NEG_INF = -1e30

A_W = GDN_CONV_DIM + GDN_V
B_MQ, B_MK, B_CQ, B_MV, B_BA = 0, MOBA_W, 2 * MOBA_W, 3 * MOBA_W, 4 * MOBA_W
B_W = B_BA + LANES
B_NORM_GROUPS = 3 * MOBA_HEADS


def _params(*sem):
    return pltpu.CompilerParams(dimension_semantics=sem, vmem_limit_bytes=VMEM_LIMIT)


def _dot(a, b, **kw):
    return jnp.dot(a, b, preferred_element_type=f32, **kw)


def _dot_nt(a, b, **kw):
    return lax.dot_general(a, b, (((1,), (1,)), ((), ())), preferred_element_type=f32, **kw)


def _dot_tn(a, b, **kw):
    return lax.dot_general(a, b, (((0,), (0,)), ((), ())), preferred_element_type=f32, **kw)


def _rms(x, gain):
    return x * lax.rsqrt(jnp.mean(x * x, axis=-1, keepdims=True) + EPS) * gain


def _silu(x):
    return x * jax.nn.sigmoid(x)


def _stacked_specs(arrays, tm, grid_rank):
    counts = [a.shape[0] // tm for a in arrays]
    starts = [sum(counts[:k]) for k in range(len(arrays))]

    def spec(k):
        def index_map(i, *_):
            return (jnp.clip(i - starts[k], 0, counts[k] - 1), 0)
        return pl.BlockSpec((tm, arrays[k].shape[1]), index_map)

    return [spec(k) for k in range(len(arrays))], starts, sum(counts)


def _stacked_tile(refs, starts, i):
    x = refs[0][...]
    for ref, start in zip(refs[1:], starts[1:]):
        x = jnp.where(i >= start, ref[...], x)
    return x


def _norm_matmul_kernel(*refs, n_norm, starts, head_outs, n_heads):
    n_x = len(starts)
    x_refs = refs[:n_x]
    g_ref, w_ref, gain_ref, o_ref = refs[n_x:n_x + 4]
    pair_refs = refs[n_x + 4:-1]
    h_ref = refs[-1]
    i = pl.program_id(0)

    @pl.when(pl.program_id(1) == 0)
    def _():
        h_ref[...] = _rms(_stacked_tile(x_refs, starts, i), g_ref[...]).astype(bf16)

    acc = _dot(h_ref[...], w_ref[...])
    if n_norm == 0 and not head_outs:
        o_ref[...] = acc
        return
    tm = acc.shape[0]
    blks = []
    for gi in range(acc.shape[1] // LANES):
        sl = slice(gi * LANES, (gi + 1) * LANES)
        blk = acc[:, sl]
        if gi < n_norm:
            blk = _rms(blk, gain_ref[:, sl])
        o_ref[:, sl] = blk
        blks.append(blk)
    bounds = list(starts[1:]) + [None]
    for e, first in enumerate(head_outs):
        for k in range(n_x):
            ref = pair_refs[e * n_x + k]
            mine = (i >= starts[k]) if bounds[k] is None else ((i >= starts[k]) & (i < bounds[k]))

            @pl.when(mine)
            def _(ref=ref, first=first):
                for h in range(n_heads):
                    ref[pl.ds(h, tm, stride=n_heads), :] = blks[first + h]


def _norm_matmul(xs, g, w, gain, *, tm, tn, n_norm, head_outs=(), n_heads=MOBA_HEADS):
    d = xs[0].shape[1]
    wn = w.shape[1]
    assert not head_outs or tn == wn
    x_specs, starts, n_tiles = _stacked_specs(xs, tm, 2)
    counts = [x.shape[0] // tm for x in xs]
    out_specs = [pl.BlockSpec((tm, tn), lambda i, j: (i, j))]
    out_shape = [jax.ShapeDtypeStruct((n_tiles * tm, wn), f32)]
    for _ in head_outs:
        for k, x in enumerate(xs):
            out_specs.append(pl.BlockSpec(
                (tm * n_heads, LANES), lambda i, j, k=k: (jnp.clip(i - starts[k], 0, counts[k] - 1), 0)))
            out_shape.append(jax.ShapeDtypeStruct((x.shape[0] * n_heads, LANES), f32))
    outs = pl.pallas_call(
        functools.partial(_norm_matmul_kernel, n_norm=n_norm, starts=starts, head_outs=tuple(head_outs),
                          n_heads=n_heads),
        grid=(n_tiles, wn // tn),
        in_specs=x_specs + [
            pl.BlockSpec((1, d), lambda i, j: (0, 0)),
            pl.BlockSpec((d, tn), lambda i, j: (0, j)),
            pl.BlockSpec((1, gain.shape[1]), lambda i, j: (0, 0)),
        ],
        out_specs=out_specs,
        out_shape=out_shape,
        scratch_shapes=[pltpu.VMEM((tm, d), bf16)],
        compiler_params=_params("arbitrary", "arbitrary"),
        name="norm_matmul",
    )(*xs, g, w, gain)
    return outs if head_outs else outs[0]


INV_BLOCK = 16


def _split_bf16(x):
    hi = x.astype(bf16)
    return hi, (x - hi.astype(f32)).astype(bf16)


def _dot3(a, b, dot=_dot):
    a_hi, a_lo = a
    b_hi, b_lo = b
    return dot(a_hi, b_hi) + (dot(a_hi, b_lo) + dot(a_lo, b_hi))


def _unit_lower_inverses(lmats):
    c = lmats[0].shape[0]
    row = lax.broadcasted_iota(i32, (c, c), 0)
    col = lax.broadcasted_iota(i32, (c, c), 1)
    same = lambda n: (row // n) == (col // n)
    eye = jnp.where(row == col, 1.0, 0.0).astype(f32)
    ms = [jnp.where(same(INV_BLOCK), l, 0.0) for l in lmats]
    ps = [eye - m for m in ms]
    power = 1
    while 2 * power < INV_BLOCK:
        sm = [_split_bf16(m) for m in ms]
        ms = [_dot3(s, s) for s in sm]
        sm = [_split_bf16(m) for m in ms]
        ps = [p + _dot3(_split_bf16(p), s) for p, s in zip(ps, sm)]
        power *= 2
    n = 2 * INV_BLOCK
    while n <= c:
        offs = [_split_bf16(jnp.where(same(n) & ~same(n // 2), l, 0.0)) for l in lmats]
        sp = [_split_bf16(p) for p in ps]
        mids = [_dot3(o, s) for o, s in zip(offs, sp)]
        ps = [p - _dot3(s, _split_bf16(m)) for p, s, m in zip(ps, sp, mids)]
        n *= 2
    return ps


def _gdn_prompt_kernel(qkv_ref, z_ref, ba_ref, convw_ref, alog_ref, dtb_ref, onorm_ref,
                       o_ref, s_out_ref, xp_ref, s_ref):
    c = GDN_CHUNK
    step = pl.program_id(0)

    @pl.when(step == 0)
    def _():
        xp_ref[0:SUBLANES, :] = jnp.zeros((SUBLANES, GDN_CONV_DIM), f32)
        s_ref[...] = jnp.zeros_like(s_ref)

    xp_ref[SUBLANES:SUBLANES + c, :] = qkv_ref[...]
    w = convw_ref[...]
    y = None
    for j in range(GDN_CONV):
        lo = SUBLANES - (GDN_CONV - 1) + j
        term = xp_ref[lo:lo + c, :] * w[j:j + 1, :]
        y = term if y is None else y + term
    y = _silu(y)
    xp_ref[0:SUBLANES, :] = xp_ref[c:c + SUBLANES, :]

    ba = ba_ref[...]
    beta_all = jax.nn.sigmoid(ba)
    g_all = -jnp.exp(alog_ref[...]) * jax.nn.softplus(ba + dtb_ref[...])
    row = lax.broadcasted_iota(i32, (c, c), 0)
    col = lax.broadcasted_iota(i32, (c, c), 1)
    incl = row >= col
    strict = row > col
    gcum_all = _dot(jnp.where(incl, 1.0, 0.0).astype(f32), g_all, precision=HIGHEST)
    gcum_t = gcum_all.T
    onorm = onorm_ref[...]

    heads = range(GDN_HEADS)
    head_cols = lambda off, h: slice(off + h * HEAD_DIM, off + (h + 1) * HEAD_DIM)
    qs = [y[:, head_cols(0, h)] for h in heads]
    ks = [y[:, head_cols(GDN_QK, h)] for h in heads]
    vs = [y[:, head_cols(2 * GDN_QK, h)] for h in heads]
    qs = [q * lax.rsqrt(jnp.sum(q * q, axis=-1, keepdims=True) + EPS) * (HEAD_DIM ** -0.5) for q in qs]
    ks = [k * lax.rsqrt(jnp.sum(k * k, axis=-1, keepdims=True) + EPS) for k in ks]
    betas = [beta_all[:, h:h + 1] for h in heads]
    gcs = [gcum_all[:, GDN_HEADS + h:GDN_HEADS + h + 1] for h in heads]
    grs = [gcum_t[GDN_HEADS + h:GDN_HEADS + h + 1, :] for h in heads]
    g_lasts = [gcum_all[c - 1:c, GDN_HEADS + h:GDN_HEADS + h + 1] for h in heads]
    decays = [jnp.exp(jnp.where(incl, gc - gr, -jnp.inf)) for gc, gr in zip(gcs, grs)]
    kbetas = [k * b for k, b in zip(ks, betas)]
    lmats = [_dot_nt(kb, k) * jnp.where(strict, dc, 0.0) for kb, k, dc in zip(kbetas, ks, decays)]
    tinvs = _unit_lower_inverses(lmats)
    egcs = [jnp.exp(gc) for gc in gcs]
    us = [_dot(t, v * b) for t, v, b in zip(tinvs, vs, betas)]
    ws = [_dot(t, kb * e) for t, kb, e in zip(tinvs, kbetas, egcs)]
    intras = [_dot_nt(q, k) * dc for q, k, dc in zip(qs, ks, decays)]
    kdecs = [k * jnp.exp(gl - gc) for k, gl, gc in zip(ks, g_lasts, gcs)]
    ss = [s_ref[h] for h in heads]
    v_news = [u - _dot(w_, s) for u, w_, s in zip(us, ws, ss)]
    os_ = [_dot(q * e, s) + _dot(it, vn) for q, e, s, it, vn in zip(qs, egcs, ss, intras, v_news)]
    for h in heads:
        s_ref[h] = ss[h] * jnp.exp(g_lasts[h]) + _dot_tn(kdecs[h], v_news[h])
    for h in heads:
        hs = head_cols(0, h)
        o_ref[:, hs] = (_rms(os_[h], onorm) * _silu(z_ref[:, hs])).astype(o_ref.dtype)

    @pl.when(step == pl.num_programs(0) - 1)
    def _():
        s_out_ref[...] = s_ref[...]


def _gdn_prompt(proj_a, proj_b, conv_w, alog_row, dtb_row, onorm, *, n_tok):
    c = GDN_CHUNK
    zblk = GDN_CONV_DIM // GDN_V
    return pl.pallas_call(
        _gdn_prompt_kernel,
        grid=(n_tok // c,),
        in_specs=[
            pl.BlockSpec((c, GDN_CONV_DIM), lambda i: (i, 0)),
            pl.BlockSpec((c, GDN_V), lambda i: (i, zblk)),
            pl.BlockSpec((c, LANES), lambda i: (i, B_BA // LANES)),
            pl.BlockSpec((GDN_CONV, GDN_CONV_DIM), lambda i: (0, 0)),
            pl.BlockSpec((1, LANES), lambda i: (0, 0)),
            pl.BlockSpec((1, LANES), lambda i: (0, 0)),
            pl.BlockSpec((1, HEAD_DIM), lambda i: (0, 0)),
        ],
        out_specs=[
            pl.BlockSpec((c, GDN_V), lambda i: (i, 0)),
            pl.BlockSpec((GDN_HEADS, HEAD_DIM, HEAD_DIM), lambda i: (0, 0, 0)),
        ],
        out_shape=[
            jax.ShapeDtypeStruct((n_tok, GDN_V), bf16),
            jax.ShapeDtypeStruct((GDN_HEADS, HEAD_DIM, HEAD_DIM), f32),
        ],
        scratch_shapes=[
            pltpu.VMEM((c + SUBLANES, GDN_CONV_DIM), f32),
            pltpu.VMEM((GDN_HEADS, HEAD_DIM, HEAD_DIM), f32),
        ],
        compiler_params=_params("arbitrary"),
        name="gdn_prompt",
    )(proj_a, proj_a, proj_b, conv_w, alog_row, dtb_row, onorm)


GDN_S_SEQ = SUBLANES


def _gdn_sample_kernel(q_ref, k_ref, v_ref, z_ref, ba_ref, bq_ref, bk_ref, bv_ref, wq_ref, wk_ref, wv_ref,
                       alog_ref, dtb_ref, onorm_ref, s_ref, o_ref, so_ref, *, n_tok):
    h = pl.program_id(1)
    nb = GDN_S_SEQ
    lane = lax.broadcasted_iota(i32, (nb, LANES), 1)

    def conv(x_ref, buf_ref, w_ref, t):
        y = None
        for j in range(GDN_CONV):
            i = t + j
            row = buf_ref[i] if i < GDN_CONV - 1 else x_ref[i - (GDN_CONV - 1)]
            term = row * w_ref[j:j + 1, :]
            y = term if y is None else y + term
        return _silu(y)

    def lane_col(x, idx):
        return jnp.sum(jnp.where(lane == idx, x, 0.0), axis=1, keepdims=True)

    qs, ks, vs, betas, egs = [], [], [], [], []
    for t in range(n_tok):
        q = conv(q_ref, bq_ref, wq_ref, t)
        k = conv(k_ref, bk_ref, wk_ref, t)
        q = q * lax.rsqrt(jnp.sum(q * q, axis=-1, keepdims=True) + EPS) * (HEAD_DIM ** -0.5)
        k = k * lax.rsqrt(jnp.sum(k * k, axis=-1, keepdims=True) + EPS)
        qs.append(q.T)
        ks.append(k.T)
        vs.append(conv(v_ref, bv_ref, wv_ref, t))
        ba = ba_ref[t]
        betas.append(lane_col(jax.nn.sigmoid(ba), h))
        g = -jnp.exp(alog_ref[...]) * jax.nn.softplus(ba + dtb_ref[...])
        egs.append(jnp.exp(lane_col(g, GDN_HEADS + h)))

    o_rows = [[None] * nb for _ in range(n_tok)]
    for b in range(nb):
        s = s_ref[b]
        for t in range(n_tok):
            kc = ks[t][:, b:b + 1]
            qc = qs[t][:, b:b + 1]
            s = s * egs[t][b:b + 1, :]
            ks_row = jnp.sum(kc * s, axis=0, keepdims=True)
            delta = (vs[t][b:b + 1, :] - ks_row) * betas[t][b:b + 1, :]
            s = s + kc * delta
            o_rows[t][b] = jnp.sum(qc * s, axis=0, keepdims=True)
        so_ref[b] = s
    for t in range(n_tok):
        o = jnp.concatenate(o_rows[t], axis=0)
        o_ref[t] = _rms(o, onorm_ref[...]) * _silu(z_ref[t])


def _gdn_sample(xa, ba, conv_buf, conv_w, alog_row, dtb_row, onorm, state):
    n_tok, n_seq, _ = xa.shape
    nb = GDN_S_SEQ
    hq, hk, hv, hz = 0, GDN_HEADS, 2 * GDN_HEADS, 3 * GDN_HEADS

    def xspec(off, rows):
        return pl.BlockSpec((rows, nb, HEAD_DIM), lambda i, h: (0, i, off + h))

    def wspec(off):
        return pl.BlockSpec((GDN_CONV, HEAD_DIM), lambda i, h: (0, off + h))

    row = pl.BlockSpec((1, LANES), lambda i, h: (0, 0))
    sspec = pl.BlockSpec((nb, None, HEAD_DIM, HEAD_DIM), lambda i, h: (i, h, 0, 0))
    return pl.pallas_call(
        functools.partial(_gdn_sample_kernel, n_tok=n_tok),
        grid=(n_seq // nb, GDN_HEADS),
        in_specs=[xspec(hq, n_tok), xspec(hk, n_tok), xspec(hv, n_tok), xspec(hz, n_tok),
                  pl.BlockSpec((n_tok, nb, LANES), lambda i, h: (0, i, 0)),
                  xspec(hq, GDN_CONV - 1), xspec(hk, GDN_CONV - 1), xspec(hv, GDN_CONV - 1),
                  wspec(hq), wspec(hk), wspec(hv), row, row, row, sspec],
        out_specs=[pl.BlockSpec((n_tok, nb, HEAD_DIM), lambda i, h: (0, i, h)), sspec],
        out_shape=[jax.ShapeDtypeStruct((n_tok, n_seq, GDN_V), f32),
                   jax.ShapeDtypeStruct(state.shape, f32)],
        compiler_params=_params("parallel", "arbitrary"),
        name="gdn_sample",
    )(xa, xa, xa, xa, ba, conv_buf, conv_buf, conv_buf, conv_w, conv_w, conv_w, alog_row, dtb_row, onorm, state)


def _bucket_thresholds():
    exact = N_BUCKETS // 2
    d = np.arange(0, 2 * MAX_DISTANCE)
    val = np.log(np.maximum(d, 1).astype(np.float64) / exact) / math.log(MAX_DISTANCE / exact) * (N_BUCKETS - exact)
    frac = np.abs(val - np.round(val))[exact + 1:MAX_DISTANCE]
    assert frac.min() > 1e-3, "a bucket boundary sits on an integer distance"
    bucket = np.where(d < exact, d, np.minimum(exact + val.astype(np.int64), N_BUCKETS - 1))
    assert np.all(np.diff(bucket) >= 0)
    return [int(np.argmax(bucket >= b)) for b in range(N_BUCKETS)]


_BUCKET_THR = _bucket_thresholds()


def _bias_from_dist(dist, rb_ref, h):
    v = jnp.full(dist.shape, rb_ref[0, h], f32)
    for b in range(1, N_BUCKETS):
        v = jnp.where(dist >= _BUCKET_THR[b], rb_ref[b, h], v)
    return v


def _topk_mask(s, blk, k):
    nblk = s.shape[1]
    sel = jnp.zeros(s.shape, f32)
    for _ in range(k):
        m = jnp.max(s, axis=1, keepdims=True)
        cand = jnp.where((s == m) & (m > -jnp.inf), blk, nblk)
        pick = blk == jnp.min(cand, axis=1, keepdims=True)
        sel = jnp.where(pick, 1.0, sel)
        s = jnp.where(pick, -jnp.inf, s)
    return sel


def _block_mean_kernel(k_ref, o_ref):
    o_ref[0] = jnp.mean(k_ref[...], axis=0, keepdims=True)


def _block_mean(proj_b, n_blk):
    return pl.pallas_call(
        _block_mean_kernel,
        grid=(n_blk,),
        in_specs=[pl.BlockSpec((MOBA_BLOCK, MOBA_W), lambda i: (i, B_MK // MOBA_W))],
        out_specs=pl.BlockSpec((1, 1, MOBA_W), lambda i: (i, 0, 0)),
        out_shape=jax.ShapeDtypeStruct((n_blk, 1, MOBA_W), f32),
        compiler_params=_params("parallel"),
        name="moba_block_mean",
    )(proj_b)


def _moba_select_kernel(q_ref, km_ref, o_ref):
    own = pl.program_id(0)
    nblk = km_ref.shape[0]
    blk = lax.broadcasted_iota(i32, (MOBA_BLOCK, nblk), 1)
    outs = []
    for h in range(MOBA_HEADS):
        hs = slice(h * HEAD_DIM, (h + 1) * HEAD_DIM)
        s = _dot3(_split_bf16(q_ref[:, hs]), _split_bf16(km_ref[:, hs]), dot=_dot_nt)
        s = jnp.where(blk < own, s, -jnp.inf)
        outs.append(_topk_mask(s, blk, MOBA_TOPK))
    pad = LANES - MOBA_HEADS * nblk
    if pad:
        outs.append(jnp.zeros((MOBA_BLOCK, pad), f32))
    o_ref[...] = jnp.concatenate(outs, axis=1)


def _moba_select(proj_b, kmean, n_blk):
    assert n_blk * MOBA_HEADS <= LANES
    return pl.pallas_call(
        _moba_select_kernel,
        grid=(n_blk,),
        in_specs=[pl.BlockSpec((MOBA_BLOCK, MOBA_W), lambda i: (i, B_MQ // MOBA_W)),
                  pl.BlockSpec((n_blk, MOBA_W), lambda i: (0, 0))],
        out_specs=pl.BlockSpec((MOBA_BLOCK, LANES), lambda i: (i, 0)),
        out_shape=jax.ShapeDtypeStruct((n_blk * MOBA_BLOCK, LANES), f32),
        compiler_params=_params("parallel"),
        name="moba_select",
    )(proj_b, kmean)


def _moba_prompt_kernel(qi_ref, kj_ref, rb_ref, q_ref, k_ref, v_ref, sel_ref, o_ref,
                        bias_ref, m_ref, l_ref, acc_ref, *, n_blk):
    step = pl.program_id(0)
    qi = qi_ref[step]
    kj = kj_ref[step]
    nq = MOBA_BLOCK

    @pl.when(step == 0)
    def _():
        r = lax.broadcasted_iota(i32, (nq, nq), 0)
        c = lax.broadcasted_iota(i32, (nq, nq), 1)
        for h in range(MOBA_HEADS):
            bias_ref[h, 0] = jnp.where(c <= r, _bias_from_dist(r - c, rb_ref, h), NEG_INF)
            bias_ref[h, 1] = _bias_from_dist(r - c + nq, rb_ref, h)

    first = kj == qi

    @pl.when(first)
    def _():
        m_ref[...] = jnp.full(m_ref.shape, NEG_INF, f32)
        l_ref[...] = jnp.zeros_like(l_ref)
        acc_ref[...] = jnp.zeros_like(acc_ref)

    heads = range(MOBA_HEADS)
    hcols = [slice(h * HEAD_DIM, (h + 1) * HEAD_DIM) for h in heads]
    scale = HEAD_DIM ** -0.5

    def attend(far):
        sel = sel_ref[...]
        lane = lax.broadcasted_iota(i32, sel.shape, 1)
        ss = [_dot_nt(q_ref[:, hs].astype(bf16), k_ref[:, hs].astype(bf16)) for hs in hcols]
        picked = [jnp.max(jnp.where(lane == h * n_blk + kj, sel, 0.0), axis=1, keepdims=True) > 0.0 for h in heads]
        if far:
            cols = [jnp.where(pk, rb_ref[N_BUCKETS - 1, h], NEG_INF) for h, pk in zip(heads, picked)]
            ss = [s * scale + c for s, c in zip(ss, cols)]
        else:
            slot = jnp.where(first, 0, 1)
            ss = [s * scale + bias_ref[h, slot] for h, s in zip(heads, ss)]
            ss = [jnp.where(pk | first, s, NEG_INF) for pk, s in zip(picked, ss)]
        m_prevs = [m_ref[h] for h in heads]
        m_news = [jnp.maximum(mp, jnp.max(s, axis=1, keepdims=True)) for mp, s in zip(m_prevs, ss)]
        alphas = [jnp.exp(mp - mn) for mp, mn in zip(m_prevs, m_news)]
        ps = [jnp.exp(s - mn) for s, mn in zip(ss, m_news)]
        pvs = [_dot(p.astype(bf16), v_ref[:, hs].astype(bf16)) for p, hs in zip(ps, hcols)]
        for h in heads:
            l_ref[h] = alphas[h] * l_ref[h] + jnp.sum(ps[h], axis=1, keepdims=True)
            acc_ref[h] = alphas[h] * acc_ref[h] + pvs[h]
            m_ref[h] = m_news[h]

    is_far = kj < qi - 1
    pl.when(is_far)(lambda: attend(True))
    pl.when(jnp.logical_not(is_far))(lambda: attend(False))

    @pl.when(kj == 0)
    def _():
        for h in range(MOBA_HEADS):
            hs = slice(h * HEAD_DIM, (h + 1) * HEAD_DIM)
            o_ref[:, hs] = (acc_ref[h] / l_ref[h]).astype(o_ref.dtype)


def _moba_prompt(proj_b, sel, rel_bias, n_blk):
    qi = np.concatenate([np.full(i + 1, i) for i in range(n_blk)]).astype(np.int32)
    kj = np.concatenate([np.arange(i, -1, -1) for i in range(n_blk)]).astype(np.int32)
    nq = MOBA_BLOCK
    grid_spec = pltpu.PrefetchScalarGridSpec(
        num_scalar_prefetch=2,
        grid=(len(qi),),
        in_specs=[
            pl.BlockSpec(memory_space=pltpu.SMEM),
            pl.BlockSpec((nq, MOBA_W), lambda s, qi, kj: (qi[s], B_MQ // MOBA_W)),
            pl.BlockSpec((nq, MOBA_W), lambda s, qi, kj: (kj[s], B_MK // MOBA_W)),
            pl.BlockSpec((nq, MOBA_W), lambda s, qi, kj: (kj[s], B_MV // MOBA_W)),
            pl.BlockSpec((nq, LANES), lambda s, qi, kj: (qi[s], 0)),
        ],
        out_specs=pl.BlockSpec((nq, MOBA_W), lambda s, qi, kj: (qi[s], 0)),
        scratch_shapes=[
            pltpu.VMEM((MOBA_HEADS, 2, nq, nq), f32),
            pltpu.VMEM((MOBA_HEADS, nq, 1), f32),
            pltpu.VMEM((MOBA_HEADS, nq, 1), f32),
            pltpu.VMEM((MOBA_HEADS, nq, HEAD_DIM), f32),
        ],
    )
    return pl.pallas_call(
        functools.partial(_moba_prompt_kernel, n_blk=n_blk),
        grid_spec=grid_spec,
        out_shape=jax.ShapeDtypeStruct((n_blk * nq, MOBA_W), bf16),
        compiler_params=_params("arbitrary"),
        name="moba_prompt",
    )(jnp.asarray(qi), jnp.asarray(kj), rel_bias, proj_b, proj_b, proj_b, sel)


def _per_head_bias(dist, row_head, rb_ref):
    v = _bias_from_dist(dist, rb_ref, 0)
    for h in range(1, MOBA_HEADS):
        v = jnp.where(row_head == h, _bias_from_dist(dist, rb_ref, h), v)
    return v


def _moba_sample_kernel(pt_ref, rb_ref, q_ref, kn_ref, vn_ref, *rest, n_pages, page):
    kp = rest[:n_pages]
    vp = rest[n_pages:2 * n_pages]
    o_ref, bias_ref, biasn_ref = rest[2 * n_pages:]
    nh = MOBA_HEADS
    rows = q_ref.shape[0]
    prow = page * nh
    past = n_pages * page
    n_blk = past // MOBA_BLOCK
    ppb = MOBA_BLOCK // page

    @pl.when(pl.program_id(0) == 0)
    def _():
        r = lax.broadcasted_iota(i32, (rows, past * nh), 0)
        c = lax.broadcasted_iota(i32, (rows, past * nh), 1)
        bias = _per_head_bias(past + r // nh - c // nh, r % nh, rb_ref)
        bias_ref[...] = jnp.where(r % nh == c % nh, bias, NEG_INF)
        r = lax.broadcasted_iota(i32, (rows, rows), 0)
        c = lax.broadcasted_iota(i32, (rows, rows), 1)
        bias = _per_head_bias(r // nh - c // nh, r % nh, rb_ref)
        biasn_ref[...] = jnp.where((r % nh == c % nh) & (c // nh <= r // nh), bias, NEG_INF)

    kb, means = [], []
    for p in range(n_pages):
        kpage = kp[p][...]
        kb.append(kpage.astype(bf16))
        part = jnp.sum(kpage.reshape(prow // SUBLANES, SUBLANES, HEAD_DIM), axis=0)
        part = part[0:nh] + part[nh:2 * nh]
        if p % ppb == 0:
            means.append(part)
        else:
            means[-1] = means[-1] + part
    kmean = jnp.concatenate(means, axis=0) / MOBA_BLOCK

    q = q_ref[...]
    r = lax.broadcasted_iota(i32, (rows, n_blk * nh), 0)
    c = lax.broadcasted_iota(i32, (rows, n_blk * nh), 1)
    s = jnp.where(r % nh == c % nh, _dot_nt(q, kmean, precision=HIGHEST), -jnp.inf)
    sel = _topk_mask(s, c, MOBA_TOPK)
    picked = [jnp.max(jnp.where(c // nh == n, sel, 0.0), axis=1, keepdims=True) > 0.0 for n in range(n_blk)]

    qb = q.astype(bf16)
    scale = HEAD_DIM ** -0.5
    ln = _dot_nt(qb, kn_ref[...].astype(bf16)) * scale + biasn_ref[...]
    m = jnp.max(ln, axis=1, keepdims=True)
    lps = []
    for p in range(n_pages):
        lp = _dot_nt(qb, kb[p]) * scale + bias_ref[:, p * prow:(p + 1) * prow]
        lp = jnp.where(picked[p // ppb], lp, NEG_INF)
        m = jnp.maximum(m, jnp.max(lp, axis=1, keepdims=True))
        lps.append(lp)
    pn = jnp.exp(ln - m)
    den = jnp.sum(pn, axis=1, keepdims=True)
    num = _dot(pn.astype(bf16), vn_ref[...].astype(bf16))
    for p in range(n_pages):
        pp = jnp.exp(lps[p] - m)
        den = den + jnp.sum(pp, axis=1, keepdims=True)
        num = num + _dot(pp.astype(bf16), vp[p][...].astype(bf16))
    o_ref[...] = num / den


def _moba_sample(q, k_new, v_new, pool_k, pool_v, page_table, rel_bias, *, page):
    n_seq, rows, _ = q.shape
    n_pages = page_table.shape[1]
    prow = page * MOBA_HEADS
    new_spec = pl.BlockSpec((None, rows, HEAD_DIM), lambda b, pt: (b, 0, 0))

    def page_spec(p):
        return pl.BlockSpec((prow, HEAD_DIM), lambda b, pt: (pt[b, p], 0))

    grid_spec = pltpu.PrefetchScalarGridSpec(
        num_scalar_prefetch=1,
        grid=(n_seq,),
        in_specs=[pl.BlockSpec(memory_space=pltpu.SMEM), new_spec, new_spec, new_spec]
        + [page_spec(p) for p in range(n_pages)] * 2,
        out_specs=new_spec,
        scratch_shapes=[pltpu.VMEM((rows, n_pages * prow), f32), pltpu.VMEM((rows, rows), f32)],
    )
    return pl.pallas_call(
        functools.partial(_moba_sample_kernel, n_pages=n_pages, page=page),
        grid_spec=grid_spec,
        out_shape=jax.ShapeDtypeStruct((n_seq, rows, HEAD_DIM), f32),
        compiler_params=_params("arbitrary"),
        name="moba_sample",
    )(page_table, rel_bias, q, k_new, v_new, *([pool_k] * n_pages), *([pool_v] * n_pages))


def _mem_attn_kernel(q_ref, k_ref, v_ref, o_ref):
    for h in range(MEM_HEADS):
        hs = slice(h * HEAD_DIM, (h + 1) * HEAD_DIM)
        s = _dot_nt(q_ref[:, hs].astype(bf16), k_ref[:, hs].astype(bf16)) * (HEAD_DIM ** -0.5)
        p = jnp.exp(s - jnp.max(s, axis=1, keepdims=True))
        num = _dot(p.astype(bf16), v_ref[:, hs].astype(bf16))
        o_ref[:, hs] = (num / jnp.sum(p, axis=1, keepdims=True)).astype(o_ref.dtype)


def _mem_attn_prompt(proj_b, mem_kv, *, n_tok, tq):
    n_mem = mem_kv.shape[0]
    return pl.pallas_call(
        _mem_attn_kernel,
        grid=(n_tok // tq,),
        in_specs=[pl.BlockSpec((tq, MEM_W), lambda i: (i, B_CQ // MEM_W)),
                  pl.BlockSpec((n_mem, MEM_W), lambda i: (0, 0)),
                  pl.BlockSpec((n_mem, MEM_W), lambda i: (0, 1))],
        out_specs=pl.BlockSpec((tq, MEM_W), lambda i: (i, 0)),
        out_shape=jax.ShapeDtypeStruct((n_tok, MEM_W), bf16),
        compiler_params=_params("parallel"),
        name="mem_attn_prompt",
    )(proj_b, mem_kv, mem_kv)


def _mem_attn_sample_kernel(q_ref, k_ref, v_ref, o_ref):
    s = _dot_nt(q_ref[...].astype(bf16), k_ref[...].astype(bf16)) * (HEAD_DIM ** -0.5)
    r = lax.broadcasted_iota(i32, s.shape, 0)
    c = lax.broadcasted_iota(i32, s.shape, 1)
    s = jnp.where(r % MEM_HEADS == c % MEM_HEADS, s, NEG_INF)
    p = jnp.exp(s - jnp.max(s, axis=1, keepdims=True))
    o_ref[...] = _dot(p.astype(bf16), v_ref[...].astype(bf16)) / jnp.sum(p, axis=1, keepdims=True)


def _mem_attn_sample(q, mem_k, mem_v):
    n_seq, rows, _ = q.shape
    mrows = mem_k.shape[1]
    return pl.pallas_call(
        _mem_attn_sample_kernel,
        grid=(n_seq,),
        in_specs=[pl.BlockSpec((None, rows, HEAD_DIM), lambda b: (b, 0, 0)),
                  pl.BlockSpec((None, mrows, HEAD_DIM), lambda b: (b, 0, 0)),
                  pl.BlockSpec((None, mrows, HEAD_DIM), lambda b: (b, 0, 0))],
        out_specs=pl.BlockSpec((None, rows, HEAD_DIM), lambda b: (b, 0, 0)),
        out_shape=jax.ShapeDtypeStruct((n_seq, rows, HEAD_DIM), f32),
        compiler_params=_params("parallel"),
        name="mem_attn_sample",
    )(q, mem_k, mem_v)


def _outproj_router_kernel(*refs, starts):
    n_g = len(starts)
    i = pl.program_id(0)
    x, og, om, oc = [_stacked_tile(refs[k * n_g:(k + 1) * n_g], starts, i) for k in range(4)]
    w_ref, g_ref, rw_ref, rb_ref, y_ref, h_ref, ti_ref, tw_ref = refs[4 * n_g:]
    mix = (_dot(og, w_ref[0:GDN_V, :])
           + _dot(om, w_ref[GDN_V:GDN_V + MOBA_W, :])
           + _dot(oc, w_ref[GDN_V + MOBA_W:, :]))
    y = x + mix
    y_ref[...] = y
    h = _rms(y, g_ref[...])
    _store_slabs(h_ref, _pack_bf16_pairs(h))
    h_hi, h_lo = _split_bf16(h)
    w_hi, w_lo = _split_bf16(rw_ref[...])
    logits = _dot(h_hi, w_hi) + (_dot(h_hi, w_lo) + _dot(h_lo, w_hi)) + rb_ref[...]
    lane = lax.broadcasted_iota(i32, logits.shape, 1)
    s = jnp.where(lane < N_EXPERTS, logits, -jnp.inf)
    vals, idxs = [], []
    for _ in range(TOP_K):
        m = jnp.max(s, axis=1, keepdims=True)
        idx = jnp.min(jnp.where(s == m, lane, LANES), axis=1, keepdims=True)
        vals.append(m)
        idxs.append(idx)
        s = jnp.where(lane == idx, -jnp.inf, s)
    exps = [jnp.exp(v - vals[0]) for v in vals]
    den = exps[0]
    for e in exps[1:]:
        den = den + e
    ti = jnp.zeros(logits.shape, i32)
    tw = jnp.zeros(logits.shape, f32)
    for r in range(TOP_K):
        ti = jnp.where(lane == r, idxs[r], ti)
        tw = jnp.where(lane == r, exps[r] / den, tw)
    ti_ref[...] = ti
    tw_ref[...] = tw


def _outproj_router(xs, ogs, oms, ocs, w_out, g, rw, rb, *, tm):
    d = xs[0].shape[1]
    row = lambda w: pl.BlockSpec((tm, w), lambda i: (i, 0))
    full = lambda a: pl.BlockSpec(a.shape, lambda i: (0,) * a.ndim)
    specs, starts, n_tiles = [], None, None
    for group in (xs, ogs, oms, ocs):
        group_specs, starts, n_tiles = _stacked_specs(group, tm, 1)
        specs += group_specs
    n = n_tiles * tm
    return pl.pallas_call(
        functools.partial(_outproj_router_kernel, starts=starts),
        grid=(n_tiles,),
        in_specs=specs + [full(w_out), full(g), full(rw), full(rb)],
        out_specs=[row(d), pl.BlockSpec((tm * PACKED_G, LANES), lambda i: (i, 0)), row(LANES), row(LANES)],
        out_shape=[jax.ShapeDtypeStruct((n, d), f32), jax.ShapeDtypeStruct((n * PACKED_G, LANES), jnp.uint32),
                   jax.ShapeDtypeStruct((n, LANES), i32), jax.ShapeDtypeStruct((n, LANES), f32)],
        compiler_params=_params("parallel"),
        name="outproj_router",
    )(*xs, *ogs, *oms, *ocs, w_out, g, rw, rb)


GATHER_UNROLL = 8
SLAB_G = D_MODEL // LANES
PACKED_G = SLAB_G // 2


def _store_slabs(slab_ref, x, mask=None):
    rows = x.shape[0]
    g = x.shape[1] // LANES
    for c in range(g):
        idx = (pl.ds(c, rows, stride=g), slice(None))
        blk = x[:, c * LANES:(c + 1) * LANES]
        slab_ref[idx] = blk if mask is None else jnp.where(mask, blk, slab_ref[idx])


def _pack_bf16_pairs(x):
    half = x.shape[1] // 2
    bits = lambda v: lax.bitcast_convert_type(v.astype(bf16).astype(f32), jnp.uint32)
    return (bits(x[:, :half]) >> 16) | (bits(x[:, half:]) & jnp.uint32(0xFFFF0000))


def _unpack_bf16_pair(word):
    as_bf16 = lambda w: lax.bitcast_convert_type(w, f32).astype(bf16)
    return as_bf16(word << 16), as_bf16(word & jnp.uint32(0xFFFF0000))


def _load_slab_group(slab_ref, lead, c, rows, g):
    return slab_ref[lead + (pl.ds(c, rows, stride=g), slice(None))]


def _issue_slab_gather(idx_ref, idx_base, src_ref, dst_ref, sem, count, g):
    assert count % GATHER_UNROLL == 0

    def body(jj, carry):
        for u in range(GATHER_UNROLL):
            j = jj * GATHER_UNROLL + u
            tok = pl.multiple_of(idx_ref[idx_base + j] * g, g)
            dst = dst_ref.at[pl.ds(pl.multiple_of(j * g, g), g)]
            pltpu.make_async_copy(src_ref.at[pl.ds(tok, g)], dst, sem).start(priority=u % 2)
        return carry

    lax.fori_loop(0, count // GATHER_UNROLL, body, 0)


def _wait_slab_gather(src_ref, dst_ref, sem):
    pltpu.make_async_copy(src_ref.at[pl.ds(0, dst_ref.shape[0])], dst_ref, sem).wait()


def _gather_x_kernel(idx_ref, src_ref, o_ref, buf_ref, sem, *, rows, g):
    i = pl.program_id(0)
    slot = i % 2

    @pl.when(i == 0)
    def _():
        _issue_slab_gather(idx_ref, 0, src_ref, buf_ref.at[0], sem.at[0], rows, g)

    @pl.when(i + 1 < pl.num_programs(0))
    def _():
        _issue_slab_gather(idx_ref, (i + 1) * rows, src_ref, buf_ref.at[1 - slot], sem.at[1 - slot], rows, g)

    _wait_slab_gather(src_ref, buf_ref.at[slot], sem.at[slot])
    for c in range(g):
        lo, hi = _unpack_bf16_pair(_load_slab_group(buf_ref, (slot,), c, rows, g))
        o_ref[:, c * LANES:(c + 1) * LANES] = lo
        o_ref[:, (g + c) * LANES:(g + c + 1) * LANES] = hi


def _gather_x(src, idx, *, rows, g):
    n_out = idx.shape[0]
    grid_spec = pltpu.PrefetchScalarGridSpec(
        num_scalar_prefetch=1,
        grid=(n_out // rows,),
        in_specs=[pl.BlockSpec(memory_space=pl.ANY)],
        out_specs=pl.BlockSpec((rows, 2 * g * LANES), lambda i, idx: (i, 0)),
        scratch_shapes=[pltpu.VMEM((2, rows * g, LANES), src.dtype), pltpu.SemaphoreType.DMA((2,))],
    )
    return pl.pallas_call(
        functools.partial(_gather_x_kernel, rows=rows, g=g),
        grid_spec=grid_spec,
        out_shape=jax.ShapeDtypeStruct((n_out, 2 * g * LANES), bf16),
        compiler_params=_params("arbitrary"),
        name="moe_gather_x",
    )(idx, src)


def _new_expert(ev_ref, v):
    return (v == 0) | (ev_ref[v] != ev_ref[jnp.maximum(v - 1, 0)])


def _expert_weights_step(copies, is_new, run, n_runs, e_here, e_next, e_first, sweep, n_sweeps, on_arrival):
    slot = (run + sweep * n_runs) % 2

    @pl.when(is_new)
    def _():
        @pl.when((sweep == 0) & (run == 0))
        def _():
            for c in copies(e_here, sweep, slot):
                c.start()

        for c in copies(e_here, sweep, slot):
            c.wait()
        on_arrival(slot)
        last = run == n_runs - 1

        @pl.when(jnp.logical_not(last))
        def _():
            for c in copies(e_next, sweep, 1 - slot):
                c.start()

        @pl.when(last & (sweep + 1 < n_sweeps))
        def _():
            for c in copies(e_first, sweep + 1, 1 - slot):
                c.start()


def _moe_gate_up_kernel(tv_ref, ev_ref, lo_ref, hi_ref, run_ref, nxt_ref, nrun_ref, x_ref, w_hbm, b_ref,
                        o_ref, wf_ref, wb_ref, sem, *, tf):
    v = pl.program_id(0)
    lo = lo_ref[v]
    hi = hi_ref[v]

    def copies(e, sweep, slot):
        return [pltpu.make_async_copy(w_hbm.at[e], wf_ref, sem.at[0])]

    def on_arrival(slot):
        wb_ref[...] = wf_ref[...].astype(bf16)

    _expert_weights_step(copies, _new_expert(ev_ref, v), run_ref[v], nrun_ref[0], ev_ref[v], nxt_ref[v], ev_ref[0],
                         0, 1, on_arrival)

    def activations(c0):
        x = x_ref[...]
        gate = _dot(x, wb_ref[:, c0:c0 + tf]) + b_ref[:, c0:c0 + tf]
        up = _dot(x, wb_ref[:, D_FF + c0:D_FF + c0 + tf]) + b_ref[:, D_FF + c0:D_FF + c0 + tf]
        gate = jnp.minimum(gate, SWIGLU_LIMIT)
        up = jnp.clip(up, -SWIGLU_LIMIT, SWIGLU_LIMIT)
        return (gate * jax.nn.sigmoid(SWIGLU_ALPHA * gate) * (up + 1.0)).astype(o_ref.dtype)

    @pl.when(hi > lo)
    def _():
        @pl.when(lo == 0)
        def _():
            for c0 in range(0, D_FF, tf):
                o_ref[:, c0:c0 + tf] = activations(c0)

        @pl.when(lo > 0)
        def _():
            r = lax.broadcasted_iota(i32, (o_ref.shape[0], tf), 0)
            mine = (r >= lo) & (r < hi)
            for c0 in range(0, D_FF, tf):
                o_ref[:, c0:c0 + tf] = jnp.where(mine, activations(c0), o_ref[:, c0:c0 + tf])


def _moe_gate_up(xs, w_gu, b_gu3, visits, *, tm, tf):
    rows, d = xs.shape
    grid_spec = pltpu.PrefetchScalarGridSpec(
        num_scalar_prefetch=len(visits),
        grid=(visits[0].shape[0],),
        in_specs=[
            pl.BlockSpec((tm, d), lambda v, tv, *_: (tv[v], 0)),
            pl.BlockSpec(memory_space=pl.ANY),
            pl.BlockSpec((None, 1, 2 * D_FF), lambda v, tv, ev, *_: (ev[v], 0, 0)),
        ],
        out_specs=pl.BlockSpec((tm, D_FF), lambda v, tv, *_: (tv[v], 0)),
        scratch_shapes=[pltpu.VMEM((d, 2 * D_FF), f32), pltpu.VMEM((d, 2 * D_FF), bf16),
                        pltpu.SemaphoreType.DMA((1,))],
    )
    return pl.pallas_call(
        functools.partial(_moe_gate_up_kernel, tf=tf),
        grid_spec=grid_spec,
        out_shape=jax.ShapeDtypeStruct((rows, D_FF), bf16),
        compiler_params=_params("arbitrary"),
        name="moe_gate_up",
    )(*visits, xs, w_gu, b_gu3)


def _moe_down_kernel(tv_ref, ev_ref, lo_ref, hi_ref, run_ref, nxt_ref, nrun_ref, a_ref, w_hbm, b_ref,
                     o_ref, wf_ref, wb_ref, sem):
    v = pl.program_id(0)
    lo = lo_ref[v]
    hi = hi_ref[v]

    def copies(e, sweep, slot):
        return [pltpu.make_async_copy(w_hbm.at[e], wf_ref.at[slot], sem.at[slot])]

    def on_arrival(slot):
        wb_ref[...] = wf_ref[slot].astype(bf16)

    _expert_weights_step(copies, _new_expert(ev_ref, v), run_ref[v], nrun_ref[0], ev_ref[v], nxt_ref[v], ev_ref[0],
                         0, 1, on_arrival)

    @pl.when(hi > lo)
    def _():
        @pl.when(lo == 0)
        def _():
            _store_slabs(o_ref, _dot(a_ref[...], wb_ref[...]) + b_ref[...])

        @pl.when(lo > 0)
        def _():
            r = lax.broadcasted_iota(i32, (a_ref.shape[0], LANES), 0)
            _store_slabs(o_ref, _dot(a_ref[...], wb_ref[...]) + b_ref[...], mask=(r >= lo) & (r < hi))


def _moe_down(act, w_dn, b_dn3, visits, *, tm):
    rows, dff = act.shape
    d = w_dn.shape[2]
    g = d // LANES
    grid_spec = pltpu.PrefetchScalarGridSpec(
        num_scalar_prefetch=len(visits),
        grid=(visits[0].shape[0],),
        in_specs=[
            pl.BlockSpec((tm, dff), lambda v, tv, *_: (tv[v], 0)),
            pl.BlockSpec(memory_space=pl.ANY),
            pl.BlockSpec((None, 1, d), lambda v, tv, ev, *_: (ev[v], 0, 0)),
        ],
        out_specs=pl.BlockSpec((tm * g, LANES), lambda v, tv, *_: (tv[v], 0)),
        scratch_shapes=[pltpu.VMEM((2, dff, d), f32), pltpu.VMEM((dff, d), bf16), pltpu.SemaphoreType.DMA((2,))],
    )
    return pl.pallas_call(
        _moe_down_kernel,
        grid_spec=grid_spec,
        out_shape=jax.ShapeDtypeStruct((rows * g, LANES), f32),
        compiler_params=_params("arbitrary"),
        name="moe_down",
    )(*visits, act, w_dn, b_dn3)


def _moe_combine_kernel(pos_ref, y_ref, tw_ref, ys_ref, oa_ref, ob_ref, buf_ref, sem, *, rows, g, tiles_a):
    i = pl.program_id(0)
    slot = i % 2
    n_tok = pl.num_programs(0) * rows

    def issue(tile, s):
        for k in range(TOP_K):
            _issue_slab_gather(pos_ref, k * n_tok + tile * rows, ys_ref, buf_ref.at[s, k], sem.at[s], rows, g)

    @pl.when(i == 0)
    def _():
        issue(0, 0)

    @pl.when(i + 1 < pl.num_programs(0))
    def _():
        issue(i + 1, 1 - slot)

    for k in range(TOP_K):
        _wait_slab_gather(ys_ref, buf_ref.at[slot, k], sem.at[slot])
    tw = tw_ref[...]

    def combine(o_ref):
        for c in range(g):
            cs = slice(c * LANES, (c + 1) * LANES)
            acc = y_ref[:, cs]
            for k in range(TOP_K):
                acc = acc + _load_slab_group(buf_ref, (slot, k), c, rows, g) * tw[:, k:k + 1]
            o_ref[:, cs] = acc

    pl.when(i < tiles_a)(lambda: combine(oa_ref))
    pl.when(i >= tiles_a)(lambda: combine(ob_ref))


def _moe_combine(y1, ys, pos_kmajor, tw, *, rows, n_a):
    n, d = y1.shape
    g = d // LANES
    tiles_a = n_a // rows
    tiles = n // rows
    grid_spec = pltpu.PrefetchScalarGridSpec(
        num_scalar_prefetch=1,
        grid=(tiles,),
        in_specs=[pl.BlockSpec((rows, d), lambda i, pos: (i, 0)),
                  pl.BlockSpec((rows, LANES), lambda i, pos: (i, 0)),
                  pl.BlockSpec(memory_space=pl.ANY)],
        out_specs=[pl.BlockSpec((rows, d), lambda i, pos: (jnp.minimum(i, tiles_a - 1), 0)),
                   pl.BlockSpec((rows, d), lambda i, pos: (jnp.maximum(i - tiles_a, 0), 0))],
        scratch_shapes=[pltpu.VMEM((2, TOP_K, rows * g, LANES), f32), pltpu.SemaphoreType.DMA((2,))],
    )
    return pl.pallas_call(
        functools.partial(_moe_combine_kernel, rows=rows, g=g, tiles_a=tiles_a),
        grid_spec=grid_spec,
        out_shape=[jax.ShapeDtypeStruct((n_a, d), f32), jax.ShapeDtypeStruct((n - n_a, d), f32)],
        compiler_params=_params("arbitrary"),
        name="moe_combine",
    )(pos_kmajor, y1, tw, ys)


def _sort_rows(top_i):
    e = top_i.reshape(-1)
    a = e.shape[0]
    order = jnp.argsort(e, stable=True).astype(i32)
    tok_sorted = order // TOP_K
    blk = LANES
    onehot = (e[:, None] == jnp.arange(N_EXPERTS, dtype=i32)[None, :]).astype(f32).reshape(a // blk, blk, N_EXPERTS)
    earlier = (jnp.arange(blk)[:, None] > jnp.arange(blk)[None, :]).astype(f32)
    within = jnp.einsum("ij,bjk->bik", earlier, onehot)
    block_sums = jnp.sum(onehot, axis=1)
    block_off = jnp.cumsum(block_sums, axis=0) - block_sums
    counts = jnp.sum(block_sums, axis=0)
    starts = jnp.cumsum(counts) - counts
    pos = jnp.sum(onehot * (within + block_off[:, None, :] + starts[None, None, :]), axis=2).reshape(a)
    return tok_sorted, pos.astype(i32), counts.astype(i32)


def _visits(counts, n_rows, *, tm):
    ends = jnp.cumsum(counts)
    starts = ends - counts
    first_tile = starts // tm
    last_tile = jnp.maximum(ends - 1, 0) // tm
    n_vis = jnp.where(counts > 0, last_tile - first_tile + 1, 0)
    vis_end = jnp.cumsum(n_vis)
    vis_start = vis_end - n_vis
    total = vis_end[-1]
    v = jnp.arange(n_rows // tm + N_EXPERTS, dtype=i32)
    vc = jnp.minimum(v, total - 1)
    ev = jnp.sum((vis_end[None, :] <= vc[:, None]).astype(i32), axis=1)
    tv = first_tile[ev] + (vc - vis_start[ev])
    lo = jnp.maximum(starts[ev], tv * tm) - tv * tm
    hi = jnp.minimum(ends[ev], (tv + 1) * tm) - tv * tm
    valid = v < total
    experts = jnp.arange(N_EXPERTS, dtype=i32)
    present = n_vis > 0
    n_runs = jnp.sum(present.astype(i32))
    run = jnp.sum((present[None, :] & (experts[None, :] < ev[:, None])).astype(i32), axis=1)
    rank = jnp.cumsum(present.astype(i32))
    run_expert = jnp.sum((rank[None, :] <= experts[:, None]).astype(i32), axis=1)
    nxt = run_expert[jnp.minimum(run + 1, n_runs - 1)]
    return tv, ev, jnp.where(valid, lo, 0), jnp.where(valid, hi, 0), run, nxt, n_runs[None]


GATE_UP_TM = 256
GATE_UP_TF = 1024
DOWN_TM = 256
GATHER_ROWS = 256
COMBINE_ROWS = 128


def _moe(y1, h_slabs, top_i, top_w, w_gu, b_gu, w_dn, b_dn, *, n_a):
    n_tok = y1.shape[0]
    tok_sorted, pos, counts = _sort_rows(top_i[:, :TOP_K])
    n_rows = tok_sorted.shape[0]
    xs = _gather_x(h_slabs, tok_sorted, rows=GATHER_ROWS, g=PACKED_G)
    act = _moe_gate_up(xs, w_gu, b_gu[:, None, :], _visits(counts, n_rows, tm=GATE_UP_TM),
                       tm=GATE_UP_TM, tf=GATE_UP_TF)
    ys = _moe_down(act, w_dn, b_dn[:, None, :], _visits(counts, n_rows, tm=DOWN_TM), tm=DOWN_TM)
    pos_kmajor = pos.reshape(n_tok, TOP_K).T.reshape(-1)
    return _moe_combine(y1, ys, pos_kmajor, top_w, rows=COMBINE_ROWS, n_a=n_a)


def kernel(x_prompt, x_sample, cache_moba_k, cache_moba_v, state_gdn, state_gdn_conv, cache_mem_k, cache_mem_v, page_table, mem_prompt, rel_bias, norm_mix, w_in, conv_w, a_log, dt_bias, gdn_o_norm, moba_q_norm, moba_k_norm, mem_q_norm, mem_norm, w_mem_kv, mem_k_norm, w_out, norm_ffn, router_w, router_b, w_gu, b_gu, w_dn, b_dn):
    assert x_prompt.shape[0] == 1 and all(a.shape[0] == 1 for a in (w_in, w_out, w_gu, w_dn, state_gdn))
    n_p = x_prompt.shape[1]
    n_seq, n_st = x_sample.shape[:2]
    n_s = n_seq * n_st
    d = x_prompt.shape[2]
    n_blk = n_p // MOBA_BLOCK
    xs = (x_prompt.reshape(n_p, d), x_sample.reshape(n_s, d))

    w = w_in[0]
    off_beta = A_W
    off_moba = off_beta + 2 * GDN_HEADS
    off_mem = off_moba + 3 * MOBA_W
    w_a = w[:, :A_W].astype(bf16)
    w_b = jnp.concatenate(
        [w[:, off_moba:off_moba + MOBA_W], w[:, off_moba + MOBA_W:off_moba + 2 * MOBA_W], w[:, off_mem:],
         w[:, off_moba + 2 * MOBA_W:off_mem], w[:, off_beta:off_moba],
         jnp.zeros((d, LANES - 2 * GDN_HEADS), w.dtype)], axis=1).astype(bf16)
    gain_b = jnp.concatenate([jnp.tile(moba_q_norm[0], MOBA_HEADS), jnp.tile(moba_k_norm[0], MOBA_HEADS),
                              jnp.tile(mem_q_norm[0], MEM_HEADS)])[None]
    proj_a = _norm_matmul(xs, norm_mix, w_a, gain_b, tm=512, tn=1024, n_norm=0)
    proj_b, mk_p, mk_s, mv_p, mv_s = _norm_matmul(
        xs, norm_mix, w_b, gain_b, tm=256, tn=B_W, n_norm=B_NORM_GROUPS, head_outs=(B_MK // LANES, B_MV // LANES))
    proj_b3 = proj_b[n_p:].reshape(n_seq, n_st, B_W)

    pad_row = lambda a: jnp.zeros((1, LANES), f32).at[0, GDN_HEADS:2 * GDN_HEADS].set(a[0])
    alog_row, dtb_row = pad_row(a_log), pad_row(dt_bias)
    o_gdn_p, p_gdn = _gdn_prompt(proj_a, proj_b, conv_w[0], alog_row, dtb_row, gdn_o_norm, n_tok=n_p)
    p_conv = proj_a[n_p - (GDN_CONV - 1):n_p, :GDN_CONV_DIM]
    xa_s = proj_a[n_p:].reshape(n_seq, n_st, A_W)
    o_gdn_s, s_gdn = _gdn_sample(jnp.swapaxes(xa_s, 0, 1), jnp.swapaxes(proj_b3[..., B_BA:], 0, 1),
                                 jnp.swapaxes(state_gdn_conv[0], 0, 1), conv_w[0], alog_row, dtb_row,
                                 gdn_o_norm, state_gdn[0])
    o_gdn_s = jnp.swapaxes(o_gdn_s, 0, 1).reshape(n_s, GDN_V).astype(bf16)
    s_conv = jnp.concatenate([state_gdn_conv[0], xa_s[..., :GDN_CONV_DIM]], axis=1)[:, n_st:]

    kmean = _block_mean(proj_b, n_blk).reshape(n_blk, MOBA_W)
    sel = _moba_select(proj_b, kmean, n_blk)
    o_moba_p = _moba_prompt(proj_b, sel, rel_bias, n_blk)
    pairs = lambda col: proj_b3[..., col:col + MOBA_W].reshape(n_seq, n_st * MOBA_HEADS, HEAD_DIM)
    pair_rows = n_st * MOBA_HEADS
    o_moba_s = _moba_sample(pairs(B_MQ), mk_s.reshape(n_seq, pair_rows, HEAD_DIM),
                            mv_s.reshape(n_seq, pair_rows, HEAD_DIM), cache_moba_k.reshape(-1, HEAD_DIM),
                            cache_moba_v.reshape(-1, HEAD_DIM), page_table, rel_bias, page=cache_moba_k.shape[2])
    o_moba_s = o_moba_s.reshape(n_s, MOBA_W).astype(bf16)

    n_mem = mem_prompt.shape[1]
    mem_kv = _norm_matmul((mem_prompt[0],), mem_norm, w_mem_kv[0].astype(bf16),
                          jnp.tile(mem_k_norm[0], MEM_HEADS)[None], tm=n_mem, tn=2 * MEM_W, n_norm=MEM_HEADS)
    o_mem_p = _mem_attn_prompt(proj_b, mem_kv, n_tok=n_p, tq=512)
    o_mem_s = _mem_attn_sample(pairs(B_CQ), cache_mem_k.reshape(n_seq, n_mem * MEM_HEADS, HEAD_DIM),
                               cache_mem_v.reshape(n_seq, n_mem * MEM_HEADS, HEAD_DIM))
    o_mem_s = o_mem_s.reshape(n_s, MEM_W).astype(bf16)

    rw = jnp.pad(router_w[0], ((0, 0), (0, LANES - N_EXPERTS)))
    rb = jnp.pad(router_b, ((0, 0), (0, LANES - N_EXPERTS)))
    y1, h2, top_i, top_w = _outproj_router(
        xs, (o_gdn_p, o_gdn_s), (o_moba_p, o_moba_s), (o_mem_p, o_mem_s),
        w_out[0].astype(bf16), norm_ffn, rw, rb, tm=256)
    y_p, y_s = _moe(y1, h2, top_i, top_w, w_gu[0], b_gu[0], w_dn[0], b_dn[0], n_a=n_p)

    heads = lambda a, lead: a.reshape(lead + (MOBA_HEADS, HEAD_DIM))
    return (y_p.reshape(1, n_p, d), y_s.reshape(n_seq, n_st, d),
            heads(mk_p, (1, 1, n_p)), heads(mv_p, (1, 1, n_p)),
            p_gdn[None, None], p_conv[None, None],
            heads(mem_kv[:, :MEM_W], (1, 1, n_mem)), heads(mem_kv[:, MEM_W:], (1, 1, n_mem)),
            heads(mk_s, (1, n_seq, n_st)), heads(mv_s, (1, n_seq, n_st)),
            s_gdn[None], s_conv[None])
```

```python
import functools
import math

import numpy as np
import jax
import jax.numpy as jnp
from jax import lax
from jax.experimental import pallas as pl
from jax.experimental.pallas import tpu as pltpu

f32 = jnp.float32
bf16 = jnp.bfloat16
i32 = jnp.int32
HIGHEST = lax.Precision.HIGHEST

LANES = 128
SUBLANES = 8
VMEM_LIMIT = 60 * 1024 * 1024

D_MODEL = 2048
HEAD_DIM = 128
GDN_HEADS = 8
GDN_CONV = 4
GDN_CHUNK = 64
GDN_QK = GDN_HEADS * HEAD_DIM
GDN_V = GDN_HEADS * HEAD_DIM
GDN_CONV_DIM = 2 * GDN_QK + GDN_V
MOBA_HEADS = 4
MOBA_BLOCK = 256
MOBA_TOPK = 3
MOBA_W = MOBA_HEADS * HEAD_DIM
MEM_HEADS = 4
MEM_W = MEM_HEADS * HEAD_DIM
N_BUCKETS = 32
MAX_DISTANCE = 128
N_EXPERTS = 32
TOP_K = 4
D_FF = D_MODEL
SWIGLU_LIMIT = 7.0
SWIGLU_ALPHA = 1.702
EPS = 1e-6
NEG_INF = -1e30

A_W = GDN_CONV_DIM + GDN_V
B_MQ, B_MK, B_CQ, B_MV, B_BA = 0, MOBA_W, 2 * MOBA_W, 3 * MOBA_W, 4 * MOBA_W
B_W = B_BA + LANES
B_NORM_GROUPS = 3 * MOBA_HEADS


def _params(*sem):
    return pltpu.CompilerParams(dimension_semantics=sem, vmem_limit_bytes=VMEM_LIMIT)


def _dot(a, b, **kw):
    return jnp.dot(a, b, preferred_element_type=f32, **kw)


def _dot_nt(a, b, **kw):
    return lax.dot_general(a, b, (((1,), (1,)), ((), ())), preferred_element_type=f32, **kw)


def _dot_tn(a, b, **kw):
    return lax.dot_general(a, b, (((0,), (0,)), ((), ())), preferred_element_type=f32, **kw)


def _rms(x, gain):
    return x * lax.rsqrt(jnp.mean(x * x, axis=-1, keepdims=True) + EPS) * gain


def _silu(x):
    return x * jax.nn.sigmoid(x)


def _stacked_specs(arrays, tm, grid_rank):
    counts = [a.shape[0] // tm for a in arrays]
    starts = [sum(counts[:k]) for k in range(len(arrays))]

    def spec(k):
        def index_map(i, *_):
            return (jnp.clip(i - starts[k], 0, counts[k] - 1), 0)
        return pl.BlockSpec((tm, arrays[k].shape[1]), index_map)

    return [spec(k) for k in range(len(arrays))], starts, sum(counts)


def _stacked_tile(refs, starts, i):
    x = refs[0][...]
    for ref, start in zip(refs[1:], starts[1:]):
        x = jnp.where(i >= start, ref[...], x)
    return x


def _norm_matmul_kernel(*refs, n_norm, starts, head_outs, n_heads, bf16_groups):
    n_x = len(starts)
    x_refs = refs[:n_x]
    g_ref, w_ref, gain_ref, o_ref = refs[n_x:n_x + 4]
    n_pair = len(head_outs) * n_x
    pair_refs = refs[n_x + 4:n_x + 4 + n_pair]
    o16_ref = refs[n_x + 4 + n_pair] if bf16_groups else None
    h_ref = refs[-1]
    i = pl.program_id(0)

    @pl.when(pl.program_id(1) == 0)
    def _():
        h_ref[...] = _rms(_stacked_tile(x_refs, starts, i), g_ref[...]).astype(bf16)

    acc = _dot(h_ref[...], w_ref[...])
    if n_norm == 0 and not head_outs and not bf16_groups:
        o_ref[...] = acc
        return
    tm = acc.shape[0]
    blks = []
    for gi in range(acc.shape[1] // LANES):
        sl = slice(gi * LANES, (gi + 1) * LANES)
        blk = acc[:, sl]
        if gi < n_norm:
            blk = _rms(blk, gain_ref[:, sl])
        o_ref[:, sl] = blk
        blks.append(blk)
    for idx, gi in enumerate(bf16_groups):
        o16_ref[:, idx * LANES:(idx + 1) * LANES] = blks[gi].astype(bf16)
    bounds = list(starts[1:]) + [None]
    for e, first in enumerate(head_outs):
        for k in range(n_x):
            ref = pair_refs[e * n_x + k]
            mine = (i >= starts[k]) if bounds[k] is None else ((i >= starts[k]) & (i < bounds[k]))

            @pl.when(mine)
            def _(ref=ref, first=first):
                for h in range(n_heads):
                    ref[pl.ds(h, tm, stride=n_heads), :] = blks[first + h]


def _norm_matmul(xs, g, w, gain, *, tm, tn, n_norm, head_outs=(), n_heads=MOBA_HEADS, bf16_groups=()):
    d = xs[0].shape[1]
    wn = w.shape[1]
    assert not (head_outs or bf16_groups) or tn == wn
    x_specs, starts, n_tiles = _stacked_specs(xs, tm, 2)
    counts = [x.shape[0] // tm for x in xs]
    out_specs = [pl.BlockSpec((tm, tn), lambda i, j: (i, j))]
    out_shape = [jax.ShapeDtypeStruct((n_tiles * tm, wn), f32)]
    for _ in head_outs:
        for k, x in enumerate(xs):
            out_specs.append(pl.BlockSpec(
                (tm * n_heads, LANES), lambda i, j, k=k: (jnp.clip(i - starts[k], 0, counts[k] - 1), 0)))
            out_shape.append(jax.ShapeDtypeStruct((x.shape[0] * n_heads, LANES), f32))
    if bf16_groups:
        out_specs.append(pl.BlockSpec((tm, len(bf16_groups) * LANES), lambda i, j: (i, 0)))
        out_shape.append(jax.ShapeDtypeStruct((n_tiles * tm, len(bf16_groups) * LANES), bf16))
    outs = pl.pallas_call(
        functools.partial(_norm_matmul_kernel, n_norm=n_norm, starts=starts, head_outs=tuple(head_outs),
                          n_heads=n_heads, bf16_groups=tuple(bf16_groups)),
        grid=(n_tiles, wn // tn),
        in_specs=x_specs + [
            pl.BlockSpec((1, d), lambda i, j: (0, 0)),
            pl.BlockSpec((d, tn), lambda i, j: (0, j)),
            pl.BlockSpec((1, gain.shape[1]), lambda i, j: (0, 0)),
        ],
        out_specs=out_specs,
        out_shape=out_shape,
        scratch_shapes=[pltpu.VMEM((tm, d), bf16)],
        compiler_params=_params("arbitrary", "arbitrary"),
        name="norm_matmul",
    )(*xs, g, w, gain)
    return outs if (head_outs or bf16_groups) else outs[0]


INV_BLOCK = 16


def _split_bf16(x):
    hi = x.astype(bf16)
    return hi, (x - hi.astype(f32)).astype(bf16)


def _dot3(a, b, dot=_dot):
    a_hi, a_lo = a
    b_hi, b_lo = b
    return dot(a_hi, b_hi) + (dot(a_hi, b_lo) + dot(a_lo, b_hi))


def _unit_lower_inverses(lmats):
    c = lmats[0].shape[0]
    row = lax.broadcasted_iota(i32, (c, c), 0)
    col = lax.broadcasted_iota(i32, (c, c), 1)
    same = lambda n: (row // n) == (col // n)
    eye = jnp.where(row == col, 1.0, 0.0).astype(f32)
    ms = [jnp.where(same(INV_BLOCK), l, 0.0) for l in lmats]
    ps = [eye - m for m in ms]
    power = 1
    while 2 * power < INV_BLOCK:
        sm = [_split_bf16(m) for m in ms]
        ms = [_dot3(s, s) for s in sm]
        sm = [_split_bf16(m) for m in ms]
        ps = [p + _dot3(_split_bf16(p), s) for p, s in zip(ps, sm)]
        power *= 2
    n = 2 * INV_BLOCK
    while n <= c:
        offs = [_split_bf16(jnp.where(same(n) & ~same(n // 2), l, 0.0)) for l in lmats]
        sp = [_split_bf16(p) for p in ps]
        mids = [_dot3(o, s) for o, s in zip(offs, sp)]
        ps = [p - _dot3(s, _split_bf16(m)) for p, s, m in zip(ps, sp, mids)]
        n *= 2
    return ps


def _gdn_prompt_kernel(qkv_ref, z_ref, ba_ref, convw_ref, alog_ref, dtb_ref, onorm_ref,
                       o_ref, s_out_ref, xp_ref, s_ref):
    c = GDN_CHUNK
    step = pl.program_id(0)

    @pl.when(step == 0)
    def _():
        xp_ref[0:SUBLANES, :] = jnp.zeros((SUBLANES, GDN_CONV_DIM), f32)
        s_ref[...] = jnp.zeros_like(s_ref)

    xp_ref[SUBLANES:SUBLANES + c, :] = qkv_ref[...]
    w = convw_ref[...]
    y = None
    for j in range(GDN_CONV):
        lo = SUBLANES - (GDN_CONV - 1) + j
        term = xp_ref[lo:lo + c, :] * w[j:j + 1, :]
        y = term if y is None else y + term
    y = _silu(y)
    xp_ref[0:SUBLANES, :] = xp_ref[c:c + SUBLANES, :]

    ba = ba_ref[...]
    beta_all = jax.nn.sigmoid(ba)
    g_all = -jnp.exp(alog_ref[...]) * jax.nn.softplus(ba + dtb_ref[...])
    row = lax.broadcasted_iota(i32, (c, c), 0)
    col = lax.broadcasted_iota(i32, (c, c), 1)
    incl = row >= col
    strict = row > col
    gcum_all = _dot(jnp.where(incl, 1.0, 0.0).astype(f32), g_all, precision=HIGHEST)
    gcum_t = gcum_all.T
    onorm = onorm_ref[...]

    heads = range(GDN_HEADS)
    head_cols = lambda off, h: slice(off + h * HEAD_DIM, off + (h + 1) * HEAD_DIM)
    qs = [y[:, head_cols(0, h)] for h in heads]
    ks = [y[:, head_cols(GDN_QK, h)] for h in heads]
    vs = [y[:, head_cols(2 * GDN_QK, h)] for h in heads]
    qs = [q * lax.rsqrt(jnp.sum(q * q, axis=-1, keepdims=True) + EPS) * (HEAD_DIM ** -0.5) for q in qs]
    ks = [k * lax.rsqrt(jnp.sum(k * k, axis=-1, keepdims=True) + EPS) for k in ks]
    betas = [beta_all[:, h:h + 1] for h in heads]
    gcs = [gcum_all[:, GDN_HEADS + h:GDN_HEADS + h + 1] for h in heads]
    grs = [gcum_t[GDN_HEADS + h:GDN_HEADS + h + 1, :] for h in heads]
    g_lasts = [gcum_all[c - 1:c, GDN_HEADS + h:GDN_HEADS + h + 1] for h in heads]
    decays = [jnp.exp(jnp.where(incl, gc - gr, -jnp.inf)) for gc, gr in zip(gcs, grs)]
    kbetas = [k * b for k, b in zip(ks, betas)]
    lmats = [_dot_nt(kb, k) * jnp.where(strict, dc, 0.0) for kb, k, dc in zip(kbetas, ks, decays)]
    tinvs = _unit_lower_inverses(lmats)
    egcs = [jnp.exp(gc) for gc in gcs]
    us = [_dot(t, v * b) for t, v, b in zip(tinvs, vs, betas)]
    ws = [_dot(t, kb * e) for t, kb, e in zip(tinvs, kbetas, egcs)]
    intras = [_dot_nt(q, k) * dc for q, k, dc in zip(qs, ks, decays)]
    kdecs = [k * jnp.exp(gl - gc) for k, gl, gc in zip(ks, g_lasts, gcs)]
    ss = [s_ref[h] for h in heads]
    v_news = [u - _dot(w_, s) for u, w_, s in zip(us, ws, ss)]
    os_ = [_dot(q * e, s) + _dot(it, vn) for q, e, s, it, vn in zip(qs, egcs, ss, intras, v_news)]
    for h in heads:
        s_ref[h] = ss[h] * jnp.exp(g_lasts[h]) + _dot_tn(kdecs[h], v_news[h])
    for h in heads:
        hs = head_cols(0, h)
        o_ref[:, hs] = (_rms(os_[h], onorm) * _silu(z_ref[:, hs])).astype(o_ref.dtype)

    @pl.when(step == pl.num_programs(0) - 1)
    def _():
        s_out_ref[...] = s_ref[...]


def _gdn_prompt(proj_a, proj_b, conv_w, alog_row, dtb_row, onorm, *, n_tok):
    c = GDN_CHUNK
    zblk = GDN_CONV_DIM // GDN_V
    return pl.pallas_call(
        _gdn_prompt_kernel,
        grid=(n_tok // c,),
        in_specs=[
            pl.BlockSpec((c, GDN_CONV_DIM), lambda i: (i, 0)),
            pl.BlockSpec((c, GDN_V), lambda i: (i, zblk)),
            pl.BlockSpec((c, LANES), lambda i: (i, B_BA // LANES)),
            pl.BlockSpec((GDN_CONV, GDN_CONV_DIM), lambda i: (0, 0)),
            pl.BlockSpec((1, LANES), lambda i: (0, 0)),
            pl.BlockSpec((1, LANES), lambda i: (0, 0)),
            pl.BlockSpec((1, HEAD_DIM), lambda i: (0, 0)),
        ],
        out_specs=[
            pl.BlockSpec((c, GDN_V), lambda i: (i, 0)),
            pl.BlockSpec((GDN_HEADS, HEAD_DIM, HEAD_DIM), lambda i: (0, 0, 0)),
        ],
        out_shape=[
            jax.ShapeDtypeStruct((n_tok, GDN_V), bf16),
            jax.ShapeDtypeStruct((GDN_HEADS, HEAD_DIM, HEAD_DIM), f32),
        ],
        scratch_shapes=[
            pltpu.VMEM((c + SUBLANES, GDN_CONV_DIM), f32),
            pltpu.VMEM((GDN_HEADS, HEAD_DIM, HEAD_DIM), f32),
        ],
        compiler_params=_params("arbitrary"),
        name="gdn_prompt",
    )(proj_a, proj_a, proj_b, conv_w, alog_row, dtb_row, onorm)


GDN_S_SEQ = SUBLANES


def _gdn_sample_kernel(q_ref, k_ref, v_ref, z_ref, ba_ref, bq_ref, bk_ref, bv_ref, wq_ref, wk_ref, wv_ref,
                       alog_ref, dtb_ref, onorm_ref, s_ref, o_ref, so_ref, *, n_tok):
    h = pl.program_id(1)
    nb = GDN_S_SEQ
    lane = lax.broadcasted_iota(i32, (nb, LANES), 1)

    def conv(x_ref, buf_ref, w_ref, t):
        y = None
        for j in range(GDN_CONV):
            i = t + j
            row = buf_ref[i] if i < GDN_CONV - 1 else x_ref[i - (GDN_CONV - 1)]
            term = row * w_ref[j:j + 1, :]
            y = term if y is None else y + term
        return _silu(y)

    def lane_col(x, idx):
        return jnp.sum(jnp.where(lane == idx, x, 0.0), axis=1, keepdims=True)

    qs, ks, vs, betas, egs = [], [], [], [], []
    for t in range(n_tok):
        q = conv(q_ref, bq_ref, wq_ref, t)
        k = conv(k_ref, bk_ref, wk_ref, t)
        q = q * lax.rsqrt(jnp.sum(q * q, axis=-1, keepdims=True) + EPS) * (HEAD_DIM ** -0.5)
        k = k * lax.rsqrt(jnp.sum(k * k, axis=-1, keepdims=True) + EPS)
        qs.append(q.T)
        ks.append(k.T)
        vs.append(conv(v_ref, bv_ref, wv_ref, t))
        ba = ba_ref[t]
        betas.append(lane_col(jax.nn.sigmoid(ba), h))
        g = -jnp.exp(alog_ref[...]) * jax.nn.softplus(ba + dtb_ref[...])
        egs.append(jnp.exp(lane_col(g, GDN_HEADS + h)))

    o_rows = [[None] * nb for _ in range(n_tok)]
    for b in range(nb):
        s = s_ref[b]
        for t in range(n_tok):
            kc = ks[t][:, b:b + 1]
            qc = qs[t][:, b:b + 1]
            s = s * egs[t][b:b + 1, :]
            ks_row = jnp.sum(kc * s, axis=0, keepdims=True)
            delta = (vs[t][b:b + 1, :] - ks_row) * betas[t][b:b + 1, :]
            s = s + kc * delta
            o_rows[t][b] = jnp.sum(qc * s, axis=0, keepdims=True)
        so_ref[b] = s
    for t in range(n_tok):
        o = jnp.concatenate(o_rows[t], axis=0)
        o_ref[t] = _rms(o, onorm_ref[...]) * _silu(z_ref[t])


def _gdn_sample(xa, ba, conv_buf, conv_w, alog_row, dtb_row, onorm, state):
    n_tok, n_seq, _ = xa.shape
    nb = GDN_S_SEQ
    hq, hk, hv, hz = 0, GDN_HEADS, 2 * GDN_HEADS, 3 * GDN_HEADS

    def xspec(off, rows):
        return pl.BlockSpec((rows, nb, HEAD_DIM), lambda i, h: (0, i, off + h))

    def wspec(off):
        return pl.BlockSpec((GDN_CONV, HEAD_DIM), lambda i, h: (0, off + h))

    row = pl.BlockSpec((1, LANES), lambda i, h: (0, 0))
    sspec = pl.BlockSpec((nb, None, HEAD_DIM, HEAD_DIM), lambda i, h: (i, h, 0, 0))
    return pl.pallas_call(
        functools.partial(_gdn_sample_kernel, n_tok=n_tok),
        grid=(n_seq // nb, GDN_HEADS),
        in_specs=[xspec(hq, n_tok), xspec(hk, n_tok), xspec(hv, n_tok), xspec(hz, n_tok),
                  pl.BlockSpec((n_tok, nb, LANES), lambda i, h: (0, i, 0)),
                  xspec(hq, GDN_CONV - 1), xspec(hk, GDN_CONV - 1), xspec(hv, GDN_CONV - 1),
                  wspec(hq), wspec(hk), wspec(hv), row, row, row, sspec],
        out_specs=[pl.BlockSpec((n_tok, nb, HEAD_DIM), lambda i, h: (0, i, h)), sspec],
        out_shape=[jax.ShapeDtypeStruct((n_tok, n_seq, GDN_V), f32),
                   jax.ShapeDtypeStruct(state.shape, f32)],
        compiler_params=_params("parallel", "arbitrary"),
        name="gdn_sample",
    )(xa, xa, xa, xa, ba, conv_buf, conv_buf, conv_buf, conv_w, conv_w, conv_w, alog_row, dtb_row, onorm, state)


def _bucket_thresholds():
    exact = N_BUCKETS // 2
    d = np.arange(0, 2 * MAX_DISTANCE)
    val = np.log(np.maximum(d, 1).astype(np.float64) / exact) / math.log(MAX_DISTANCE / exact) * (N_BUCKETS - exact)
    frac = np.abs(val - np.round(val))[exact + 1:MAX_DISTANCE]
    assert frac.min() > 1e-3, "a bucket boundary sits on an integer distance"
    bucket = np.where(d < exact, d, np.minimum(exact + val.astype(np.int64), N_BUCKETS - 1))
    assert np.all(np.diff(bucket) >= 0)
    return [int(np.argmax(bucket >= b)) for b in range(N_BUCKETS)]


_BUCKET_THR = _bucket_thresholds()


def _bias_from_dist(dist, rb_ref, h):
    v = jnp.full(dist.shape, rb_ref[0, h], f32)
    for b in range(1, N_BUCKETS):
        v = jnp.where(dist >= _BUCKET_THR[b], rb_ref[b, h], v)
    return v


def _topk_mask(s, blk, k):
    nblk = s.shape[1]
    sel = jnp.zeros(s.shape, f32)
    for _ in range(k):
        m = jnp.max(s, axis=1, keepdims=True)
        cand = jnp.where((s == m) & (m > -jnp.inf), blk, nblk)
        pick = blk == jnp.min(cand, axis=1, keepdims=True)
        sel = jnp.where(pick, 1.0, sel)
        s = jnp.where(pick, -jnp.inf, s)
    return sel


def _block_mean_kernel(k_ref, o_ref):
    o_ref[0] = jnp.mean(k_ref[...], axis=0, keepdims=True)


def _block_mean(proj_b, n_blk):
    return pl.pallas_call(
        _block_mean_kernel,
        grid=(n_blk,),
        in_specs=[pl.BlockSpec((MOBA_BLOCK, MOBA_W), lambda i: (i, B_MK // MOBA_W))],
        out_specs=pl.BlockSpec((1, 1, MOBA_W), lambda i: (i, 0, 0)),
        out_shape=jax.ShapeDtypeStruct((n_blk, 1, MOBA_W), f32),
        compiler_params=_params("parallel"),
        name="moba_block_mean",
    )(proj_b)


def _moba_select_kernel(q_ref, km_ref, o_ref):
    own = pl.program_id(0)
    nblk = km_ref.shape[0]
    blk = lax.broadcasted_iota(i32, (MOBA_BLOCK, nblk), 1)
    outs = []
    for h in range(MOBA_HEADS):
        hs = slice(h * HEAD_DIM, (h + 1) * HEAD_DIM)
        s = _dot3(_split_bf16(q_ref[:, hs]), _split_bf16(km_ref[:, hs]), dot=_dot_nt)
        s = jnp.where(blk < own, s, -jnp.inf)
        outs.append(_topk_mask(s, blk, MOBA_TOPK))
    pad = LANES - MOBA_HEADS * nblk
    if pad:
        outs.append(jnp.zeros((MOBA_BLOCK, pad), f32))
    o_ref[...] = jnp.concatenate(outs, axis=1)


def _moba_select(proj_b, kmean, n_blk):
    assert n_blk * MOBA_HEADS <= LANES
    return pl.pallas_call(
        _moba_select_kernel,
        grid=(n_blk,),
        in_specs=[pl.BlockSpec((MOBA_BLOCK, MOBA_W), lambda i: (i, B_MQ // MOBA_W)),
                  pl.BlockSpec((n_blk, MOBA_W), lambda i: (0, 0))],
        out_specs=pl.BlockSpec((MOBA_BLOCK, LANES), lambda i: (i, 0)),
        out_shape=jax.ShapeDtypeStruct((n_blk * MOBA_BLOCK, LANES), f32),
        compiler_params=_params("parallel"),
        name="moba_select",
    )(proj_b, kmean)


def _moba_prompt_kernel(qi_ref, kj_ref, rb_ref, q_ref, k_ref, v_ref, sel_ref, o_ref,
                        bias_ref, m_ref, l_ref, acc_ref, *, n_blk):
    step = pl.program_id(0)
    qi = qi_ref[step]
    kj = kj_ref[step]
    nq = MOBA_BLOCK

    @pl.when(step == 0)
    def _():
        r = lax.broadcasted_iota(i32, (nq, nq), 0)
        c = lax.broadcasted_iota(i32, (nq, nq), 1)
        for h in range(MOBA_HEADS):
            bias_ref[h, 0] = jnp.where(c <= r, _bias_from_dist(r - c, rb_ref, h), NEG_INF)
            bias_ref[h, 1] = _bias_from_dist(r - c + nq, rb_ref, h)

    first = kj == qi

    @pl.when(first)
    def _():
        m_ref[...] = jnp.full(m_ref.shape, NEG_INF, f32)
        l_ref[...] = jnp.zeros_like(l_ref)
        acc_ref[...] = jnp.zeros_like(acc_ref)

    heads = range(MOBA_HEADS)
    hcols = [slice(h * HEAD_DIM, (h + 1) * HEAD_DIM) for h in heads]
    scale = HEAD_DIM ** -0.5

    def attend(far):
        sel = sel_ref[...]
        lane = lax.broadcasted_iota(i32, sel.shape, 1)
        ss = [_dot_nt(q_ref[:, hs].astype(bf16), k_ref[:, hs].astype(bf16)) for hs in hcols]
        picked = [jnp.max(jnp.where(lane == h * n_blk + kj, sel, 0.0), axis=1, keepdims=True) > 0.0 for h in heads]
        if far:
            cols = [jnp.where(pk, rb_ref[N_BUCKETS - 1, h], NEG_INF) for h, pk in zip(heads, picked)]
            ss = [s * scale + c for s, c in zip(ss, cols)]
        else:
            slot = jnp.where(first, 0, 1)
            ss = [s * scale + bias_ref[h, slot] for h, s in zip(heads, ss)]
            ss = [jnp.where(pk | first, s, NEG_INF) for pk, s in zip(picked, ss)]
        m_prevs = [m_ref[h] for h in heads]
        m_news = [jnp.maximum(mp, jnp.max(s, axis=1, keepdims=True)) for mp, s in zip(m_prevs, ss)]
        alphas = [jnp.exp(mp - mn) for mp, mn in zip(m_prevs, m_news)]
        ps = [jnp.exp(s - mn) for s, mn in zip(ss, m_news)]
        pvs = [_dot(p.astype(bf16), v_ref[:, hs].astype(bf16)) for p, hs in zip(ps, hcols)]
        for h in heads:
            l_ref[h] = alphas[h] * l_ref[h] + jnp.sum(ps[h], axis=1, keepdims=True)
            acc_ref[h] = alphas[h] * acc_ref[h] + pvs[h]
            m_ref[h] = m_news[h]

    is_far = kj < qi - 1
    pl.when(is_far)(lambda: attend(True))
    pl.when(jnp.logical_not(is_far))(lambda: attend(False))

    @pl.when(kj == 0)
    def _():
        for h in range(MOBA_HEADS):
            hs = slice(h * HEAD_DIM, (h + 1) * HEAD_DIM)
            o_ref[:, hs] = (acc_ref[h] / l_ref[h]).astype(o_ref.dtype)


def _moba_prompt(qkv, sel, rel_bias, n_blk):
    qi = np.concatenate([np.full(i + 1, i) for i in range(n_blk)]).astype(np.int32)
    kj = np.concatenate([np.arange(i, -1, -1) for i in range(n_blk)]).astype(np.int32)
    nq = MOBA_BLOCK
    grid_spec = pltpu.PrefetchScalarGridSpec(
        num_scalar_prefetch=2,
        grid=(len(qi),),
        in_specs=[
            pl.BlockSpec(memory_space=pltpu.SMEM),
            pl.BlockSpec((nq, MOBA_W), lambda s, qi, kj: (qi[s], 0)),
            pl.BlockSpec((nq, MOBA_W), lambda s, qi, kj: (kj[s], 1)),
            pl.BlockSpec((nq, MOBA_W), lambda s, qi, kj: (kj[s], 2)),
            pl.BlockSpec((nq, LANES), lambda s, qi, kj: (qi[s], 0)),
        ],
        out_specs=pl.BlockSpec((nq, MOBA_W), lambda s, qi, kj: (qi[s], 0)),
        scratch_shapes=[
            pltpu.VMEM((MOBA_HEADS, 2, nq, nq), f32),
            pltpu.VMEM((MOBA_HEADS, nq, 1), f32),
            pltpu.VMEM((MOBA_HEADS, nq, 1), f32),
            pltpu.VMEM((MOBA_HEADS, nq, HEAD_DIM), f32),
        ],
    )
    return pl.pallas_call(
        functools.partial(_moba_prompt_kernel, n_blk=n_blk),
        grid_spec=grid_spec,
        out_shape=jax.ShapeDtypeStruct((n_blk * nq, MOBA_W), bf16),
        compiler_params=_params("arbitrary"),
        name="moba_prompt",
    )(jnp.asarray(qi), jnp.asarray(kj), rel_bias, qkv, qkv, qkv, sel)


def _per_head_bias(dist, row_head, rb_ref):
    v = _bias_from_dist(dist, rb_ref, 0)
    for h in range(1, MOBA_HEADS):
        v = jnp.where(row_head == h, _bias_from_dist(dist, rb_ref, h), v)
    return v


def _moba_sample_kernel(pt_ref, rb_ref, q_ref, kn_ref, vn_ref, *rest, n_pages, page):
    kp = rest[:n_pages]
    vp = rest[n_pages:2 * n_pages]
    o_ref, bias_ref, biasn_ref = rest[2 * n_pages:]
    nh = MOBA_HEADS
    rows = q_ref.shape[0]
    prow = page * nh
    past = n_pages * page
    n_blk = past // MOBA_BLOCK
    ppb = MOBA_BLOCK // page

    @pl.when(pl.program_id(0) == 0)
    def _():
        r = lax.broadcasted_iota(i32, (rows, past * nh), 0)
        c = lax.broadcasted_iota(i32, (rows, past * nh), 1)
        bias = _per_head_bias(past + r // nh - c // nh, r % nh, rb_ref)
        bias_ref[...] = jnp.where(r % nh == c % nh, bias, NEG_INF)
        r = lax.broadcasted_iota(i32, (rows, rows), 0)
        c = lax.broadcasted_iota(i32, (rows, rows), 1)
        bias = _per_head_bias(r // nh - c // nh, r % nh, rb_ref)
        biasn_ref[...] = jnp.where((r % nh == c % nh) & (c // nh <= r // nh), bias, NEG_INF)

    kb, means = [], []
    for p in range(n_pages):
        kpage = kp[p][...]
        kb.append(kpage.astype(bf16))
        part = jnp.sum(kpage.reshape(prow // SUBLANES, SUBLANES, HEAD_DIM), axis=0)
        part = part[0:nh] + part[nh:2 * nh]
        if p % ppb == 0:
            means.append(part)
        else:
            means[-1] = means[-1] + part
    kmean = jnp.concatenate(means, axis=0) / MOBA_BLOCK

    q = q_ref[...]
    r = lax.broadcasted_iota(i32, (rows, n_blk * nh), 0)
    c = lax.broadcasted_iota(i32, (rows, n_blk * nh), 1)
    s = jnp.where(r % nh == c % nh, _dot_nt(q, kmean, precision=HIGHEST), -jnp.inf)
    sel = _topk_mask(s, c, MOBA_TOPK)
    picked = [jnp.max(jnp.where(c // nh == n, sel, 0.0), axis=1, keepdims=True) > 0.0 for n in range(n_blk)]

    qb = q.astype(bf16)
    scale = HEAD_DIM ** -0.5
    ln = _dot_nt(qb, kn_ref[...].astype(bf16)) * scale + biasn_ref[...]
    m = jnp.max(ln, axis=1, keepdims=True)
    lps = []
    for p in range(n_pages):
        lp = _dot_nt(qb, kb[p]) * scale + bias_ref[:, p * prow:(p + 1) * prow]
        lp = jnp.where(picked[p // ppb], lp, NEG_INF)
        m = jnp.maximum(m, jnp.max(lp, axis=1, keepdims=True))
        lps.append(lp)
    pn = jnp.exp(ln - m)
    den = jnp.sum(pn, axis=1, keepdims=True)
    num = _dot(pn.astype(bf16), vn_ref[...].astype(bf16))
    for p in range(n_pages):
        pp = jnp.exp(lps[p] - m)
        den = den + jnp.sum(pp, axis=1, keepdims=True)
        num = num + _dot(pp.astype(bf16), vp[p][...].astype(bf16))
    o_ref[...] = num / den


def _moba_sample(q, k_new, v_new, pool_k, pool_v, page_table, rel_bias, *, page):
    n_seq, rows, _ = q.shape
    n_pages = page_table.shape[1]
    prow = page * MOBA_HEADS
    new_spec = pl.BlockSpec((None, rows, HEAD_DIM), lambda b, pt: (b, 0, 0))

    def page_spec(p):
        return pl.BlockSpec((prow, HEAD_DIM), lambda b, pt: (pt[b, p], 0))

    grid_spec = pltpu.PrefetchScalarGridSpec(
        num_scalar_prefetch=1,
        grid=(n_seq,),
        in_specs=[pl.BlockSpec(memory_space=pltpu.SMEM), new_spec, new_spec, new_spec]
        + [page_spec(p) for p in range(n_pages)] * 2,
        out_specs=new_spec,
        scratch_shapes=[pltpu.VMEM((rows, n_pages * prow), f32), pltpu.VMEM((rows, rows), f32)],
    )
    return pl.pallas_call(
        functools.partial(_moba_sample_kernel, n_pages=n_pages, page=page),
        grid_spec=grid_spec,
        out_shape=jax.ShapeDtypeStruct((n_seq, rows, HEAD_DIM), f32),
        compiler_params=_params("arbitrary"),
        name="moba_sample",
    )(page_table, rel_bias, q, k_new, v_new, *([pool_k] * n_pages), *([pool_v] * n_pages))


def _mem_attn_kernel(q_ref, k_ref, v_ref, o_ref):
    for h in range(MEM_HEADS):
        hs = slice(h * HEAD_DIM, (h + 1) * HEAD_DIM)
        s = _dot_nt(q_ref[:, hs].astype(bf16), k_ref[:, hs].astype(bf16)) * (HEAD_DIM ** -0.5)
        p = jnp.exp(s - jnp.max(s, axis=1, keepdims=True))
        num = _dot(p.astype(bf16), v_ref[:, hs].astype(bf16))
        o_ref[:, hs] = (num / jnp.sum(p, axis=1, keepdims=True)).astype(o_ref.dtype)


def _mem_attn_prompt(proj_b, mem_kv, *, n_tok, tq):
    n_mem = mem_kv.shape[0]
    return pl.pallas_call(
        _mem_attn_kernel,
        grid=(n_tok // tq,),
        in_specs=[pl.BlockSpec((tq, MEM_W), lambda i: (i, B_CQ // MEM_W)),
                  pl.BlockSpec((n_mem, MEM_W), lambda i: (0, 0)),
                  pl.BlockSpec((n_mem, MEM_W), lambda i: (0, 1))],
        out_specs=pl.BlockSpec((tq, MEM_W), lambda i: (i, 0)),
        out_shape=jax.ShapeDtypeStruct((n_tok, MEM_W), bf16),
        compiler_params=_params("parallel"),
        name="mem_attn_prompt",
    )(proj_b, mem_kv, mem_kv)


def _mem_attn_sample_kernel(q_ref, k_ref, v_ref, o_ref):
    for b in range(q_ref.shape[0]):
        s = _dot_nt(q_ref[b].astype(bf16), k_ref[b].astype(bf16)) * (HEAD_DIM ** -0.5)
        r = lax.broadcasted_iota(i32, s.shape, 0)
        c = lax.broadcasted_iota(i32, s.shape, 1)
        s = jnp.where(r % MEM_HEADS == c % MEM_HEADS, s, NEG_INF)
        p = jnp.exp(s - jnp.max(s, axis=1, keepdims=True))
        o_ref[b] = _dot(p.astype(bf16), v_ref[b].astype(bf16)) / jnp.sum(p, axis=1, keepdims=True)


MEM_SEQ_PER_STEP = 4


def _mem_attn_sample(q, mem_k, mem_v):
    n_seq, rows, _ = q.shape
    mrows = mem_k.shape[1]
    nb = MEM_SEQ_PER_STEP
    return pl.pallas_call(
        _mem_attn_sample_kernel,
        grid=(n_seq // nb,),
        in_specs=[pl.BlockSpec((nb, rows, HEAD_DIM), lambda b: (b, 0, 0)),
                  pl.BlockSpec((nb, mrows, HEAD_DIM), lambda b: (b, 0, 0)),
                  pl.BlockSpec((nb, mrows, HEAD_DIM), lambda b: (b, 0, 0))],
        out_specs=pl.BlockSpec((nb, rows, HEAD_DIM), lambda b: (b, 0, 0)),
        out_shape=jax.ShapeDtypeStruct((n_seq, rows, HEAD_DIM), f32),
        compiler_params=_params("parallel"),
        name="mem_attn_sample",
    )(q, mem_k, mem_v)


def _outproj_router_kernel(*refs, starts):
    n_g = len(starts)
    i = pl.program_id(0)
    x, og, om, oc = [_stacked_tile(refs[k * n_g:(k + 1) * n_g], starts, i) for k in range(4)]
    w_ref, g_ref, rw_ref, rb_ref, y_ref, h_ref, ti_ref, tw_ref = refs[4 * n_g:]
    mix = (_dot(og, w_ref[0:GDN_V, :])
           + _dot(om, w_ref[GDN_V:GDN_V + MOBA_W, :])
           + _dot(oc, w_ref[GDN_V + MOBA_W:, :]))
    y = x + mix
    y_ref[...] = y
    h = _rms(y, g_ref[...])
    _store_slabs(h_ref, _pack_bf16_pairs(h))
    h_hi, h_lo = _split_bf16(h)
    w_hi, w_lo = _split_bf16(rw_ref[...])
    logits = _dot(h_hi, w_hi) + (_dot(h_hi, w_lo) + _dot(h_lo, w_hi)) + rb_ref[...]
    lane = lax.broadcasted_iota(i32, logits.shape, 1)
    s = jnp.where(lane < N_EXPERTS, logits, -jnp.inf)
    vals, idxs = [], []
    for _ in range(TOP_K):
        m = jnp.max(s, axis=1, keepdims=True)
        idx = jnp.min(jnp.where(s == m, lane, LANES), axis=1, keepdims=True)
        vals.append(m)
        idxs.append(idx)
        s = jnp.where(lane == idx, -jnp.inf, s)
    exps = [jnp.exp(v - vals[0]) for v in vals]
    den = exps[0]
    for e in exps[1:]:
        den = den + e
    ti = jnp.zeros(logits.shape, i32)
    tw = jnp.zeros(logits.shape, f32)
    for r in range(TOP_K):
        ti = jnp.where(lane == r, idxs[r], ti)
        tw = jnp.where(lane == r, exps[r] / den, tw)
    ti_ref[...] = ti
    tw_ref[...] = tw


def _outproj_router(xs, ogs, oms, ocs, w_out, g, rw, rb, *, tm):
    d = xs[0].shape[1]
    row = lambda w: pl.BlockSpec((tm, w), lambda i: (i, 0))
    full = lambda a: pl.BlockSpec(a.shape, lambda i: (0,) * a.ndim)
    specs, starts, n_tiles = [], None, None
    for group in (xs, ogs, oms, ocs):
        group_specs, starts, n_tiles = _stacked_specs(group, tm, 1)
        specs += group_specs
    n = n_tiles * tm
    return pl.pallas_call(
        functools.partial(_outproj_router_kernel, starts=starts),
        grid=(n_tiles,),
        in_specs=specs + [full(w_out), full(g), full(rw), full(rb)],
        out_specs=[row(d), pl.BlockSpec((tm * PACKED_G, LANES), lambda i: (i, 0)), row(LANES), row(LANES)],
        out_shape=[jax.ShapeDtypeStruct((n, d), f32), jax.ShapeDtypeStruct((n * PACKED_G, LANES), jnp.uint32),
                   jax.ShapeDtypeStruct((n, LANES), i32), jax.ShapeDtypeStruct((n, LANES), f32)],
        compiler_params=_params("parallel"),
        name="outproj_router",
    )(*xs, *ogs, *oms, *ocs, w_out, g, rw, rb)


GATHER_UNROLL = 8
SLAB_G = D_MODEL // LANES
PACKED_G = SLAB_G // 2


def _store_slabs(slab_ref, x, mask=None):
    rows = x.shape[0]
    g = x.shape[1] // LANES
    for c in range(g):
        idx = (pl.ds(c, rows, stride=g), slice(None))
        blk = x[:, c * LANES:(c + 1) * LANES]
        slab_ref[idx] = blk if mask is None else jnp.where(mask, blk, slab_ref[idx])


def _pack_bf16_pairs(x):
    half = x.shape[1] // 2
    bits = lambda v: lax.bitcast_convert_type(v.astype(bf16).astype(f32), jnp.uint32)
    return (bits(x[:, :half]) >> 16) | (bits(x[:, half:]) & jnp.uint32(0xFFFF0000))


def _unpack_bf16_pair(word):
    as_bf16 = lambda w: lax.bitcast_convert_type(w, f32).astype(bf16)
    return as_bf16(word << 16), as_bf16(word & jnp.uint32(0xFFFF0000))


def _load_slab_group(slab_ref, lead, c, rows, g):
    return slab_ref[lead + (pl.ds(c, rows, stride=g), slice(None))]


def _issue_slab_gather(idx_ref, idx_base, src_ref, dst_ref, sem, count, g):
    assert count % GATHER_UNROLL == 0

    def body(jj, carry):
        for u in range(GATHER_UNROLL):
            j = jj * GATHER_UNROLL + u
            tok = pl.multiple_of(idx_ref[idx_base + j] * g, g)
            dst = dst_ref.at[pl.ds(pl.multiple_of(j * g, g), g)]
            pltpu.make_async_copy(src_ref.at[pl.ds(tok, g)], dst, sem).start(priority=u % 2)
        return carry

    lax.fori_loop(0, count // GATHER_UNROLL, body, 0)


def _wait_slab_gather(src_ref, dst_ref, sem):
    pltpu.make_async_copy(src_ref.at[pl.ds(0, dst_ref.shape[0])], dst_ref, sem).wait()


def _gather_x_kernel(idx_ref, src_ref, o_ref, buf_ref, sem, *, rows, g):
    i = pl.program_id(0)
    slot = i % 2

    @pl.when(i == 0)
    def _():
        _issue_slab_gather(idx_ref, 0, src_ref, buf_ref.at[0], sem.at[0], rows, g)

    @pl.when(i + 1 < pl.num_programs(0))
    def _():
        _issue_slab_gather(idx_ref, (i + 1) * rows, src_ref, buf_ref.at[1 - slot], sem.at[1 - slot], rows, g)

    _wait_slab_gather(src_ref, buf_ref.at[slot], sem.at[slot])
    for c in range(g):
        lo, hi = _unpack_bf16_pair(_load_slab_group(buf_ref, (slot,), c, rows, g))
        o_ref[:, c * LANES:(c + 1) * LANES] = lo
        o_ref[:, (g + c) * LANES:(g + c + 1) * LANES] = hi


def _gather_x(src, idx, *, rows, g):
    n_out = idx.shape[0]
    grid_spec = pltpu.PrefetchScalarGridSpec(
        num_scalar_prefetch=1,
        grid=(n_out // rows,),
        in_specs=[pl.BlockSpec(memory_space=pl.ANY)],
        out_specs=pl.BlockSpec((rows, 2 * g * LANES), lambda i, idx: (i, 0)),
        scratch_shapes=[pltpu.VMEM((2, rows * g, LANES), src.dtype), pltpu.SemaphoreType.DMA((2,))],
    )
    return pl.pallas_call(
        functools.partial(_gather_x_kernel, rows=rows, g=g),
        grid_spec=grid_spec,
        out_shape=jax.ShapeDtypeStruct((n_out, 2 * g * LANES), bf16),
        compiler_params=_params("arbitrary"),
        name="moe_gather_x",
    )(idx, src)


def _new_expert(ev_ref, v):
    return (v == 0) | (ev_ref[v] != ev_ref[jnp.maximum(v - 1, 0)])


def _expert_weights_step(copies, is_new, run, n_runs, e_here, e_next, e_first, sweep, n_sweeps, on_arrival):
    slot = (run + sweep * n_runs) % 2

    @pl.when(is_new)
    def _():
        @pl.when((sweep == 0) & (run == 0))
        def _():
            for c in copies(e_here, sweep, slot):
                c.start()

        for c in copies(e_here, sweep, slot):
            c.wait()
        on_arrival(slot)
        last = run == n_runs - 1

        @pl.when(jnp.logical_not(last))
        def _():
            for c in copies(e_next, sweep, 1 - slot):
                c.start()

        @pl.when(last & (sweep + 1 < n_sweeps))
        def _():
            for c in copies(e_first, sweep + 1, 1 - slot):
                c.start()


def _moe_gate_up_kernel(tv_ref, ev_ref, lo_ref, hi_ref, run_ref, nxt_ref, nrun_ref, x_ref, w_hbm, b_ref,
                        o_ref, wf_ref, wb_ref, sem, *, tf):
    v = pl.program_id(0)
    lo = lo_ref[v]
    hi = hi_ref[v]

    def copies(e, sweep, slot):
        return [pltpu.make_async_copy(w_hbm.at[e], wf_ref, sem.at[0])]

    def on_arrival(slot):
        wb_ref[...] = wf_ref[...].astype(bf16)

    _expert_weights_step(copies, _new_expert(ev_ref, v), run_ref[v], nrun_ref[0], ev_ref[v], nxt_ref[v], ev_ref[0],
                         0, 1, on_arrival)

    def activations(c0):
        x = x_ref[...]
        gate = _dot(x, wb_ref[:, c0:c0 + tf]) + b_ref[:, c0:c0 + tf]
        up = _dot(x, wb_ref[:, D_FF + c0:D_FF + c0 + tf]) + b_ref[:, D_FF + c0:D_FF + c0 + tf]
        gate = jnp.minimum(gate, SWIGLU_LIMIT)
        up = jnp.clip(up, -SWIGLU_LIMIT, SWIGLU_LIMIT)
        return (gate * jax.nn.sigmoid(SWIGLU_ALPHA * gate) * (up + 1.0)).astype(o_ref.dtype)

    @pl.when(hi > lo)
    def _():
        @pl.when(lo == 0)
        def _():
            for c0 in range(0, D_FF, tf):
                o_ref[:, c0:c0 + tf] = activations(c0)

        @pl.when(lo > 0)
        def _():
            r = lax.broadcasted_iota(i32, (o_ref.shape[0], tf), 0)
            mine = (r >= lo) & (r < hi)
            for c0 in range(0, D_FF, tf):
                o_ref[:, c0:c0 + tf] = jnp.where(mine, activations(c0), o_ref[:, c0:c0 + tf])


def _moe_gate_up(xs, w_gu, b_gu3, visits, *, tm, tf):
    rows, d = xs.shape
    grid_spec = pltpu.PrefetchScalarGridSpec(
        num_scalar_prefetch=len(visits),
        grid=(visits[0].shape[0],),
        in_specs=[
            pl.BlockSpec((tm, d), lambda v, tv, *_: (tv[v], 0)),
            pl.BlockSpec(memory_space=pl.ANY),
            pl.BlockSpec((None, 1, 2 * D_FF), lambda v, tv, ev, *_: (ev[v], 0, 0)),
        ],
        out_specs=pl.BlockSpec((tm, D_FF), lambda v, tv, *_: (tv[v], 0)),
        scratch_shapes=[pltpu.VMEM((d, 2 * D_FF), f32), pltpu.VMEM((d, 2 * D_FF), bf16),
                        pltpu.SemaphoreType.DMA((1,))],
    )
    return pl.pallas_call(
        functools.partial(_moe_gate_up_kernel, tf=tf),
        grid_spec=grid_spec,
        out_shape=jax.ShapeDtypeStruct((rows, D_FF), bf16),
        compiler_params=_params("arbitrary"),
        name="moe_gate_up",
    )(*visits, xs, w_gu, b_gu3)


def _moe_down_kernel(tv_ref, ev_ref, lo_ref, hi_ref, run_ref, nxt_ref, nrun_ref, a_ref, w_hbm, b_ref,
                     o_ref, wf_ref, wb_ref, sem):
    v = pl.program_id(0)
    lo = lo_ref[v]
    hi = hi_ref[v]

    def copies(e, sweep, slot):
        return [pltpu.make_async_copy(w_hbm.at[e], wf_ref.at[slot], sem.at[slot])]

    def on_arrival(slot):
        wb_ref[...] = wf_ref[slot].astype(bf16)

    _expert_weights_step(copies, _new_expert(ev_ref, v), run_ref[v], nrun_ref[0], ev_ref[v], nxt_ref[v], ev_ref[0],
                         0, 1, on_arrival)

    @pl.when(hi > lo)
    def _():
        @pl.when(lo == 0)
        def _():
            _store_slabs(o_ref, _dot(a_ref[...], wb_ref[...]) + b_ref[...])

        @pl.when(lo > 0)
        def _():
            r = lax.broadcasted_iota(i32, (a_ref.shape[0], LANES), 0)
            _store_slabs(o_ref, _dot(a_ref[...], wb_ref[...]) + b_ref[...], mask=(r >= lo) & (r < hi))


def _moe_down(act, w_dn, b_dn3, visits, *, tm):
    rows, dff = act.shape
    d = w_dn.shape[2]
    g = d // LANES
    grid_spec = pltpu.PrefetchScalarGridSpec(
        num_scalar_prefetch=len(visits),
        grid=(visits[0].shape[0],),
        in_specs=[
            pl.BlockSpec((tm, dff), lambda v, tv, *_: (tv[v], 0)),
            pl.BlockSpec(memory_space=pl.ANY),
            pl.BlockSpec((None, 1, d), lambda v, tv, ev, *_: (ev[v], 0, 0)),
        ],
        out_specs=pl.BlockSpec((tm * g, LANES), lambda v, tv, *_: (tv[v], 0)),
        scratch_shapes=[pltpu.VMEM((2, dff, d), f32), pltpu.VMEM((dff, d), bf16), pltpu.SemaphoreType.DMA((2,))],
    )
    return pl.pallas_call(
        _moe_down_kernel,
        grid_spec=grid_spec,
        out_shape=jax.ShapeDtypeStruct((rows * g, LANES), f32),
        compiler_params=_params("arbitrary"),
        name="moe_down",
    )(*visits, act, w_dn, b_dn3)


def _moe_combine_kernel(pos_ref, y_ref, tw_ref, ys_ref, oa_ref, ob_ref, buf_ref, sem, *, rows, g, tiles_a):
    i = pl.program_id(0)
    slot = i % 2
    n_tok = pl.num_programs(0) * rows

    def issue(tile, s):
        for k in range(TOP_K):
            _issue_slab_gather(pos_ref, k * n_tok + tile * rows, ys_ref, buf_ref.at[s, k], sem.at[s], rows, g)

    @pl.when(i == 0)
    def _():
        issue(0, 0)

    @pl.when(i + 1 < pl.num_programs(0))
    def _():
        issue(i + 1, 1 - slot)

    for k in range(TOP_K):
        _wait_slab_gather(ys_ref, buf_ref.at[slot, k], sem.at[slot])
    tw = tw_ref[...]

    def combine(o_ref):
        for c in range(g):
            cs = slice(c * LANES, (c + 1) * LANES)
            acc = y_ref[:, cs]
            for k in range(TOP_K):
                acc = acc + _load_slab_group(buf_ref, (slot, k), c, rows, g) * tw[:, k:k + 1]
            o_ref[:, cs] = acc

    pl.when(i < tiles_a)(lambda: combine(oa_ref))
    pl.when(i >= tiles_a)(lambda: combine(ob_ref))


def _moe_combine(y1, ys, pos_kmajor, tw, *, rows, n_a):
    n, d = y1.shape
    g = d // LANES
    tiles_a = n_a // rows
    tiles = n // rows
    grid_spec = pltpu.PrefetchScalarGridSpec(
        num_scalar_prefetch=1,
        grid=(tiles,),
        in_specs=[pl.BlockSpec((rows, d), lambda i, pos: (i, 0)),
                  pl.BlockSpec((rows, LANES), lambda i, pos: (i, 0)),
                  pl.BlockSpec(memory_space=pl.ANY)],
        out_specs=[pl.BlockSpec((rows, d), lambda i, pos: (jnp.minimum(i, tiles_a - 1), 0)),
                   pl.BlockSpec((rows, d), lambda i, pos: (jnp.maximum(i - tiles_a, 0), 0))],
        scratch_shapes=[pltpu.VMEM((2, TOP_K, rows * g, LANES), f32), pltpu.SemaphoreType.DMA((2,))],
    )
    return pl.pallas_call(
        functools.partial(_moe_combine_kernel, rows=rows, g=g, tiles_a=tiles_a),
        grid_spec=grid_spec,
        out_shape=[jax.ShapeDtypeStruct((n_a, d), f32), jax.ShapeDtypeStruct((n - n_a, d), f32)],
        compiler_params=_params("arbitrary"),
        name="moe_combine",
    )(pos_kmajor, y1, tw, ys)


def _sort_rows(top_i):
    e = top_i.reshape(-1)
    a = e.shape[0]
    order = jnp.argsort(e, stable=True).astype(i32)
    tok_sorted = order // TOP_K
    blk = LANES
    onehot = (e[:, None] == jnp.arange(N_EXPERTS, dtype=i32)[None, :]).astype(f32).reshape(a // blk, blk, N_EXPERTS)
    earlier = (jnp.arange(blk)[:, None] > jnp.arange(blk)[None, :]).astype(f32)
    within = jnp.einsum("ij,bjk->bik", earlier, onehot)
    block_sums = jnp.sum(onehot, axis=1)
    block_off = jnp.cumsum(block_sums, axis=0) - block_sums
    counts = jnp.sum(block_sums, axis=0)
    starts = jnp.cumsum(counts) - counts
    pos = jnp.sum(onehot * (within + block_off[:, None, :] + starts[None, None, :]), axis=2).reshape(a)
    return tok_sorted, pos.astype(i32), counts.astype(i32)


def _visits(counts, n_rows, *, tm):
    ends = jnp.cumsum(counts)
    starts = ends - counts
    first_tile = starts // tm
    last_tile = jnp.maximum(ends - 1, 0) // tm
    n_vis = jnp.where(counts > 0, last_tile - first_tile + 1, 0)
    vis_end = jnp.cumsum(n_vis)
    vis_start = vis_end - n_vis
    total = vis_end[-1]
    v = jnp.arange(n_rows // tm + N_EXPERTS, dtype=i32)
    vc = jnp.minimum(v, total - 1)
    ev = jnp.sum((vis_end[None, :] <= vc[:, None]).astype(i32), axis=1)
    tv = first_tile[ev] + (vc - vis_start[ev])
    lo = jnp.maximum(starts[ev], tv * tm) - tv * tm
    hi = jnp.minimum(ends[ev], (tv + 1) * tm) - tv * tm
    valid = v < total
    experts = jnp.arange(N_EXPERTS, dtype=i32)
    present = n_vis > 0
    n_runs = jnp.sum(present.astype(i32))
    run = jnp.sum((present[None, :] & (experts[None, :] < ev[:, None])).astype(i32), axis=1)
    rank = jnp.cumsum(present.astype(i32))
    run_expert = jnp.sum((rank[None, :] <= experts[:, None]).astype(i32), axis=1)
    nxt = run_expert[jnp.minimum(run + 1, n_runs - 1)]
    return tv, ev, jnp.where(valid, lo, 0), jnp.where(valid, hi, 0), run, nxt, n_runs[None]


GATE_UP_TM = 256
GATE_UP_TF = 1024
DOWN_TM = 256
GATHER_ROWS = 256
COMBINE_ROWS = 128


def _moe(y1, h_slabs, top_i, top_w, w_gu, b_gu, w_dn, b_dn, *, n_a):
    n_tok = y1.shape[0]
    tok_sorted, pos, counts = _sort_rows(top_i[:, :TOP_K])
    n_rows = tok_sorted.shape[0]
    xs = _gather_x(h_slabs, tok_sorted, rows=GATHER_ROWS, g=PACKED_G)
    act = _moe_gate_up(xs, w_gu, b_gu[:, None, :], _visits(counts, n_rows, tm=GATE_UP_TM),
                       tm=GATE_UP_TM, tf=GATE_UP_TF)
    ys = _moe_down(act, w_dn, b_dn[:, None, :], _visits(counts, n_rows, tm=DOWN_TM), tm=DOWN_TM)
    pos_kmajor = pos.reshape(n_tok, TOP_K).T.reshape(-1)
    return _moe_combine(y1, ys, pos_kmajor, top_w, rows=COMBINE_ROWS, n_a=n_a)


def kernel(x_prompt, x_sample, cache_moba_k, cache_moba_v, state_gdn, state_gdn_conv, cache_mem_k, cache_mem_v, page_table, mem_prompt, rel_bias, norm_mix, w_in, conv_w, a_log, dt_bias, gdn_o_norm, moba_q_norm, moba_k_norm, mem_q_norm, mem_norm, w_mem_kv, mem_k_norm, w_out, norm_ffn, router_w, router_b, w_gu, b_gu, w_dn, b_dn):
    assert x_prompt.shape[0] == 1 and all(a.shape[0] == 1 for a in (w_in, w_out, w_gu, w_dn, state_gdn))
    n_p = x_prompt.shape[1]
    n_seq, n_st = x_sample.shape[:2]
    n_s = n_seq * n_st
    d = x_prompt.shape[2]
    n_blk = n_p // MOBA_BLOCK
    xs = (x_prompt.reshape(n_p, d), x_sample.reshape(n_s, d))

    w = w_in[0]
    off_beta = A_W
    off_moba = off_beta + 2 * GDN_HEADS
    off_mem = off_moba + 3 * MOBA_W
    w_a = w[:, :A_W].astype(bf16)
    w_b = jnp.concatenate(
        [w[:, off_moba:off_moba + MOBA_W], w[:, off_moba + MOBA_W:off_moba + 2 * MOBA_W], w[:, off_mem:],
         w[:, off_moba + 2 * MOBA_W:off_mem], w[:, off_beta:off_moba],
         jnp.zeros((d, LANES - 2 * GDN_HEADS), w.dtype)], axis=1).astype(bf16)
    gain_b = jnp.concatenate([jnp.tile(moba_q_norm[0], MOBA_HEADS), jnp.tile(moba_k_norm[0], MOBA_HEADS),
                              jnp.tile(mem_q_norm[0], MEM_HEADS)])[None]
    proj_a = _norm_matmul(xs, norm_mix, w_a, gain_b, tm=512, tn=1024, n_norm=0)
    head_groups = lambda col: list(range(col // LANES, (col + MOBA_W) // LANES))
    proj_b, mk_p, mk_s, mv_p, mv_s, moba_qkv16 = _norm_matmul(
        xs, norm_mix, w_b, gain_b, tm=256, tn=B_W, n_norm=B_NORM_GROUPS, head_outs=(B_MK // LANES, B_MV // LANES),
        bf16_groups=head_groups(B_MQ) + head_groups(B_MK) + head_groups(B_MV))
    proj_b3 = proj_b[n_p:].reshape(n_seq, n_st, B_W)

    pad_row = lambda a: jnp.zeros((1, LANES), f32).at[0, GDN_HEADS:2 * GDN_HEADS].set(a[0])
    alog_row, dtb_row = pad_row(a_log), pad_row(dt_bias)
    o_gdn_p, p_gdn = _gdn_prompt(proj_a, proj_b, conv_w[0], alog_row, dtb_row, gdn_o_norm, n_tok=n_p)
    p_conv = proj_a[n_p - (GDN_CONV - 1):n_p, :GDN_CONV_DIM]
    xa_s = proj_a[n_p:].reshape(n_seq, n_st, A_W)
    o_gdn_s, s_gdn = _gdn_sample(jnp.swapaxes(xa_s, 0, 1), jnp.swapaxes(proj_b3[..., B_BA:], 0, 1),
                                 jnp.swapaxes(state_gdn_conv[0], 0, 1), conv_w[0], alog_row, dtb_row,
                                 gdn_o_norm, state_gdn[0])
    o_gdn_s = jnp.swapaxes(o_gdn_s, 0, 1).reshape(n_s, GDN_V).astype(bf16)
    s_conv = jnp.concatenate([state_gdn_conv[0], xa_s[..., :GDN_CONV_DIM]], axis=1)[:, n_st:]

    kmean = _block_mean(proj_b, n_blk).reshape(n_blk, MOBA_W)
    sel = _moba_select(proj_b, kmean, n_blk)
    o_moba_p = _moba_prompt(moba_qkv16, sel, rel_bias, n_blk)
    pairs = lambda col: proj_b3[..., col:col + MOBA_W].reshape(n_seq, n_st * MOBA_HEADS, HEAD_DIM)
    pair_rows = n_st * MOBA_HEADS
    o_moba_s = _moba_sample(pairs(B_MQ), mk_s.reshape(n_seq, pair_rows, HEAD_DIM),
                            mv_s.reshape(n_seq, pair_rows, HEAD_DIM), cache_moba_k.reshape(-1, HEAD_DIM),
                            cache_moba_v.reshape(-1, HEAD_DIM), page_table, rel_bias, page=cache_moba_k.shape[2])
    o_moba_s = o_moba_s.reshape(n_s, MOBA_W).astype(bf16)

    n_mem = mem_prompt.shape[1]
    mem_kv = _norm_matmul((mem_prompt[0],), mem_norm, w_mem_kv[0].astype(bf16),
                          jnp.tile(mem_k_norm[0], MEM_HEADS)[None], tm=n_mem, tn=2 * MEM_W, n_norm=MEM_HEADS)
    o_mem_p = _mem_attn_prompt(proj_b, mem_kv, n_tok=n_p, tq=512)
    o_mem_s = _mem_attn_sample(pairs(B_CQ), cache_mem_k.reshape(n_seq, n_mem * MEM_HEADS, HEAD_DIM),
                               cache_mem_v.reshape(n_seq, n_mem * MEM_HEADS, HEAD_DIM))
    o_mem_s = o_mem_s.reshape(n_s, MEM_W).astype(bf16)

    rw = jnp.pad(router_w[0], ((0, 0), (0, LANES - N_EXPERTS)))
    rb = jnp.pad(router_b, ((0, 0), (0, LANES - N_EXPERTS)))
    y1, h2, top_i, top_w = _outproj_router(
        xs, (o_gdn_p, o_gdn_s), (o_moba_p, o_moba_s), (o_mem_p, o_mem_s),
        w_out[0].astype(bf16), norm_ffn, rw, rb, tm=256)
    y_p, y_s = _moe(y1, h2, top_i, top_w, w_gu[0], b_gu[0], w_dn[0], b_dn[0], n_a=n_p)

    heads = lambda a, lead: a.reshape(lead + (MOBA_HEADS, HEAD_DIM))
    return (y_p.reshape(1, n_p, d), y_s.reshape(n_seq, n_st, d),
            heads(mk_p, (1, 1, n_p)), heads(mv_p, (1, 1, n_p)),
            p_gdn[None, None], p_conv[None, None],
            heads(mem_kv[:, :MEM_W], (1, 1, n_mem)), heads(mem_kv[:, MEM_W:], (1, 1, n_mem)),
            heads(mk_s, (1, n_seq, n_st)), heads(mv_s, (1, n_seq, n_st)),
            s_gdn[None], s_conv[None])
```

```python
import functools
import math

import numpy as np
import jax
import jax.numpy as jnp
from jax import lax
from jax.experimental import pallas as pl
from jax.experimental.pallas import tpu as pltpu

f32 = jnp.float32
bf16 = jnp.bfloat16
i32 = jnp.int32
HIGHEST = lax.Precision.HIGHEST

LANES = 128
SUBLANES = 8
VMEM_LIMIT = 60 * 1024 * 1024

D_MODEL = 2048
HEAD_DIM = 128
GDN_HEADS = 8
GDN_CONV = 4
GDN_CHUNK = 64
GDN_QK = GDN_HEADS * HEAD_DIM
GDN_V = GDN_HEADS * HEAD_DIM
GDN_CONV_DIM = 2 * GDN_QK + GDN_V
MOBA_HEADS = 4
MOBA_BLOCK = 256
MOBA_TOPK = 3
MOBA_W = MOBA_HEADS * HEAD_DIM
MEM_HEADS = 4
MEM_W = MEM_HEADS * HEAD_DIM
N_BUCKETS = 32
MAX_DISTANCE = 128
N_EXPERTS = 32
TOP_K = 4
D_FF = D_MODEL
SWIGLU_LIMIT = 7.0
SWIGLU_ALPHA = 1.702
EPS = 1e-6
NEG_INF = -1e30

A_W = GDN_CONV_DIM + GDN_V
B_MQ, B_MK, B_CQ, B_MV, B_BA = 0, MOBA_W, 2 * MOBA_W, 3 * MOBA_W, 4 * MOBA_W
B_W = B_BA + LANES
B_NORM_GROUPS = 3 * MOBA_HEADS


def _params(*sem):
    return pltpu.CompilerParams(dimension_semantics=sem, vmem_limit_bytes=VMEM_LIMIT)


def _dot(a, b, **kw):
    return jnp.dot(a, b, preferred_element_type=f32, **kw)


def _dot_nt(a, b, **kw):
    return lax.dot_general(a, b, (((1,), (1,)), ((), ())), preferred_element_type=f32, **kw)


def _dot_tn(a, b, **kw):
    return lax.dot_general(a, b, (((0,), (0,)), ((), ())), preferred_element_type=f32, **kw)


def _rms(x, gain):
    return x * lax.rsqrt(jnp.mean(x * x, axis=-1, keepdims=True) + EPS) * gain


def _silu(x):
    return x * jax.nn.sigmoid(x)


def _stacked_specs(arrays, tm, grid_rank):
    counts = [a.shape[0] // tm for a in arrays]
    starts = [sum(counts[:k]) for k in range(len(arrays))]

    def spec(k):
        def index_map(i, *_):
            return (jnp.clip(i - starts[k], 0, counts[k] - 1), 0)
        return pl.BlockSpec((tm, arrays[k].shape[1]), index_map)

    return [spec(k) for k in range(len(arrays))], starts, sum(counts)


def _stacked_tile(refs, starts, i):
    x = refs[0][...]
    for ref, start in zip(refs[1:], starts[1:]):
        x = jnp.where(i >= start, ref[...], x)
    return x


def _norm_matmul_kernel(*refs, n_norm, starts, head_outs, n_heads, bf16_groups):
    n_x = len(starts)
    x_refs = refs[:n_x]
    g_ref, w_ref, gain_ref, o_ref = refs[n_x:n_x + 4]
    n_pair = len(head_outs) * n_x
    pair_refs = refs[n_x + 4:n_x + 4 + n_pair]
    o16_ref = refs[n_x + 4 + n_pair] if bf16_groups else None
    h_ref = refs[-1]
    i = pl.program_id(0)

    @pl.when(pl.program_id(1) == 0)
    def _():
        h_ref[...] = _rms(_stacked_tile(x_refs, starts, i), g_ref[...]).astype(bf16)

    acc = _dot(h_ref[...], w_ref[...])
    if n_norm == 0 and not head_outs and not bf16_groups:
        o_ref[...] = acc
        return
    tm = acc.shape[0]
    blks = []
    for gi in range(acc.shape[1] // LANES):
        sl = slice(gi * LANES, (gi + 1) * LANES)
        blk = acc[:, sl]
        if gi < n_norm:
            blk = _rms(blk, gain_ref[:, sl])
        o_ref[:, sl] = blk
        blks.append(blk)
    for idx, gi in enumerate(bf16_groups):
        o16_ref[:, idx * LANES:(idx + 1) * LANES] = blks[gi].astype(bf16)
    bounds = list(starts[1:]) + [None]
    for e, first in enumerate(head_outs):
        for k in range(n_x):
            ref = pair_refs[e * n_x + k]
            mine = (i >= starts[k]) if bounds[k] is None else ((i >= starts[k]) & (i < bounds[k]))

            @pl.when(mine)
            def _(ref=ref, first=first):
                for h in range(n_heads):
                    ref[pl.ds(h, tm, stride=n_heads), :] = blks[first + h]


def _norm_matmul(xs, g, w, gain, *, tm, tn, n_norm, head_outs=(), n_heads=MOBA_HEADS, bf16_groups=()):
    d = xs[0].shape[1]
    wn = w.shape[1]
    assert not (head_outs or bf16_groups) or tn == wn
    x_specs, starts, n_tiles = _stacked_specs(xs, tm, 2)
    counts = [x.shape[0] // tm for x in xs]
    out_specs = [pl.BlockSpec((tm, tn), lambda i, j: (i, j))]
    out_shape = [jax.ShapeDtypeStruct((n_tiles * tm, wn), f32)]
    for _ in head_outs:
        for k, x in enumerate(xs):
            out_specs.append(pl.BlockSpec(
                (tm * n_heads, LANES), lambda i, j, k=k: (jnp.clip(i - starts[k], 0, counts[k] - 1), 0)))
            out_shape.append(jax.ShapeDtypeStruct((x.shape[0] * n_heads, LANES), f32))
    if bf16_groups:
        out_specs.append(pl.BlockSpec((tm, len(bf16_groups) * LANES), lambda i, j: (i, 0)))
        out_shape.append(jax.ShapeDtypeStruct((n_tiles * tm, len(bf16_groups) * LANES), bf16))
    outs = pl.pallas_call(
        functools.partial(_norm_matmul_kernel, n_norm=n_norm, starts=starts, head_outs=tuple(head_outs),
                          n_heads=n_heads, bf16_groups=tuple(bf16_groups)),
        grid=(n_tiles, wn // tn),
        in_specs=x_specs + [
            pl.BlockSpec((1, d), lambda i, j: (0, 0)),
            pl.BlockSpec((d, tn), lambda i, j: (0, j)),
            pl.BlockSpec((1, gain.shape[1]), lambda i, j: (0, 0)),
        ],
        out_specs=out_specs,
        out_shape=out_shape,
        scratch_shapes=[pltpu.VMEM((tm, d), bf16)],
        compiler_params=_params("arbitrary", "arbitrary"),
        name="norm_matmul",
    )(*xs, g, w, gain)
    return outs if (head_outs or bf16_groups) else outs[0]


INV_BLOCK = 16


def _split_bf16(x):
    hi = x.astype(bf16)
    return hi, (x - hi.astype(f32)).astype(bf16)


def _dot3(a, b, dot=_dot):
    a_hi, a_lo = a
    b_hi, b_lo = b
    return dot(a_hi, b_hi) + (dot(a_hi, b_lo) + dot(a_lo, b_hi))


def _unit_lower_inverses(lmats):
    c = lmats[0].shape[0]
    row = lax.broadcasted_iota(i32, (c, c), 0)
    col = lax.broadcasted_iota(i32, (c, c), 1)
    same = lambda n: (row // n) == (col // n)
    eye = jnp.where(row == col, 1.0, 0.0).astype(f32)
    ms = [jnp.where(same(INV_BLOCK), l, 0.0) for l in lmats]
    ps = [eye - m for m in ms]
    power = 1
    while 2 * power < INV_BLOCK:
        sm = [_split_bf16(m) for m in ms]
        ms = [_dot3(s, s) for s in sm]
        sm = [_split_bf16(m) for m in ms]
        ps = [p + _dot3(_split_bf16(p), s) for p, s in zip(ps, sm)]
        power *= 2
    n = 2 * INV_BLOCK
    while n <= c:
        offs = [_split_bf16(jnp.where(same(n) & ~same(n // 2), l, 0.0)) for l in lmats]
        sp = [_split_bf16(p) for p in ps]
        mids = [_dot3(o, s) for o, s in zip(offs, sp)]
        ps = [p - _dot3(s, _split_bf16(m)) for p, s, m in zip(ps, sp, mids)]
        n *= 2
    return ps


def _gdn_prompt_kernel(qkv_ref, z_ref, ba_ref, convw_ref, alog_ref, dtb_ref, onorm_ref,
                       o_ref, s_out_ref, xp_ref, s_ref):
    c = GDN_CHUNK
    step = pl.program_id(0)

    @pl.when(step == 0)
    def _():
        xp_ref[0:SUBLANES, :] = jnp.zeros((SUBLANES, GDN_CONV_DIM), f32)
        s_ref[...] = jnp.zeros_like(s_ref)

    xp_ref[SUBLANES:SUBLANES + c, :] = qkv_ref[...]
    w = convw_ref[...]
    y = None
    for j in range(GDN_CONV):
        lo = SUBLANES - (GDN_CONV - 1) + j
        term = xp_ref[lo:lo + c, :] * w[j:j + 1, :]
        y = term if y is None else y + term
    y = _silu(y)
    xp_ref[0:SUBLANES, :] = xp_ref[c:c + SUBLANES, :]

    ba = ba_ref[...]
    beta_all = jax.nn.sigmoid(ba)
    g_all = -jnp.exp(alog_ref[...]) * jax.nn.softplus(ba + dtb_ref[...])
    row = lax.broadcasted_iota(i32, (c, c), 0)
    col = lax.broadcasted_iota(i32, (c, c), 1)
    incl = row >= col
    strict = row > col
    gcum_all = _dot(jnp.where(incl, 1.0, 0.0).astype(f32), g_all, precision=HIGHEST)
    gcum_t = gcum_all.T
    onorm = onorm_ref[...]

    heads = range(GDN_HEADS)
    head_cols = lambda off, h: slice(off + h * HEAD_DIM, off + (h + 1) * HEAD_DIM)
    qs = [y[:, head_cols(0, h)] for h in heads]
    ks = [y[:, head_cols(GDN_QK, h)] for h in heads]
    vs = [y[:, head_cols(2 * GDN_QK, h)] for h in heads]
    qs = [q * lax.rsqrt(jnp.sum(q * q, axis=-1, keepdims=True) + EPS) * (HEAD_DIM ** -0.5) for q in qs]
    ks = [k * lax.rsqrt(jnp.sum(k * k, axis=-1, keepdims=True) + EPS) for k in ks]
    betas = [beta_all[:, h:h + 1] for h in heads]
    gcs = [gcum_all[:, GDN_HEADS + h:GDN_HEADS + h + 1] for h in heads]
    grs = [gcum_t[GDN_HEADS + h:GDN_HEADS + h + 1, :] for h in heads]
    g_lasts = [gcum_all[c - 1:c, GDN_HEADS + h:GDN_HEADS + h + 1] for h in heads]
    decays = [jnp.exp(jnp.where(incl, gc - gr, -jnp.inf)) for gc, gr in zip(gcs, grs)]
    kbetas = [k * b for k, b in zip(ks, betas)]
    lmats = [_dot_nt(kb, k) * jnp.where(strict, dc, 0.0) for kb, k, dc in zip(kbetas, ks, decays)]
    tinvs = _unit_lower_inverses(lmats)
    egcs = [jnp.exp(gc) for gc in gcs]
    us = [_dot(t, v * b) for t, v, b in zip(tinvs, vs, betas)]
    ws = [_dot(t, kb * e) for t, kb, e in zip(tinvs, kbetas, egcs)]
    intras = [_dot_nt(q, k) * dc for q, k, dc in zip(qs, ks, decays)]
    kdecs = [k * jnp.exp(gl - gc) for k, gl, gc in zip(ks, g_lasts, gcs)]
    ss = [s_ref[h] for h in heads]
    v_news = [u - _dot(w_, s) for u, w_, s in zip(us, ws, ss)]
    os_ = [_dot(q * e, s) + _dot(it, vn) for q, e, s, it, vn in zip(qs, egcs, ss, intras, v_news)]
    for h in heads:
        s_ref[h] = ss[h] * jnp.exp(g_lasts[h]) + _dot_tn(kdecs[h], v_news[h])
    for h in heads:
        hs = head_cols(0, h)
        o_ref[:, hs] = (_rms(os_[h], onorm) * _silu(z_ref[:, hs])).astype(o_ref.dtype)

    @pl.when(step == pl.num_programs(0) - 1)
    def _():
        s_out_ref[...] = s_ref[...]


def _gdn_prompt(proj_a, proj_b, conv_w, alog_row, dtb_row, onorm, *, n_tok):
    c = GDN_CHUNK
    zblk = GDN_CONV_DIM // GDN_V
    return pl.pallas_call(
        _gdn_prompt_kernel,
        grid=(n_tok // c,),
        in_specs=[
            pl.BlockSpec((c, GDN_CONV_DIM), lambda i: (i, 0)),
            pl.BlockSpec((c, GDN_V), lambda i: (i, zblk)),
            pl.BlockSpec((c, LANES), lambda i: (i, B_BA // LANES)),
            pl.BlockSpec((GDN_CONV, GDN_CONV_DIM), lambda i: (0, 0)),
            pl.BlockSpec((1, LANES), lambda i: (0, 0)),
            pl.BlockSpec((1, LANES), lambda i: (0, 0)),
            pl.BlockSpec((1, HEAD_DIM), lambda i: (0, 0)),
        ],
        out_specs=[
            pl.BlockSpec((c, GDN_V), lambda i: (i, 0)),
            pl.BlockSpec((GDN_HEADS, HEAD_DIM, HEAD_DIM), lambda i: (0, 0, 0)),
        ],
        out_shape=[
            jax.ShapeDtypeStruct((n_tok, GDN_V), bf16),
            jax.ShapeDtypeStruct((GDN_HEADS, HEAD_DIM, HEAD_DIM), f32),
        ],
        scratch_shapes=[
            pltpu.VMEM((c + SUBLANES, GDN_CONV_DIM), f32),
            pltpu.VMEM((GDN_HEADS, HEAD_DIM, HEAD_DIM), f32),
        ],
        compiler_params=_params("arbitrary"),
        name="gdn_prompt",
    )(proj_a, proj_a, proj_b, conv_w, alog_row, dtb_row, onorm)


GDN_S_SEQ = SUBLANES


def _gdn_sample_kernel(q_ref, k_ref, v_ref, z_ref, ba_ref, bq_ref, bk_ref, bv_ref, wq_ref, wk_ref, wv_ref,
                       alog_ref, dtb_ref, onorm_ref, s_ref, o_ref, so_ref, *, n_tok):
    h = pl.program_id(1)
    nb = GDN_S_SEQ
    lane = lax.broadcasted_iota(i32, (nb, LANES), 1)

    def conv(x_ref, buf_ref, w_ref, t):
        y = None
        for j in range(GDN_CONV):
            i = t + j
            row = buf_ref[i] if i < GDN_CONV - 1 else x_ref[i - (GDN_CONV - 1)]
            term = row * w_ref[j:j + 1, :]
            y = term if y is None else y + term
        return _silu(y)

    def lane_col(x, idx):
        return jnp.sum(jnp.where(lane == idx, x, 0.0), axis=1, keepdims=True)

    qs, ks, vs, betas, egs = [], [], [], [], []
    for t in range(n_tok):
        q = conv(q_ref, bq_ref, wq_ref, t)
        k = conv(k_ref, bk_ref, wk_ref, t)
        q = q * lax.rsqrt(jnp.sum(q * q, axis=-1, keepdims=True) + EPS) * (HEAD_DIM ** -0.5)
        k = k * lax.rsqrt(jnp.sum(k * k, axis=-1, keepdims=True) + EPS)
        qs.append(q.T)
        ks.append(k.T)
        vs.append(conv(v_ref, bv_ref, wv_ref, t))
        ba = ba_ref[t]
        betas.append(lane_col(jax.nn.sigmoid(ba), h))
        g = -jnp.exp(alog_ref[...]) * jax.nn.softplus(ba + dtb_ref[...])
        egs.append(jnp.exp(lane_col(g, GDN_HEADS + h)))

    o_rows = [[None] * nb for _ in range(n_tok)]
    for b in range(nb):
        s = s_ref[b]
        for t in range(n_tok):
            kc = ks[t][:, b:b + 1]
            qc = qs[t][:, b:b + 1]
            s = s * egs[t][b:b + 1, :]
            ks_row = jnp.sum(kc * s, axis=0, keepdims=True)
            delta = (vs[t][b:b + 1, :] - ks_row) * betas[t][b:b + 1, :]
            s = s + kc * delta
            o_rows[t][b] = jnp.sum(qc * s, axis=0, keepdims=True)
        so_ref[b] = s
    for t in range(n_tok):
        o = jnp.concatenate(o_rows[t], axis=0)
        o_ref[t] = _rms(o, onorm_ref[...]) * _silu(z_ref[t])


def _gdn_sample(xa, ba, conv_buf, conv_w, alog_row, dtb_row, onorm, state):
    n_tok, n_seq, _ = xa.shape
    nb = GDN_S_SEQ
    hq, hk, hv, hz = 0, GDN_HEADS, 2 * GDN_HEADS, 3 * GDN_HEADS

    def xspec(off, rows):
        return pl.BlockSpec((rows, nb, HEAD_DIM), lambda i, h: (0, i, off + h))

    def wspec(off):
        return pl.BlockSpec((GDN_CONV, HEAD_DIM), lambda i, h: (0, off + h))

    row = pl.BlockSpec((1, LANES), lambda i, h: (0, 0))
    sspec = pl.BlockSpec((nb, None, HEAD_DIM, HEAD_DIM), lambda i, h: (i, h, 0, 0))
    return pl.pallas_call(
        functools.partial(_gdn_sample_kernel, n_tok=n_tok),
        grid=(n_seq // nb, GDN_HEADS),
        in_specs=[xspec(hq, n_tok), xspec(hk, n_tok), xspec(hv, n_tok), xspec(hz, n_tok),
                  pl.BlockSpec((n_tok, nb, LANES), lambda i, h: (0, i, 0)),
                  xspec(hq, GDN_CONV - 1), xspec(hk, GDN_CONV - 1), xspec(hv, GDN_CONV - 1),
                  wspec(hq), wspec(hk), wspec(hv), row, row, row, sspec],
        out_specs=[pl.BlockSpec((n_tok, nb, HEAD_DIM), lambda i, h: (0, i, h)), sspec],
        out_shape=[jax.ShapeDtypeStruct((n_tok, n_seq, GDN_V), f32),
                   jax.ShapeDtypeStruct(state.shape, f32)],
        compiler_params=_params("parallel", "arbitrary"),
        name="gdn_sample",
    )(xa, xa, xa, xa, ba, conv_buf, conv_buf, conv_buf, conv_w, conv_w, conv_w, alog_row, dtb_row, onorm, state)


def _bucket_thresholds():
    exact = N_BUCKETS // 2
    d = np.arange(0, 2 * MAX_DISTANCE)
    val = np.log(np.maximum(d, 1).astype(np.float64) / exact) / math.log(MAX_DISTANCE / exact) * (N_BUCKETS - exact)
    frac = np.abs(val - np.round(val))[exact + 1:MAX_DISTANCE]
    assert frac.min() > 1e-3, "a bucket boundary sits on an integer distance"
    bucket = np.where(d < exact, d, np.minimum(exact + val.astype(np.int64), N_BUCKETS - 1))
    assert np.all(np.diff(bucket) >= 0)
    return [int(np.argmax(bucket >= b)) for b in range(N_BUCKETS)]


_BUCKET_THR = _bucket_thresholds()


def _bias_from_dist(dist, rb_ref, h):
    v = jnp.full(dist.shape, rb_ref[0, h], f32)
    for b in range(1, N_BUCKETS):
        v = jnp.where(dist >= _BUCKET_THR[b], rb_ref[b, h], v)
    return v


def _topk_mask(s, blk, k):
    nblk = s.shape[1]
    sel = jnp.zeros(s.shape, f32)
    for _ in range(k):
        m = jnp.max(s, axis=1, keepdims=True)
        cand = jnp.where((s == m) & (m > -jnp.inf), blk, nblk)
        pick = blk == jnp.min(cand, axis=1, keepdims=True)
        sel = jnp.where(pick, 1.0, sel)
        s = jnp.where(pick, -jnp.inf, s)
    return sel


def _block_mean_kernel(k_ref, o_ref):
    o_ref[0] = jnp.mean(k_ref[...], axis=0, keepdims=True)


def _block_mean(proj_b, n_blk):
    return pl.pallas_call(
        _block_mean_kernel,
        grid=(n_blk,),
        in_specs=[pl.BlockSpec((MOBA_BLOCK, MOBA_W), lambda i: (i, B_MK // MOBA_W))],
        out_specs=pl.BlockSpec((1, 1, MOBA_W), lambda i: (i, 0, 0)),
        out_shape=jax.ShapeDtypeStruct((n_blk, 1, MOBA_W), f32),
        compiler_params=_params("parallel"),
        name="moba_block_mean",
    )(proj_b)


def _moba_select_kernel(q_ref, km_ref, o_ref):
    own = pl.program_id(0)
    nblk = km_ref.shape[0]
    blk = lax.broadcasted_iota(i32, (MOBA_BLOCK, nblk), 1)
    outs = []
    for h in range(MOBA_HEADS):
        hs = slice(h * HEAD_DIM, (h + 1) * HEAD_DIM)
        s = _dot3(_split_bf16(q_ref[:, hs]), _split_bf16(km_ref[:, hs]), dot=_dot_nt)
        s = jnp.where(blk < own, s, -jnp.inf)
        outs.append(_topk_mask(s, blk, MOBA_TOPK))
    pad = LANES - MOBA_HEADS * nblk
    if pad:
        outs.append(jnp.zeros((MOBA_BLOCK, pad), f32))
    o_ref[...] = jnp.concatenate(outs, axis=1)


def _moba_select(proj_b, kmean, n_blk):
    assert n_blk * MOBA_HEADS <= LANES
    return pl.pallas_call(
        _moba_select_kernel,
        grid=(n_blk,),
        in_specs=[pl.BlockSpec((MOBA_BLOCK, MOBA_W), lambda i: (i, B_MQ // MOBA_W)),
                  pl.BlockSpec((n_blk, MOBA_W), lambda i: (0, 0))],
        out_specs=pl.BlockSpec((MOBA_BLOCK, LANES), lambda i: (i, 0)),
        out_shape=jax.ShapeDtypeStruct((n_blk * MOBA_BLOCK, LANES), f32),
        compiler_params=_params("parallel"),
        name="moba_select",
    )(proj_b, kmean)


def _moba_prompt_kernel(qi_ref, kj_ref, rb_ref, q_ref, k_ref, v_ref, sel_ref, o_ref,
                        bias_ref, m_ref, l_ref, acc_ref, *, n_blk):
    step = pl.program_id(0)
    qi = qi_ref[step]
    kj = kj_ref[step]
    nq = MOBA_BLOCK

    @pl.when(step == 0)
    def _():
        r = lax.broadcasted_iota(i32, (nq, nq), 0)
        c = lax.broadcasted_iota(i32, (nq, nq), 1)
        for h in range(MOBA_HEADS):
            bias_ref[h, 0] = jnp.where(c <= r, _bias_from_dist(r - c, rb_ref, h), NEG_INF)
            bias_ref[h, 1] = _bias_from_dist(r - c + nq, rb_ref, h)

    first = kj == qi

    @pl.when(first)
    def _():
        m_ref[...] = jnp.full(m_ref.shape, NEG_INF, f32)
        l_ref[...] = jnp.zeros_like(l_ref)
        acc_ref[...] = jnp.zeros_like(acc_ref)

    heads = range(MOBA_HEADS)
    hcols = [slice(h * HEAD_DIM, (h + 1) * HEAD_DIM) for h in heads]
    scale = HEAD_DIM ** -0.5

    def attend(far):
        sel = sel_ref[...]
        lane = lax.broadcasted_iota(i32, sel.shape, 1)
        ss = [_dot_nt(q_ref[:, hs].astype(bf16), k_ref[:, hs].astype(bf16)) for hs in hcols]
        picked = [jnp.max(jnp.where(lane == h * n_blk + kj, sel, 0.0), axis=1, keepdims=True) > 0.0 for h in heads]
        if far:
            cols = [jnp.where(pk, rb_ref[N_BUCKETS - 1, h], NEG_INF) for h, pk in zip(heads, picked)]
            ss = [s * scale + c for s, c in zip(ss, cols)]
        else:
            slot = jnp.where(first, 0, 1)
            ss = [s * scale + bias_ref[h, slot] for h, s in zip(heads, ss)]
            ss = [jnp.where(pk | first, s, NEG_INF) for pk, s in zip(picked, ss)]
        m_prevs = [m_ref[h] for h in heads]
        m_news = [jnp.maximum(mp, jnp.max(s, axis=1, keepdims=True)) for mp, s in zip(m_prevs, ss)]
        alphas = [jnp.exp(mp - mn) for mp, mn in zip(m_prevs, m_news)]
        ps = [jnp.exp(s - mn) for s, mn in zip(ss, m_news)]
        pvs = [_dot(p.astype(bf16), v_ref[:, hs].astype(bf16)) for p, hs in zip(ps, hcols)]
        for h in heads:
            l_ref[h] = alphas[h] * l_ref[h] + jnp.sum(ps[h], axis=1, keepdims=True)
            acc_ref[h] = alphas[h] * acc_ref[h] + pvs[h]
            m_ref[h] = m_news[h]

    is_far = kj < qi - 1
    pl.when(is_far)(lambda: attend(True))
    pl.when(jnp.logical_not(is_far))(lambda: attend(False))

    @pl.when(kj == 0)
    def _():
        for h in range(MOBA_HEADS):
            hs = slice(h * HEAD_DIM, (h + 1) * HEAD_DIM)
            o_ref[:, hs] = (acc_ref[h] / l_ref[h]).astype(o_ref.dtype)


def _moba_prompt(qkv, sel, rel_bias, n_blk):
    qi = np.concatenate([np.full(i + 1, i) for i in range(n_blk)]).astype(np.int32)
    kj = np.concatenate([np.arange(i, -1, -1) for i in range(n_blk)]).astype(np.int32)
    nq = MOBA_BLOCK
    grid_spec = pltpu.PrefetchScalarGridSpec(
        num_scalar_prefetch=2,
        grid=(len(qi),),
        in_specs=[
            pl.BlockSpec(memory_space=pltpu.SMEM),
            pl.BlockSpec((nq, MOBA_W), lambda s, qi, kj: (qi[s], 0)),
            pl.BlockSpec((nq, MOBA_W), lambda s, qi, kj: (kj[s], 1)),
            pl.BlockSpec((nq, MOBA_W), lambda s, qi, kj: (kj[s], 2)),
            pl.BlockSpec((nq, LANES), lambda s, qi, kj: (qi[s], 0)),
        ],
        out_specs=pl.BlockSpec((nq, MOBA_W), lambda s, qi, kj: (qi[s], 0)),
        scratch_shapes=[
            pltpu.VMEM((MOBA_HEADS, 2, nq, nq), f32),
            pltpu.VMEM((MOBA_HEADS, nq, 1), f32),
            pltpu.VMEM((MOBA_HEADS, nq, 1), f32),
            pltpu.VMEM((MOBA_HEADS, nq, HEAD_DIM), f32),
        ],
    )
    return pl.pallas_call(
        functools.partial(_moba_prompt_kernel, n_blk=n_blk),
        grid_spec=grid_spec,
        out_shape=jax.ShapeDtypeStruct((n_blk * nq, MOBA_W), bf16),
        compiler_params=_params("arbitrary"),
        name="moba_prompt",
    )(jnp.asarray(qi), jnp.asarray(kj), rel_bias, qkv, qkv, qkv, sel)


def _per_head_bias(dist, row_head, rb_ref):
    v = _bias_from_dist(dist, rb_ref, 0)
    for h in range(1, MOBA_HEADS):
        v = jnp.where(row_head == h, _bias_from_dist(dist, rb_ref, h), v)
    return v


def _moba_sample_kernel(pt_ref, rb_ref, q_ref, kn_ref, vn_ref, *rest, n_pages, page):
    kp = rest[:n_pages]
    vp = rest[n_pages:2 * n_pages]
    o_ref, bias_ref, biasn_ref = rest[2 * n_pages:]
    nh = MOBA_HEADS
    rows = q_ref.shape[0]
    prow = page * nh
    past = n_pages * page
    n_blk = past // MOBA_BLOCK
    ppb = MOBA_BLOCK // page

    @pl.when(pl.program_id(0) == 0)
    def _():
        r = lax.broadcasted_iota(i32, (rows, past * nh), 0)
        c = lax.broadcasted_iota(i32, (rows, past * nh), 1)
        bias = _per_head_bias(past + r // nh - c // nh, r % nh, rb_ref)
        bias_ref[...] = jnp.where(r % nh == c % nh, bias, NEG_INF)
        r = lax.broadcasted_iota(i32, (rows, rows), 0)
        c = lax.broadcasted_iota(i32, (rows, rows), 1)
        bias = _per_head_bias(r // nh - c // nh, r % nh, rb_ref)
        biasn_ref[...] = jnp.where((r % nh == c % nh) & (c // nh <= r // nh), bias, NEG_INF)

    kb, means = [], []
    for p in range(n_pages):
        kpage = kp[p][...]
        kb.append(kpage.astype(bf16))
        part = jnp.sum(kpage.reshape(prow // SUBLANES, SUBLANES, HEAD_DIM), axis=0)
        part = part[0:nh] + part[nh:2 * nh]
        if p % ppb == 0:
            means.append(part)
        else:
            means[-1] = means[-1] + part
    kmean = jnp.concatenate(means, axis=0) / MOBA_BLOCK

    q = q_ref[...]
    r = lax.broadcasted_iota(i32, (rows, n_blk * nh), 0)
    c = lax.broadcasted_iota(i32, (rows, n_blk * nh), 1)
    s = jnp.where(r % nh == c % nh, _dot_nt(q, kmean, precision=HIGHEST), -jnp.inf)
    sel = _topk_mask(s, c, MOBA_TOPK)
    picked = [jnp.max(jnp.where(c // nh == n, sel, 0.0), axis=1, keepdims=True) > 0.0 for n in range(n_blk)]

    qb = q.astype(bf16)
    scale = HEAD_DIM ** -0.5
    ln = _dot_nt(qb, kn_ref[...].astype(bf16)) * scale + biasn_ref[...]
    m = jnp.max(ln, axis=1, keepdims=True)
    lps = []
    for p in range(n_pages):
        lp = _dot_nt(qb, kb[p]) * scale + bias_ref[:, p * prow:(p + 1) * prow]
        lp = jnp.where(picked[p // ppb], lp, NEG_INF)
        m = jnp.maximum(m, jnp.max(lp, axis=1, keepdims=True))
        lps.append(lp)
    pn = jnp.exp(ln - m)
    den = jnp.sum(pn, axis=1, keepdims=True)
    num = _dot(pn.astype(bf16), vn_ref[...].astype(bf16))
    for p in range(n_pages):
        pp = jnp.exp(lps[p] - m)
        den = den + jnp.sum(pp, axis=1, keepdims=True)
        num = num + _dot(pp.astype(bf16), vp[p][...].astype(bf16))
    o_ref[...] = num / den


def _moba_sample(q, k_new, v_new, pool_k, pool_v, page_table, rel_bias, *, page):
    n_seq, rows, _ = q.shape
    n_pages = page_table.shape[1]
    prow = page * MOBA_HEADS
    new_spec = pl.BlockSpec((None, rows, HEAD_DIM), lambda b, pt: (b, 0, 0))

    def page_spec(p):
        return pl.BlockSpec((prow, HEAD_DIM), lambda b, pt: (pt[b, p], 0))

    grid_spec = pltpu.PrefetchScalarGridSpec(
        num_scalar_prefetch=1,
        grid=(n_seq,),
        in_specs=[pl.BlockSpec(memory_space=pltpu.SMEM), new_spec, new_spec, new_spec]
        + [page_spec(p) for p in range(n_pages)] * 2,
        out_specs=new_spec,
        scratch_shapes=[pltpu.VMEM((rows, n_pages * prow), f32), pltpu.VMEM((rows, rows), f32)],
    )
    return pl.pallas_call(
        functools.partial(_moba_sample_kernel, n_pages=n_pages, page=page),
        grid_spec=grid_spec,
        out_shape=jax.ShapeDtypeStruct((n_seq, rows, HEAD_DIM), f32),
        compiler_params=_params("arbitrary"),
        name="moba_sample",
    )(page_table, rel_bias, q, k_new, v_new, *([pool_k] * n_pages), *([pool_v] * n_pages))


def _mem_attn_kernel(q_ref, k_ref, v_ref, o_ref):
    for h in range(MEM_HEADS):
        hs = slice(h * HEAD_DIM, (h + 1) * HEAD_DIM)
        s = _dot_nt(q_ref[:, hs].astype(bf16), k_ref[:, hs].astype(bf16)) * (HEAD_DIM ** -0.5)
        p = jnp.exp(s - jnp.max(s, axis=1, keepdims=True))
        num = _dot(p.astype(bf16), v_ref[:, hs].astype(bf16))
        o_ref[:, hs] = (num / jnp.sum(p, axis=1, keepdims=True)).astype(o_ref.dtype)


def _mem_attn_prompt(proj_b, mem_kv, *, n_tok, tq):
    n_mem = mem_kv.shape[0]
    return pl.pallas_call(
        _mem_attn_kernel,
        grid=(n_tok // tq,),
        in_specs=[pl.BlockSpec((tq, MEM_W), lambda i: (i, B_CQ // MEM_W)),
                  pl.BlockSpec((n_mem, MEM_W), lambda i: (0, 0)),
                  pl.BlockSpec((n_mem, MEM_W), lambda i: (0, 1))],
        out_specs=pl.BlockSpec((tq, MEM_W), lambda i: (i, 0)),
        out_shape=jax.ShapeDtypeStruct((n_tok, MEM_W), bf16),
        compiler_params=_params("parallel"),
        name="mem_attn_prompt",
    )(proj_b, mem_kv, mem_kv)


def _mem_attn_sample_kernel(q_ref, k_ref, v_ref, o_ref):
    for b in range(q_ref.shape[0]):
        s = _dot_nt(q_ref[b].astype(bf16), k_ref[b].astype(bf16)) * (HEAD_DIM ** -0.5)
        r = lax.broadcasted_iota(i32, s.shape, 0)
        c = lax.broadcasted_iota(i32, s.shape, 1)
        s = jnp.where(r % MEM_HEADS == c % MEM_HEADS, s, NEG_INF)
        p = jnp.exp(s - jnp.max(s, axis=1, keepdims=True))
        o_ref[b] = _dot(p.astype(bf16), v_ref[b].astype(bf16)) / jnp.sum(p, axis=1, keepdims=True)


MEM_SEQ_PER_STEP = 8


def _mem_attn_sample(q, mem_k, mem_v):
    n_seq, rows, _ = q.shape
    mrows = mem_k.shape[1]
    nb = MEM_SEQ_PER_STEP
    return pl.pallas_call(
        _mem_attn_sample_kernel,
        grid=(n_seq // nb,),
        in_specs=[pl.BlockSpec((nb, rows, HEAD_DIM), lambda b: (b, 0, 0)),
                  pl.BlockSpec((nb, mrows, HEAD_DIM), lambda b: (b, 0, 0)),
                  pl.BlockSpec((nb, mrows, HEAD_DIM), lambda b: (b, 0, 0))],
        out_specs=pl.BlockSpec((nb, rows, HEAD_DIM), lambda b: (b, 0, 0)),
        out_shape=jax.ShapeDtypeStruct((n_seq, rows, HEAD_DIM), f32),
        compiler_params=_params("parallel"),
        name="mem_attn_sample",
    )(q, mem_k, mem_v)


def _outproj_router_kernel(*refs, starts):
    n_g = len(starts)
    i = pl.program_id(0)
    x, og, om, oc = [_stacked_tile(refs[k * n_g:(k + 1) * n_g], starts, i) for k in range(4)]
    w_ref, g_ref, rw_ref, rb_ref, y_ref, h_ref, ti_ref, tw_ref = refs[4 * n_g:]
    mix = (_dot(og, w_ref[0:GDN_V, :])
           + _dot(om, w_ref[GDN_V:GDN_V + MOBA_W, :])
           + _dot(oc, w_ref[GDN_V + MOBA_W:, :]))
    y = x + mix
    y_ref[...] = y
    h = _rms(y, g_ref[...])
    _store_slabs(h_ref, _pack_bf16_pairs(h))
    h_hi, h_lo = _split_bf16(h)
    w_hi, w_lo = _split_bf16(rw_ref[...])
    logits = _dot(h_hi, w_hi) + (_dot(h_hi, w_lo) + _dot(h_lo, w_hi)) + rb_ref[...]
    lane = lax.broadcasted_iota(i32, logits.shape, 1)
    s = jnp.where(lane < N_EXPERTS, logits, -jnp.inf)
    vals, idxs = [], []
    for _ in range(TOP_K):
        m = jnp.max(s, axis=1, keepdims=True)
        idx = jnp.min(jnp.where(s == m, lane, LANES), axis=1, keepdims=True)
        vals.append(m)
        idxs.append(idx)
        s = jnp.where(lane == idx, -jnp.inf, s)
    exps = [jnp.exp(v - vals[0]) for v in vals]
    den = exps[0]
    for e in exps[1:]:
        den = den + e
    ti = jnp.zeros(logits.shape, i32)
    tw = jnp.zeros(logits.shape, f32)
    for r in range(TOP_K):
        ti = jnp.where(lane == r, idxs[r], ti)
        tw = jnp.where(lane == r, exps[r] / den, tw)
    ti_ref[...] = ti
    tw_ref[...] = tw


def _outproj_router(xs, ogs, oms, ocs, w_out, g, rw, rb, *, tm):
    d = xs[0].shape[1]
    row = lambda w: pl.BlockSpec((tm, w), lambda i: (i, 0))
    full = lambda a: pl.BlockSpec(a.shape, lambda i: (0,) * a.ndim)
    specs, starts, n_tiles = [], None, None
    for group in (xs, ogs, oms, ocs):
        group_specs, starts, n_tiles = _stacked_specs(group, tm, 1)
        specs += group_specs
    n = n_tiles * tm
    return pl.pallas_call(
        functools.partial(_outproj_router_kernel, starts=starts),
        grid=(n_tiles,),
        in_specs=specs + [full(w_out), full(g), full(rw), full(rb)],
        out_specs=[row(d), pl.BlockSpec((tm * PACKED_G, LANES), lambda i: (i, 0)), row(LANES), row(LANES)],
        out_shape=[jax.ShapeDtypeStruct((n, d), f32), jax.ShapeDtypeStruct((n * PACKED_G, LANES), jnp.uint32),
                   jax.ShapeDtypeStruct((n, LANES), i32), jax.ShapeDtypeStruct((n, LANES), f32)],
        compiler_params=_params("parallel"),
        name="outproj_router",
    )(*xs, *ogs, *oms, *ocs, w_out, g, rw, rb)


GATHER_UNROLL = 8
SLAB_G = D_MODEL // LANES
PACKED_G = SLAB_G // 2


def _store_slabs(slab_ref, x, mask=None):
    rows = x.shape[0]
    g = x.shape[1] // LANES
    for c in range(g):
        idx = (pl.ds(c, rows, stride=g), slice(None))
        blk = x[:, c * LANES:(c + 1) * LANES]
        slab_ref[idx] = blk if mask is None else jnp.where(mask, blk, slab_ref[idx])


def _pack_bf16_pairs(x):
    half = x.shape[1] // 2
    bits = lambda v: lax.bitcast_convert_type(v.astype(bf16).astype(f32), jnp.uint32)
    return (bits(x[:, :half]) >> 16) | (bits(x[:, half:]) & jnp.uint32(0xFFFF0000))


def _unpack_bf16_pair(word):
    as_bf16 = lambda w: lax.bitcast_convert_type(w, f32).astype(bf16)
    return as_bf16(word << 16), as_bf16(word & jnp.uint32(0xFFFF0000))


def _load_slab_group(slab_ref, lead, c, rows, g):
    return slab_ref[lead + (pl.ds(c, rows, stride=g), slice(None))]


def _issue_slab_gather(idx_ref, idx_base, src_ref, dst_ref, sem, count, g):
    assert count % GATHER_UNROLL == 0

    def body(jj, carry):
        for u in range(GATHER_UNROLL):
            j = jj * GATHER_UNROLL + u
            tok = pl.multiple_of(idx_ref[idx_base + j] * g, g)
            dst = dst_ref.at[pl.ds(pl.multiple_of(j * g, g), g)]
            pltpu.make_async_copy(src_ref.at[pl.ds(tok, g)], dst, sem).start(priority=u % 2)
        return carry

    lax.fori_loop(0, count // GATHER_UNROLL, body, 0)


def _wait_slab_gather(src_ref, dst_ref, sem):
    pltpu.make_async_copy(src_ref.at[pl.ds(0, dst_ref.shape[0])], dst_ref, sem).wait()


def _gather_x_kernel(idx_ref, src_ref, o_ref, buf_ref, sem, *, rows, g):
    i = pl.program_id(0)
    slot = i % 2

    @pl.when(i == 0)
    def _():
        _issue_slab_gather(idx_ref, 0, src_ref, buf_ref.at[0], sem.at[0], rows, g)

    @pl.when(i + 1 < pl.num_programs(0))
    def _():
        _issue_slab_gather(idx_ref, (i + 1) * rows, src_ref, buf_ref.at[1 - slot], sem.at[1 - slot], rows, g)

    _wait_slab_gather(src_ref, buf_ref.at[slot], sem.at[slot])
    for c in range(g):
        lo, hi = _unpack_bf16_pair(_load_slab_group(buf_ref, (slot,), c, rows, g))
        o_ref[:, c * LANES:(c + 1) * LANES] = lo
        o_ref[:, (g + c) * LANES:(g + c + 1) * LANES] = hi


def _gather_x(src, idx, *, rows, g):
    n_out = idx.shape[0]
    grid_spec = pltpu.PrefetchScalarGridSpec(
        num_scalar_prefetch=1,
        grid=(n_out // rows,),
        in_specs=[pl.BlockSpec(memory_space=pl.ANY)],
        out_specs=pl.BlockSpec((rows, 2 * g * LANES), lambda i, idx: (i, 0)),
        scratch_shapes=[pltpu.VMEM((2, rows * g, LANES), src.dtype), pltpu.SemaphoreType.DMA((2,))],
    )
    return pl.pallas_call(
        functools.partial(_gather_x_kernel, rows=rows, g=g),
        grid_spec=grid_spec,
        out_shape=jax.ShapeDtypeStruct((n_out, 2 * g * LANES), bf16),
        compiler_params=_params("arbitrary"),
        name="moe_gather_x",
    )(idx, src)


def _new_expert(ev_ref, v):
    return (v == 0) | (ev_ref[v] != ev_ref[jnp.maximum(v - 1, 0)])


def _expert_weights_step(copies, is_new, run, n_runs, e_here, e_next, e_first, sweep, n_sweeps, on_arrival):
    slot = (run + sweep * n_runs) % 2

    @pl.when(is_new)
    def _():
        @pl.when((sweep == 0) & (run == 0))
        def _():
            for c in copies(e_here, sweep, slot):
                c.start()

        for c in copies(e_here, sweep, slot):
            c.wait()
        on_arrival(slot)
        last = run == n_runs - 1

        @pl.when(jnp.logical_not(last))
        def _():
            for c in copies(e_next, sweep, 1 - slot):
                c.start()

        @pl.when(last & (sweep + 1 < n_sweeps))
        def _():
            for c in copies(e_first, sweep + 1, 1 - slot):
                c.start()


def _moe_gate_up_kernel(tv_ref, ev_ref, lo_ref, hi_ref, run_ref, nxt_ref, nrun_ref, x_ref, w_hbm, b_ref,
                        o_ref, wf_ref, wb_ref, sem, *, tf):
    v = pl.program_id(0)
    lo = lo_ref[v]
    hi = hi_ref[v]

    def copies(e, sweep, slot):
        return [pltpu.make_async_copy(w_hbm.at[e], wf_ref, sem.at[0])]

    def on_arrival(slot):
        wb_ref[...] = wf_ref[...].astype(bf16)

    _expert_weights_step(copies, _new_expert(ev_ref, v), run_ref[v], nrun_ref[0], ev_ref[v], nxt_ref[v], ev_ref[0],
                         0, 1, on_arrival)

    def activations(c0):
        x = x_ref[...]
        gate = _dot(x, wb_ref[:, c0:c0 + tf]) + b_ref[:, c0:c0 + tf]
        up = _dot(x, wb_ref[:, D_FF + c0:D_FF + c0 + tf]) + b_ref[:, D_FF + c0:D_FF + c0 + tf]
        gate = jnp.minimum(gate, SWIGLU_LIMIT)
        up = jnp.clip(up, -SWIGLU_LIMIT, SWIGLU_LIMIT)
        return (gate * jax.nn.sigmoid(SWIGLU_ALPHA * gate) * (up + 1.0)).astype(o_ref.dtype)

    @pl.when(hi > lo)
    def _():
        @pl.when(lo == 0)
        def _():
            for c0 in range(0, D_FF, tf):
                o_ref[:, c0:c0 + tf] = activations(c0)

        @pl.when(lo > 0)
        def _():
            r = lax.broadcasted_iota(i32, (o_ref.shape[0], tf), 0)
            mine = (r >= lo) & (r < hi)
            for c0 in range(0, D_FF, tf):
                o_ref[:, c0:c0 + tf] = jnp.where(mine, activations(c0), o_ref[:, c0:c0 + tf])


def _moe_gate_up(xs, w_gu, b_gu3, visits, *, tm, tf):
    rows, d = xs.shape
    grid_spec = pltpu.PrefetchScalarGridSpec(
        num_scalar_prefetch=len(visits),
        grid=(visits[0].shape[0],),
        in_specs=[
            pl.BlockSpec((tm, d), lambda v, tv, *_: (tv[v], 0)),
            pl.BlockSpec(memory_space=pl.ANY),
            pl.BlockSpec((None, 1, 2 * D_FF), lambda v, tv, ev, *_: (ev[v], 0, 0)),
        ],
        out_specs=pl.BlockSpec((tm, D_FF), lambda v, tv, *_: (tv[v], 0)),
        scratch_shapes=[pltpu.VMEM((d, 2 * D_FF), f32), pltpu.VMEM((d, 2 * D_FF), bf16),
                        pltpu.SemaphoreType.DMA((1,))],
    )
    return pl.pallas_call(
        functools.partial(_moe_gate_up_kernel, tf=tf),
        grid_spec=grid_spec,
        out_shape=jax.ShapeDtypeStruct((rows, D_FF), bf16),
        compiler_params=_params("arbitrary"),
        name="moe_gate_up",
    )(*visits, xs, w_gu, b_gu3)


def _moe_down_kernel(tv_ref, ev_ref, lo_ref, hi_ref, run_ref, nxt_ref, nrun_ref, a_ref, w_hbm, b_ref,
                     o_ref, wf_ref, wb_ref, sem):
    v = pl.program_id(0)
    lo = lo_ref[v]
    hi = hi_ref[v]

    def copies(e, sweep, slot):
        return [pltpu.make_async_copy(w_hbm.at[e], wf_ref.at[slot], sem.at[slot])]

    def on_arrival(slot):
        wb_ref[...] = wf_ref[slot].astype(bf16)

    _expert_weights_step(copies, _new_expert(ev_ref, v), run_ref[v], nrun_ref[0], ev_ref[v], nxt_ref[v], ev_ref[0],
                         0, 1, on_arrival)

    @pl.when(hi > lo)
    def _():
        @pl.when(lo == 0)
        def _():
            _store_slabs(o_ref, _dot(a_ref[...], wb_ref[...]) + b_ref[...])

        @pl.when(lo > 0)
        def _():
            r = lax.broadcasted_iota(i32, (a_ref.shape[0], LANES), 0)
            _store_slabs(o_ref, _dot(a_ref[...], wb_ref[...]) + b_ref[...], mask=(r >= lo) & (r < hi))


def _moe_down(act, w_dn, b_dn3, visits, *, tm):
    rows, dff = act.shape
    d = w_dn.shape[2]
    g = d // LANES
    grid_spec = pltpu.PrefetchScalarGridSpec(
        num_scalar_prefetch=len(visits),
        grid=(visits[0].shape[0],),
        in_specs=[
            pl.BlockSpec((tm, dff), lambda v, tv, *_: (tv[v], 0)),
            pl.BlockSpec(memory_space=pl.ANY),
            pl.BlockSpec((None, 1, d), lambda v, tv, ev, *_: (ev[v], 0, 0)),
        ],
        out_specs=pl.BlockSpec((tm * g, LANES), lambda v, tv, *_: (tv[v], 0)),
        scratch_shapes=[pltpu.VMEM((2, dff, d), f32), pltpu.VMEM((dff, d), bf16), pltpu.SemaphoreType.DMA((2,))],
    )
    return pl.pallas_call(
        _moe_down_kernel,
        grid_spec=grid_spec,
        out_shape=jax.ShapeDtypeStruct((rows * g, LANES), f32),
        compiler_params=_params("arbitrary"),
        name="moe_down",
    )(*visits, act, w_dn, b_dn3)


def _moe_combine_kernel(pos_ref, y_ref, tw_ref, ys_ref, oa_ref, ob_ref, buf_ref, sem, *, rows, g, tiles_a):
    i = pl.program_id(0)
    slot = i % 2
    n_tok = pl.num_programs(0) * rows

    def issue(tile, s):
        for k in range(TOP_K):
            _issue_slab_gather(pos_ref, k * n_tok + tile * rows, ys_ref, buf_ref.at[s, k], sem.at[s], rows, g)

    @pl.when(i == 0)
    def _():
        issue(0, 0)

    @pl.when(i + 1 < pl.num_programs(0))
    def _():
        issue(i + 1, 1 - slot)

    for k in range(TOP_K):
        _wait_slab_gather(ys_ref, buf_ref.at[slot, k], sem.at[slot])
    tw = tw_ref[...]

    def combine(o_ref):
        for c in range(g):
            cs = slice(c * LANES, (c + 1) * LANES)
            acc = y_ref[:, cs]
            for k in range(TOP_K):
                acc = acc + _load_slab_group(buf_ref, (slot, k), c, rows, g) * tw[:, k:k + 1]
            o_ref[:, cs] = acc

    pl.when(i < tiles_a)(lambda: combine(oa_ref))
    pl.when(i >= tiles_a)(lambda: combine(ob_ref))


def _moe_combine(y1, ys, pos_kmajor, tw, *, rows, n_a):
    n, d = y1.shape
    g = d // LANES
    tiles_a = n_a // rows
    tiles = n // rows
    grid_spec = pltpu.PrefetchScalarGridSpec(
        num_scalar_prefetch=1,
        grid=(tiles,),
        in_specs=[pl.BlockSpec((rows, d), lambda i, pos: (i, 0)),
                  pl.BlockSpec((rows, LANES), lambda i, pos: (i, 0)),
                  pl.BlockSpec(memory_space=pl.ANY)],
        out_specs=[pl.BlockSpec((rows, d), lambda i, pos: (jnp.minimum(i, tiles_a - 1), 0)),
                   pl.BlockSpec((rows, d), lambda i, pos: (jnp.maximum(i - tiles_a, 0), 0))],
        scratch_shapes=[pltpu.VMEM((2, TOP_K, rows * g, LANES), f32), pltpu.SemaphoreType.DMA((2,))],
    )
    return pl.pallas_call(
        functools.partial(_moe_combine_kernel, rows=rows, g=g, tiles_a=tiles_a),
        grid_spec=grid_spec,
        out_shape=[jax.ShapeDtypeStruct((n_a, d), f32), jax.ShapeDtypeStruct((n - n_a, d), f32)],
        compiler_params=_params("arbitrary"),
        name="moe_combine",
    )(pos_kmajor, y1, tw, ys)


def _sort_rows(top_i):
    e = top_i.reshape(-1)
    a = e.shape[0]
    order = jnp.argsort(e, stable=True).astype(i32)
    tok_sorted = order // TOP_K
    blk = LANES
    onehot = (e[:, None] == jnp.arange(N_EXPERTS, dtype=i32)[None, :]).astype(f32).reshape(a // blk, blk, N_EXPERTS)
    earlier = (jnp.arange(blk)[:, None] > jnp.arange(blk)[None, :]).astype(f32)
    within = jnp.einsum("ij,bjk->bik", earlier, onehot)
    block_sums = jnp.sum(onehot, axis=1)
    block_off = jnp.cumsum(block_sums, axis=0) - block_sums
    counts = jnp.sum(block_sums, axis=0)
    starts = jnp.cumsum(counts) - counts
    pos = jnp.sum(onehot * (within + block_off[:, None, :] + starts[None, None, :]), axis=2).reshape(a)
    return tok_sorted, pos.astype(i32), counts.astype(i32)


def _visits(counts, n_rows, *, tm):
    ends = jnp.cumsum(counts)
    starts = ends - counts
    first_tile = starts // tm
    last_tile = jnp.maximum(ends - 1, 0) // tm
    n_vis = jnp.where(counts > 0, last_tile - first_tile + 1, 0)
    vis_end = jnp.cumsum(n_vis)
    vis_start = vis_end - n_vis
    total = vis_end[-1]
    v = jnp.arange(n_rows // tm + N_EXPERTS, dtype=i32)
    vc = jnp.minimum(v, total - 1)
    ev = jnp.sum((vis_end[None, :] <= vc[:, None]).astype(i32), axis=1)
    tv = first_tile[ev] + (vc - vis_start[ev])
    lo = jnp.maximum(starts[ev], tv * tm) - tv * tm
    hi = jnp.minimum(ends[ev], (tv + 1) * tm) - tv * tm
    valid = v < total
    experts = jnp.arange(N_EXPERTS, dtype=i32)
    present = n_vis > 0
    n_runs = jnp.sum(present.astype(i32))
    run = jnp.sum((present[None, :] & (experts[None, :] < ev[:, None])).astype(i32), axis=1)
    rank = jnp.cumsum(present.astype(i32))
    run_expert = jnp.sum((rank[None, :] <= experts[:, None]).astype(i32), axis=1)
    nxt = run_expert[jnp.minimum(run + 1, n_runs - 1)]
    return tv, ev, jnp.where(valid, lo, 0), jnp.where(valid, hi, 0), run, nxt, n_runs[None]


GATE_UP_TM = 256
GATE_UP_TF = 1024
DOWN_TM = 256
GATHER_ROWS = 512
COMBINE_ROWS = 256


def _moe(y1, h_slabs, top_i, top_w, w_gu, b_gu, w_dn, b_dn, *, n_a):
    n_tok = y1.shape[0]
    tok_sorted, pos, counts = _sort_rows(top_i[:, :TOP_K])
    n_rows = tok_sorted.shape[0]
    xs = _gather_x(h_slabs, tok_sorted, rows=GATHER_ROWS, g=PACKED_G)
    act = _moe_gate_up(xs, w_gu, b_gu[:, None, :], _visits(counts, n_rows, tm=GATE_UP_TM),
                       tm=GATE_UP_TM, tf=GATE_UP_TF)
    ys = _moe_down(act, w_dn, b_dn[:, None, :], _visits(counts, n_rows, tm=DOWN_TM), tm=DOWN_TM)
    pos_kmajor = pos.reshape(n_tok, TOP_K).T.reshape(-1)
    return _moe_combine(y1, ys, pos_kmajor, top_w, rows=COMBINE_ROWS, n_a=n_a)


def kernel(x_prompt, x_sample, cache_moba_k, cache_moba_v, state_gdn, state_gdn_conv, cache_mem_k, cache_mem_v, page_table, mem_prompt, rel_bias, norm_mix, w_in, conv_w, a_log, dt_bias, gdn_o_norm, moba_q_norm, moba_k_norm, mem_q_norm, mem_norm, w_mem_kv, mem_k_norm, w_out, norm_ffn, router_w, router_b, w_gu, b_gu, w_dn, b_dn):
    assert x_prompt.shape[0] == 1 and all(a.shape[0] == 1 for a in (w_in, w_out, w_gu, w_dn, state_gdn))
    n_p = x_prompt.shape[1]
    n_seq, n_st = x_sample.shape[:2]
    n_s = n_seq * n_st
    d = x_prompt.shape[2]
    n_blk = n_p // MOBA_BLOCK
    xs = (x_prompt.reshape(n_p, d), x_sample.reshape(n_s, d))

    w = w_in[0]
    off_beta = A_W
    off_moba = off_beta + 2 * GDN_HEADS
    off_mem = off_moba + 3 * MOBA_W
    w_a = w[:, :A_W].astype(bf16)
    w_b = jnp.concatenate(
        [w[:, off_moba:off_moba + MOBA_W], w[:, off_moba + MOBA_W:off_moba + 2 * MOBA_W], w[:, off_mem:],
         w[:, off_moba + 2 * MOBA_W:off_mem], w[:, off_beta:off_moba],
         jnp.zeros((d, LANES - 2 * GDN_HEADS), w.dtype)], axis=1).astype(bf16)
    gain_b = jnp.concatenate([jnp.tile(moba_q_norm[0], MOBA_HEADS), jnp.tile(moba_k_norm[0], MOBA_HEADS),
                              jnp.tile(mem_q_norm[0], MEM_HEADS)])[None]
    proj_a = _norm_matmul(xs, norm_mix, w_a, gain_b, tm=512, tn=1024, n_norm=0)
    head_groups = lambda col: list(range(col // LANES, (col + MOBA_W) // LANES))
    proj_b, mk_p, mk_s, mv_p, mv_s, moba_qkv16 = _norm_matmul(
        xs, norm_mix, w_b, gain_b, tm=256, tn=B_W, n_norm=B_NORM_GROUPS, head_outs=(B_MK // LANES, B_MV // LANES),
        bf16_groups=head_groups(B_MQ) + head_groups(B_MK) + head_groups(B_MV))
    proj_b3 = proj_b[n_p:].reshape(n_seq, n_st, B_W)

    pad_row = lambda a: jnp.zeros((1, LANES), f32).at[0, GDN_HEADS:2 * GDN_HEADS].set(a[0])
    alog_row, dtb_row = pad_row(a_log), pad_row(dt_bias)
    o_gdn_p, p_gdn = _gdn_prompt(proj_a, proj_b, conv_w[0], alog_row, dtb_row, gdn_o_norm, n_tok=n_p)
    p_conv = proj_a[n_p - (GDN_CONV - 1):n_p, :GDN_CONV_DIM]
    xa_s = proj_a[n_p:].reshape(n_seq, n_st, A_W)
    o_gdn_s, s_gdn = _gdn_sample(jnp.swapaxes(xa_s, 0, 1), jnp.swapaxes(proj_b3[..., B_BA:], 0, 1),
                                 jnp.swapaxes(state_gdn_conv[0], 0, 1), conv_w[0], alog_row, dtb_row,
                                 gdn_o_norm, state_gdn[0])
    o_gdn_s = jnp.swapaxes(o_gdn_s, 0, 1).reshape(n_s, GDN_V).astype(bf16)
    s_conv = jnp.concatenate([state_gdn_conv[0], xa_s[..., :GDN_CONV_DIM]], axis=1)[:, n_st:]

    kmean = _block_mean(proj_b, n_blk).reshape(n_blk, MOBA_W)
    sel = _moba_select(proj_b, kmean, n_blk)
    o_moba_p = _moba_prompt(moba_qkv16, sel, rel_bias, n_blk)
    pairs = lambda col: proj_b3[..., col:col + MOBA_W].reshape(n_seq, n_st * MOBA_HEADS, HEAD_DIM)
    pair_rows = n_st * MOBA_HEADS
    o_moba_s = _moba_sample(pairs(B_MQ), mk_s.reshape(n_seq, pair_rows, HEAD_DIM),
                            mv_s.reshape(n_seq, pair_rows, HEAD_DIM), cache_moba_k.reshape(-1, HEAD_DIM),
                            cache_moba_v.reshape(-1, HEAD_DIM), page_table, rel_bias, page=cache_moba_k.shape[2])
    o_moba_s = o_moba_s.reshape(n_s, MOBA_W).astype(bf16)

    n_mem = mem_prompt.shape[1]
    mem_kv = _norm_matmul((mem_prompt[0],), mem_norm, w_mem_kv[0].astype(bf16),
                          jnp.tile(mem_k_norm[0], MEM_HEADS)[None], tm=n_mem, tn=2 * MEM_W, n_norm=MEM_HEADS)
    o_mem_p = _mem_attn_prompt(proj_b, mem_kv, n_tok=n_p, tq=512)
    o_mem_s = _mem_attn_sample(pairs(B_CQ), cache_mem_k.reshape(n_seq, n_mem * MEM_HEADS, HEAD_DIM),
                               cache_mem_v.reshape(n_seq, n_mem * MEM_HEADS, HEAD_DIM))
    o_mem_s = o_mem_s.reshape(n_s, MEM_W).astype(bf16)

    rw = jnp.pad(router_w[0], ((0, 0), (0, LANES - N_EXPERTS)))
    rb = jnp.pad(router_b, ((0, 0), (0, LANES - N_EXPERTS)))
    y1, h2, top_i, top_w = _outproj_router(
        xs, (o_gdn_p, o_gdn_s), (o_moba_p, o_moba_s), (o_mem_p, o_mem_s),
        w_out[0].astype(bf16), norm_ffn, rw, rb, tm=256)
    y_p, y_s = _moe(y1, h2, top_i, top_w, w_gu[0], b_gu[0], w_dn[0], b_dn[0], n_a=n_p)

    heads = lambda a, lead: a.reshape(lead + (MOBA_HEADS, HEAD_DIM))
    return (y_p.reshape(1, n_p, d), y_s.reshape(n_seq, n_st, d),
            heads(mk_p, (1, 1, n_p)), heads(mv_p, (1, 1, n_p)),
            p_gdn[None, None], p_conv[None, None],
            heads(mem_kv[:, :MEM_W], (1, 1, n_mem)), heads(mem_kv[:, MEM_W:], (1, 1, n_mem)),
            heads(mk_s, (1, n_seq, n_st)), heads(mv_s, (1, n_seq, n_st)),
            s_gdn[None], s_conv[None])
```
